```python
import jax, jax.numpy as jnp
from jax import lax
import numpy as np

D_MODEL = 4096
BATCH = 2
SEQ = 4096
DEPTH = 4

MIX_WIDTH = D_MODEL
HEAD_DIM = 128
CONV_WIDTH = D_MODEL // 4
CONV_K = 3
NSA_HEADS = (D_MODEL // 2) // HEAD_DIM
NSA_WIDTH = NSA_HEADS * HEAD_DIM
NSA_KV_HEADS = 4
NSA_REP = NSA_HEADS // NSA_KV_HEADS
NSA_BRANCHES = 3
KV_WIDTH = NSA_KV_HEADS * HEAD_DIM
CMP_LEN = 32
CMP_STRIDE = 16
SLC_BLOCK = 64
N_SELECT = 16
WINDOW = 512
Q_BLOCK = 128
SLC_Q_BLOCK = 64
GMLP_WIDTH = D_MODEL // 4
GMLP_GROUPS = GMLP_WIDTH // HEAD_DIM
GMLP_CHUNK = 128
D_FF = 4 * D_MODEL
PLE_DIM = 256
IN_COLS = 3 * CONV_WIDTH + NSA_WIDTH + 2 * NSA_BRANCHES * KV_WIDTH + NSA_HEADS * NSA_BRANCHES + 2 * GMLP_WIDTH
EPS = 1e-6
NEG_INF = -1e30
FORCE_SCORE = 1e9

kernel_name = 'hymba_style_conv_nsa_gmlp_trunk'


def rms_norm(x, gain):
    x32 = x.astype(jnp.float32)
    y = x32 * lax.rsqrt(jnp.mean(x32 * x32, axis=-1, keepdims=True) + EPS)
    return (y * gain.astype(jnp.float32)).astype(x.dtype)


def masked_softmax(s, mask):
    p = jax.nn.softmax(jnp.where(mask, s, NEG_INF), axis=-1)
    return jnp.where(mask, p, 0.0)


def alibi_slopes():
    h = np.arange(1, NSA_HEADS + 1, dtype=np.float32)
    slopes = np.power(np.float32(2.0), -8.0 * h / NSA_HEADS).astype(np.float32)
    return jnp.asarray(slopes).reshape(NSA_KV_HEADS, NSA_REP)


def short_conv_mixer(xa, gb, gc, conv_w):
    T = xa.shape[1]
    inner = gc * xa
    padded = jnp.pad(inner, ((0, 0), (CONV_K - 1, 0), (0, 0)))
    conv = sum(conv_w[k] * padded[:, k:k + T] for k in range(CONV_K))
    return gb * conv


def nsa_mixer(q, k_cmp, v_cmp, k_slc, v_slc, k_win, v_win, gate_logits,
              q_gain, k_gain, cmp_pos, cmp_w1, cmp_w2):
    B, T = q.shape[:2]
    G, R, Dh = NSA_KV_HEADS, NSA_REP, HEAD_DIM
    f32 = jnp.float32
    scale = Dh ** -0.5
    slopes = alibi_slopes()
    pos = jnp.arange(T, dtype=jnp.int32)
    q = rms_norm(q.reshape(B, T, G, R, Dh), q_gain)
    heads = lambda a: a.reshape(B, T, G, Dh)

    ratio = CMP_LEN // CMP_STRIDE
    n_cmp = T // CMP_STRIDE - ratio + 1

    def compress(k, pe, w1, w2):
        kc = k.reshape(B, T // CMP_STRIDE, CMP_STRIDE, G, Dh)
        blocks = jnp.concatenate([kc[:, i:i + n_cmp] for i in range(ratio)], axis=2)
        blocks = blocks + pe[None, None, :, None, :]
        hid = jax.nn.gelu(jnp.einsum('bnlgd,lde->bnge', blocks, w1))
        return jnp.einsum('bnge,ef->bngf', hid, w2)

    kc = rms_norm(compress(heads(k_cmp), cmp_pos[0], cmp_w1[0], cmp_w2[0]), k_gain[0])
    vc = compress(heads(v_cmp), cmp_pos[1], cmp_w1[1], cmp_w2[1])
    cmp_end = jnp.arange(n_cmp, dtype=jnp.int32) * CMP_STRIDE + CMP_LEN - 1
    dist_c = (pos[:, None] - cmp_end[None, :]).astype(f32)
    s_c = jnp.einsum('btgrd,bngd->bgrtn', q, kc, preferred_element_type=f32) * scale
    s_c = s_c - slopes[:, :, None, None] * dist_c
    p_cmp = masked_softmax(s_c, dist_c >= 0)
    o_cmp = jnp.einsum('bgrtn,bngd->btgrd', p_cmp.astype(vc.dtype), vc)

    n_slc = T // SLC_BLOCK
    jc = jnp.arange(n_cmp, dtype=jnp.int32)[:, None] * CMP_STRIDE
    bs = jnp.arange(n_slc, dtype=jnp.int32)[None, :] * SLC_BLOCK
    overlap = jnp.clip(jnp.minimum(jc + CMP_LEN, bs + SLC_BLOCK) - jnp.maximum(jc, bs), 0, None).astype(f32) / CMP_LEN
    imp = jnp.einsum('bgrtn,nm->bgtm', p_cmp, overlap)
    blk = jnp.arange(n_slc, dtype=jnp.int32)[None, :]
    qblk = (pos // SLC_BLOCK)[:, None]
    valid = blk <= qblk
    forced = (blk == 0) | (blk == qblk) | (blk == qblk - 1)
    imp = jnp.where(forced, FORCE_SCORE, jnp.where(valid, imp, NEG_INF))
    n_sel = min(N_SELECT, n_slc)
    _, idx = lax.top_k(imp, n_sel)

    kb = rms_norm(heads(k_slc), k_gain[1]).reshape(B, n_slc, SLC_BLOCK, G, Dh).transpose(0, 3, 1, 2, 4)
    vb = heads(v_slc).reshape(B, n_slc, SLC_BLOCK, G, Dh).transpose(0, 3, 1, 2, 4)
    n_qc = T // SLC_Q_BLOCK
    q_c = q.reshape(B, n_qc, SLC_Q_BLOCK, G, R, Dh).transpose(1, 0, 2, 3, 4, 5)
    idx_c = idx.reshape(B, G, n_qc, SLC_Q_BLOCK, n_sel).transpose(2, 0, 1, 3, 4)
    pos_c = pos.reshape(n_qc, SLC_Q_BLOCK)
    bi = jnp.arange(B)[:, None, None, None]
    gi = jnp.arange(G)[None, :, None, None]
    tok_off = jnp.arange(SLC_BLOCK, dtype=jnp.int32)

    def slc_block(args):
        qb, ib, tb = args
        kg = kb[bi, gi, ib]
        vg = vb[bi, gi, ib]
        s = jnp.einsum('bqgrd,bgqkid->bgrqki', qb, kg, preferred_element_type=f32) * scale
        kpos = ib[..., None] * SLC_BLOCK + tok_off
        dist = (tb[None, None, :, None, None] - kpos).astype(f32)[:, :, None]
        s = s - slopes[None, :, :, None, None, None] * dist
        mask = jnp.broadcast_to(dist >= 0, s.shape)
        shp = s.shape
        p = masked_softmax(s.reshape(shp[:4] + (-1,)), mask.reshape(shp[:4] + (-1,))).reshape(shp)
        return jnp.einsum('bgrqki,bgqkid->bqgrd', p.astype(vg.dtype), vg)

    o_slc = lax.map(slc_block, (q_c, idx_c, pos_c)).transpose(1, 0, 2, 3, 4, 5).reshape(B, T, G, R, Dh)

    n_qb = T // Q_BLOCK
    n_prev = WINDOW // Q_BLOCK
    band = WINDOW + Q_BLOCK

    def banded(a):
        ap = jnp.pad(a, ((0, 0), (WINDOW, 0), (0, 0), (0, 0))).reshape(B, n_qb + n_prev, Q_BLOCK, G, Dh)
        return jnp.concatenate([ap[:, j:j + n_qb] for j in range(n_prev + 1)], axis=2)

    kw = banded(rms_norm(heads(k_win), k_gain[2]))
    vw = banded(heads(v_win))
    qw = q.reshape(B, n_qb, Q_BLOCK, G, R, Dh)
    qpos = pos.reshape(n_qb, Q_BLOCK)
    kpos = jnp.arange(n_qb, dtype=jnp.int32)[:, None] * Q_BLOCK - WINDOW + jnp.arange(band, dtype=jnp.int32)[None, :]
    dist_w = qpos[:, :, None] - kpos[:, None, :]
    mask_w = (dist_w >= 0) & (dist_w < WINDOW) & (kpos[:, None, :] >= 0)
    s_w = jnp.einsum('bnqgrd,bnkgd->bgrnqk', qw, kw, preferred_element_type=f32) * scale
    s_w = s_w - slopes[:, :, None, None, None] * dist_w.astype(f32)
    p_w = masked_softmax(s_w, mask_w)
    o_win = jnp.einsum('bgrnqk,bnkgd->bnqgrd', p_w.astype(vw.dtype), vw).reshape(B, T, G, R, Dh)

    g = jax.nn.sigmoid(gate_logits.reshape(B, T, G, R, NSA_BRANCHES))
    out = g[..., 0:1] * o_cmp + g[..., 1:2] * o_slc + g[..., 2:3] * o_win
    return out.reshape(B, T, NSA_WIDTH)


def gmlp_mixer(u, v, sgu_gain, w_sp, b_sp):
    B, T = u.shape[:2]
    u = jax.nn.gelu(u)
    v = rms_norm(jax.nn.gelu(v), sgu_gain)
    nc = T // GMLP_CHUNK
    vc = v.reshape(B, nc, GMLP_CHUNK, GMLP_GROUPS, HEAD_DIM)
    causal = jnp.tril(jnp.ones((GMLP_CHUNK, GMLP_CHUNK), dtype=w_sp.dtype))
    spatial = jnp.einsum('gts,bcsge->bctge', w_sp * causal, vc) + b_sp.T[:, :, None]
    return u * spatial.reshape(B, T, GMLP_WIDTH)


def setup_inputs(seed: int = 0) -> dict:
    key = jax.random.key(seed)
    ks = jax.random.split(key, 21)
    f32 = jnp.float32

    def nrm(k, shape, scale):
        return jax.random.normal(k, shape, f32) * scale

    def gain(k, shape):
        return 1.0 + 0.01 * jax.random.normal(k, shape, f32)

    return {
        'x': nrm(ks[0], (BATCH, SEQ, D_MODEL), 1.0),
        'p': nrm(ks[1], (DEPTH, BATCH, SEQ, PLE_DIM), 1.0),
        'mix_norm': gain(ks[2], (DEPTH, D_MODEL)),
        'w_in': nrm(ks[3], (DEPTH, D_MODEL, IN_COLS), D_MODEL ** -0.5),
        'conv_w': nrm(ks[4], (DEPTH, CONV_K, CONV_WIDTH), CONV_K ** -0.5),
        'q_gain': gain(ks[5], (DEPTH, HEAD_DIM)),
        'k_gain': gain(ks[6], (DEPTH, NSA_BRANCHES, HEAD_DIM)),
        'cmp_pos': nrm(ks[7], (DEPTH, 2, CMP_LEN, HEAD_DIM), 0.1),
        'cmp_w1': nrm(ks[8], (DEPTH, 2, CMP_LEN, HEAD_DIM, HEAD_DIM), (CMP_LEN * HEAD_DIM) ** -0.5),
        'cmp_w2': nrm(ks[9], (DEPTH, 2, HEAD_DIM, HEAD_DIM), HEAD_DIM ** -0.5),
        'sgu_gain': gain(ks[10], (DEPTH, GMLP_WIDTH)),
        'w_sp': nrm(ks[11], (DEPTH, GMLP_GROUPS, GMLP_CHUNK, GMLP_CHUNK), GMLP_CHUNK ** -0.5),
        'b_sp': 1.0 + nrm(ks[12], (DEPTH, GMLP_GROUPS, GMLP_CHUNK), 0.01),
        'mix_out_norm': gain(ks[13], (DEPTH, MIX_WIDTH)),
        'w_out': nrm(ks[14], (DEPTH, MIX_WIDTH, D_MODEL), MIX_WIDTH ** -0.5),
        'mlp_norm': gain(ks[15], (DEPTH, D_MODEL)),
        'w_mlp_in': nrm(ks[16], (DEPTH, D_MODEL, D_FF), D_MODEL ** -0.5),
        'w_mlp_out': nrm(ks[17], (DEPTH, D_FF, D_MODEL), D_FF ** -0.5),
        'ple_norm': gain(ks[18], (DEPTH, D_MODEL)),
        'w_ple_proj': nrm(ks[19], (DEPTH, PLE_DIM, D_MODEL), PLE_DIM ** -0.5),
        'w_ple_gate': nrm(ks[20], (DEPTH, D_MODEL, D_MODEL), D_MODEL ** -0.5),
    }


def reference(x, p, mix_norm, w_in, conv_w, q_gain, k_gain, cmp_pos, cmp_w1, cmp_w2,
              sgu_gain, w_sp, b_sp, mix_out_norm, w_out, mlp_norm, w_mlp_in, w_mlp_out,
              ple_norm, w_ple_proj, w_ple_gate):
    sizes = [CONV_WIDTH] * 3 + [NSA_WIDTH] + [KV_WIDTH] * (2 * NSA_BRANCHES) + [NSA_HEADS * NSA_BRANCHES] + [GMLP_WIDTH] * 2
    offsets = np.cumsum(sizes)[:-1].tolist()
    h = x
    for i in range(DEPTH):
        xn = rms_norm(h, mix_norm[i])
        z = xn @ w_in[i]
        (a_x, a_b, a_c, q, k_cmp, v_cmp, k_slc, v_slc, k_win, v_win,
         gate_logits, g_u, g_v) = jnp.split(z, offsets, axis=-1)
        o_a = short_conv_mixer(a_x, a_b, a_c, conv_w[i])
        o_b = nsa_mixer(q, k_cmp, v_cmp, k_slc, v_slc, k_win, v_win, gate_logits,
                        q_gain[i], k_gain[i], cmp_pos[i], cmp_w1[i], cmp_w2[i])
        o_c = gmlp_mixer(g_u, g_v, sgu_gain[i], w_sp[i], b_sp[i])
        on = mix_out_norm[i]
        mixed = jnp.concatenate([
            rms_norm(o_a, on[:CONV_WIDTH]),
            rms_norm(o_b, on[CONV_WIDTH:CONV_WIDTH + NSA_WIDTH]),
            rms_norm(o_c, on[CONV_WIDTH + NSA_WIDTH:]),
        ], axis=-1)
        h = h + mixed @ w_out[i]
        hn = rms_norm(h, mlp_norm[i])
        h = h + jnp.square(jax.nn.relu(hn @ w_mlp_in[i])) @ w_mlp_out[i]
        gate = jax.nn.sigmoid(rms_norm(h, ple_norm[i]) @ w_ple_gate[i])
        h = h + gate * (p[i] @ w_ple_proj[i])
    return h
```

```python
import functools

import jax
import jax.numpy as jnp
import numpy as np
from jax import lax
from jax.experimental import pallas as pl
from jax.experimental.pallas import tpu as pltpu

D_MODEL = 4096
HEAD_DIM = 128
CONV_WIDTH = D_MODEL // 4
CONV_K = 3
NSA_HEADS = (D_MODEL // 2) // HEAD_DIM
NSA_WIDTH = NSA_HEADS * HEAD_DIM
NSA_KV_HEADS = 4
NSA_REP = NSA_HEADS // NSA_KV_HEADS
NSA_BRANCHES = 3
KV_WIDTH = NSA_KV_HEADS * HEAD_DIM
CMP_LEN = 32
CMP_STRIDE = 16
SLC_BLOCK = 64
SLC_SHIFT = 6
N_SELECT = 16
WINDOW = 512
GMLP_WIDTH = D_MODEL // 4
GMLP_GROUPS = GMLP_WIDTH // HEAD_DIM
GMLP_CHUNK = 128
D_FF = 4 * D_MODEL
PLE_DIM = 256
N_GATES = NSA_HEADS * NSA_BRANCHES
EPS = 1e-6
NEG_INF = -1e30
FORCE_SCORE = 1e9
BELOW_NEG_INF = -3e38

Z_GATE_OFF = 3 * CONV_WIDTH + NSA_WIDTH + 2 * NSA_BRANCHES * KV_WIDTH
Z_MAIN_COLS = Z_GATE_OFF + 2 * GMLP_WIDTH
Q_OFF = 3 * CONV_WIDTH
KCMP_OFF = Q_OFF + NSA_WIDTH
VCMP_OFF = KCMP_OFF + KV_WIDTH
KV4_OFF = VCMP_OFF + KV_WIDTH
GU_OFF = Z_GATE_OFF
GV_OFF = GU_OFF + GMLP_WIDTH

LANES = 128
VMEM_LIMIT_BYTES = 56 * 1024 * 1024


def _cparams(dims):
    return pltpu.CompilerParams(dimension_semantics=dims, vmem_limit_bytes=VMEM_LIMIT_BYTES)


def _rms(x, gain):
    return x * lax.rsqrt(jnp.mean(x * x, axis=-1, keepdims=True) + EPS) * gain


def _gelu(x):
    return jax.nn.gelu(x)


def _rmsnorm_kernel(x_ref, g_ref, o_ref):
    o_ref[...] = _rms(x_ref[...], g_ref[...]).astype(o_ref.dtype)


def rmsnorm_cast(x, gain, tm=512):
    m, d = x.shape
    return pl.pallas_call(
        _rmsnorm_kernel,
        grid=(m // tm,),
        in_specs=[pl.BlockSpec((tm, d), lambda i: (i, 0)),
                  pl.BlockSpec((1, d), lambda i: (0, 0))],
        out_specs=pl.BlockSpec((tm, d), lambda i: (i, 0)),
        out_shape=jax.ShapeDtypeStruct((m, d), jnp.bfloat16),
        compiler_params=_cparams(("parallel",)),
        name="rmsnorm_cast",
    )(x, gain.reshape(1, d))


def _mm_kernel(*refs, mode, nk):
    if mode == "ple":
        x_ref, w_ref, res_ref, p_ref, wp_ref, o_ref = refs[:6]
        rest = refs[6:]
    elif mode == "residual":
        x_ref, w_ref, res_ref, o_ref = refs[:4]
        rest = refs[4:]
    else:
        x_ref, w_ref, o_ref = refs[:3]
        rest = refs[3:]

    def epilogue(acc):
        if mode == "plain":
            o_ref[...] = acc.astype(o_ref.dtype)
        elif mode == "relu2":
            r = jnp.maximum(acc, 0.0)
            o_ref[...] = (r * r).astype(o_ref.dtype)
        elif mode == "residual":
            o_ref[...] = res_ref[...] + acc
        else:
            proj = jnp.dot(p_ref[...], wp_ref[...], preferred_element_type=jnp.float32)
            o_ref[...] = res_ref[...] + jax.nn.sigmoid(acc) * proj

    part = jnp.dot(x_ref[...], w_ref[...], preferred_element_type=jnp.float32)
    if nk == 1:
        epilogue(part)
        return
    acc_ref, = rest
    k = pl.program_id(2)

    @pl.when(k == 0)
    def _():
        acc_ref[...] = part

    @pl.when(jnp.logical_and(k > 0, k < nk - 1))
    def _():
        acc_ref[...] += part

    @pl.when(k == nk - 1)
    def _():
        epilogue(acc_ref[...] + part)


def matmul(x, w, mode="plain", out_dtype=jnp.float32, res=None, p=None, wp=None,
           tm=1024, tn=1024, tk=2048):
    m, kdim = x.shape
    n = w.shape[1]
    tm, tn, tk = min(tm, m), min(tn, n), min(tk, kdim)
    nk = kdim // tk
    assert m % tm == 0 and n % tn == 0 and kdim % tk == 0
    in_specs = [pl.BlockSpec((tm, tk), lambda i, j, k: (i, k)),
                pl.BlockSpec((tk, tn), lambda i, j, k: (k, j))]
    args = [x, w]
    if mode in ("residual", "ple"):
        in_specs.append(pl.BlockSpec((tm, tn), lambda i, j, k: (i, j)))
        args.append(res)
    if mode == "ple":
        pd = p.shape[1]
        in_specs += [pl.BlockSpec((tm, pd), lambda i, j, k: (i, 0)),
                     pl.BlockSpec((pd, tn), lambda i, j, k: (0, j))]
        args += [p, wp]
    scratch = [pltpu.VMEM((tm, tn), jnp.float32)] if nk > 1 else []
    return pl.pallas_call(
        functools.partial(_mm_kernel, mode=mode, nk=nk),
        grid=(m // tm, n // tn, nk),
        in_specs=in_specs,
        out_specs=pl.BlockSpec((tm, tn), lambda i, j, k: (i, j)),
        out_shape=jax.ShapeDtypeStruct((m, n), out_dtype),
        scratch_shapes=scratch,
        compiler_params=_cparams(("parallel", "parallel", "arbitrary")),
        name="matmul_" + mode,
    )(*args)


HALO_ROWS = 8


def _conv_kernel(xa_ref, gb_ref, gc_ref, xap_ref, gcp_ref, cw_ref, on_ref, o_ref):
    i = pl.program_id(1)
    inner = gc_ref[0] * xa_ref[0]
    prev = gcp_ref[0] * xap_ref[0]
    prev = jnp.where(i > 0, prev, 0.0)
    tt = inner.shape[0]
    row = lax.broadcasted_iota(jnp.int32, inner.shape, 0)
    m1 = jnp.where(row == 0, prev[HALO_ROWS - 1:HALO_ROWS], pltpu.roll(inner, 1, 0))
    m2 = pltpu.roll(inner, 2, 0)
    m2 = jnp.where(row == 0, prev[HALO_ROWS - 2:HALO_ROWS - 1], m2)
    m2 = jnp.where(row == 1, prev[HALO_ROWS - 1:HALO_ROWS], m2)
    cw = cw_ref[...]
    conv = cw[0:1] * m2 + cw[1:2] * m1 + cw[2:3] * inner
    o_ref[0] = _rms(gb_ref[0] * conv, on_ref[...]).astype(o_ref.dtype)


def conv_mixer(z3, conv_w, on_gain, tt=512):
    b, t, _ = z3.shape
    c = CONV_WIDTH
    hb = tt // HALO_ROWS
    cur = lambda col: pl.BlockSpec((1, tt, c), lambda bi, i: (bi, i, col))
    halo = lambda col: pl.BlockSpec((1, HALO_ROWS, c),
                                    lambda bi, i: (bi, jnp.maximum(i * hb - 1, 0), col))
    return pl.pallas_call(
        _conv_kernel,
        grid=(b, t // tt),
        in_specs=[cur(0), cur(1), cur(2), halo(0), halo(2),
                  pl.BlockSpec((HALO_ROWS, c), lambda bi, i: (0, 0)),
                  pl.BlockSpec((1, c), lambda bi, i: (0, 0))],
        out_specs=pl.BlockSpec((1, tt, c), lambda bi, i: (bi, i, 0)),
        out_shape=jax.ShapeDtypeStruct((b, t, c), jnp.bfloat16),
        compiler_params=_cparams(("parallel", "parallel")),
        name="conv_mixer",
    )(z3, z3, z3, z3, z3,
      jnp.pad(conv_w, ((0, HALO_ROWS - CONV_K), (0, 0))), on_gain.reshape(1, c))


def _gmlp_kernel(gu_ref, gv_ref, sg_ref, wsp_ref, bexp_ref, on_ref, o_ref, *, n_chunks):
    u = _gelu(gu_ref[0])
    v = _rms(_gelu(gv_ref[0]), sg_ref[...]).astype(jnp.bfloat16)
    ck = GMLP_CHUNK
    r_i = lax.broadcasted_iota(jnp.int32, (ck, ck), 0)
    c_i = lax.broadcasted_iota(jnp.int32, (ck, ck), 1)
    bexp = bexp_ref[...]
    rows = []
    for c in range(n_chunks):
        cols = []
        for g in range(GMLP_GROUPS):
            w = jnp.where(r_i >= c_i, wsp_ref[g], 0.0).astype(jnp.bfloat16)
            vg = v[c * ck:(c + 1) * ck, g * HEAD_DIM:(g + 1) * HEAD_DIM]
            cols.append(jnp.dot(w, vg, preferred_element_type=jnp.float32))
        rows.append(jnp.concatenate(cols, axis=1) + bexp)
    spatial = jnp.concatenate(rows, axis=0) if n_chunks > 1 else rows[0]
    o_ref[0] = _rms(u * spatial, on_ref[...]).astype(o_ref.dtype)


def gmlp_mixer(z3, sgu_gain, w_sp, b_sp, on_gain, tt=256):
    b, t, _ = z3.shape
    c = GMLP_WIDTH
    bexp = jnp.repeat(b_sp.T, HEAD_DIM, axis=1)
    const2 = lambda shape: pl.BlockSpec(shape, lambda bi, i: (0,) * len(shape))
    return pl.pallas_call(
        functools.partial(_gmlp_kernel, n_chunks=tt // GMLP_CHUNK),
        grid=(b, t // tt),
        in_specs=[pl.BlockSpec((1, tt, c), lambda bi, i: (bi, i, GU_OFF // c)),
                  pl.BlockSpec((1, tt, c), lambda bi, i: (bi, i, GV_OFF // c)),
                  const2((1, c)), const2((GMLP_GROUPS, GMLP_CHUNK, GMLP_CHUNK)),
                  const2((GMLP_CHUNK, c)), const2((1, c))],
        out_specs=pl.BlockSpec((1, tt, c), lambda bi, i: (bi, i, 0)),
        out_shape=jax.ShapeDtypeStruct((b, t, c), jnp.bfloat16),
        compiler_params=_cparams(("parallel", "parallel")),
        name="gmlp_mixer",
    )(z3, z3, sgu_gain.reshape(1, c), w_sp, bexp, on_gain.reshape(1, c))


def _kvprep_kernel(z_ref, kg_ref, o_ref):
    z = z_ref[...]
    for part in range(4):
        for g in range(NSA_KV_HEADS):
            lo = part * KV_WIDTH + g * HEAD_DIM
            blk = z[:, lo:lo + HEAD_DIM]
            if part % 2 == 0:
                blk = _rms(blk, kg_ref[1 + part // 2:2 + part // 2, :])
            o_ref[:, lo:lo + HEAD_DIM] = blk.astype(o_ref.dtype)


def kv_prep(z2, k_gain, tt=512):
    m = z2.shape[0]
    w = 4 * KV_WIDTH
    kg = jnp.pad(k_gain, ((0, 8 - NSA_BRANCHES), (0, 0)))
    return pl.pallas_call(
        _kvprep_kernel,
        grid=(m // tt,),
        in_specs=[pl.BlockSpec((tt, w), lambda i: (i, KV4_OFF // w)),
                  pl.BlockSpec((8, HEAD_DIM), lambda i: (0, 0))],
        out_specs=pl.BlockSpec((tt, w), lambda i: (i, 0)),
        out_shape=jax.ShapeDtypeStruct((m, w), jnp.bfloat16),
        compiler_params=_cparams(("parallel",)),
        name="kv_prep",
    )(z2, kg)


def _compress_kernel(k_ref, v_ref, w1_ref, w2_ref, pe_ref, kg_ref, kc_ref, vc_ref, *, n_chunk):
    def one(src_ref, which):
        ya = jnp.zeros((n_chunk, HEAD_DIM), jnp.float32)
        yb = jnp.zeros((n_chunk, HEAD_DIM), jnp.float32)
        bias = jnp.zeros((8, HEAD_DIM), jnp.float32)
        for l in range(CMP_STRIDE):
            xl = src_ref[0, pl.ds(l, n_chunk, stride=CMP_STRIDE), :].astype(jnp.bfloat16)
            ya += jnp.dot(xl, w1_ref[which, l], preferred_element_type=jnp.float32)
            yb += jnp.dot(xl, w1_ref[which, CMP_STRIDE + l], preferred_element_type=jnp.float32)
        for l in range(CMP_LEN):
            pe_l = jnp.broadcast_to(pe_ref[which, l:l + 1, :], (8, HEAD_DIM)).astype(jnp.bfloat16)
            bias += jnp.dot(pe_l, w1_ref[which, l], preferred_element_type=jnp.float32)
        hid = _gelu(ya + pltpu.roll(yb, n_chunk - 1, 0) + bias[0:1])
        return jnp.dot(hid.astype(jnp.bfloat16), w2_ref[which], preferred_element_type=jnp.float32)

    kc_ref[0, 0] = _rms(one(k_ref, 0), kg_ref[0:1, :]).astype(kc_ref.dtype)
    vc_ref[0, 0] = one(v_ref, 1).astype(vc_ref.dtype)


def compress_kv(z3, cmp_pos, cmp_w1, cmp_w2, k_gain):
    b, t, _ = z3.shape
    n_chunk = t // CMP_STRIDE
    kg = jnp.pad(k_gain, ((0, 8 - NSA_BRANCHES), (0, 0)))
    const = lambda shape: pl.BlockSpec(shape, lambda bi, g: (0,) * len(shape))
    out = jax.ShapeDtypeStruct((b, NSA_KV_HEADS, n_chunk, HEAD_DIM), jnp.bfloat16)
    return pl.pallas_call(
        functools.partial(_compress_kernel, n_chunk=n_chunk),
        grid=(b, NSA_KV_HEADS),
        in_specs=[pl.BlockSpec((1, t, HEAD_DIM), lambda bi, g: (bi, 0, KCMP_OFF // HEAD_DIM + g)),
                  pl.BlockSpec((1, t, HEAD_DIM), lambda bi, g: (bi, 0, VCMP_OFF // HEAD_DIM + g)),
                  const((2, CMP_LEN, HEAD_DIM, HEAD_DIM)), const((2, HEAD_DIM, HEAD_DIM)),
                  const((2, CMP_LEN, HEAD_DIM)), const((8, HEAD_DIM))],
        out_specs=[pl.BlockSpec((1, 1, n_chunk, HEAD_DIM), lambda bi, g: (bi, g, 0, 0))] * 2,
        out_shape=[out, out],
        compiler_params=_cparams(("parallel", "parallel")),
        name="compress_kv",
    )(z3, z3, cmp_w1.astype(jnp.bfloat16), cmp_w2.astype(jnp.bfloat16), cmp_pos, kg)


def _alibi_slopes():
    h = np.arange(1, NSA_HEADS + 1, dtype=np.float32)
    return np.power(np.float32(2.0), -8.0 * h / NSA_HEADS).astype(np.float32)


def _overlap_matrix(n_chunk, n_slc):
    n_cmp = n_chunk - CMP_LEN // CMP_STRIDE + 1
    jc = np.arange(n_cmp)[:, None] * CMP_STRIDE
    bs = np.arange(n_slc)[None, :] * SLC_BLOCK
    ov = np.clip(np.minimum(jc + CMP_LEN, bs + SLC_BLOCK) - np.maximum(jc, bs), 0, None)
    out = np.zeros((n_chunk, LANES), np.float32)
    out[:n_cmp, :n_slc] = ov.astype(np.float32) / CMP_LEN
    return out


def _q_heads(q_ref, qg_ref):
    q = q_ref[0]
    scale = HEAD_DIM ** -0.5
    return [(_rms(q[:, r * HEAD_DIM:(r + 1) * HEAD_DIM], qg_ref[...]) * scale).astype(jnp.bfloat16)
            for r in range(NSA_REP)]


def _cmp_kernel(slopes_ref, q_ref, kc_ref, vc_ref, qg_ref, ov_ref, o_ref, sel_ref,
                *, tq, n_chunk, n_slc):
    g = pl.program_id(1)
    i = pl.program_id(2)
    n_cmp = n_chunk - CMP_LEN // CMP_STRIDE + 1
    qh = _q_heads(q_ref, qg_ref)
    kc = kc_ref[0, 0]
    vc = vc_ref[0, 0]
    t = i * tq + lax.broadcasted_iota(jnp.int32, (tq, n_chunk), 0)
    n = lax.broadcasted_iota(jnp.int32, (tq, n_chunk), 1)
    dist = t - (n * CMP_STRIDE + CMP_LEN - 1)
    mask = jnp.logical_and(dist >= 0, n < n_cmp)
    distf = dist.astype(jnp.float32)
    psum = jnp.zeros((tq, n_chunk), jnp.float32)
    for r in range(NSA_REP):
        s = lax.dot_general(qh[r], kc, (((1,), (1,)), ((), ())),
                            preferred_element_type=jnp.float32)
        s = jnp.where(mask, s - slopes_ref[g * NSA_REP + r] * distf, NEG_INF)
        mx = jnp.max(s, axis=-1, keepdims=True)
        e = jnp.where(mask, jnp.exp(s - mx), 0.0)
        den = jnp.sum(e, axis=-1, keepdims=True)
        p = e / jnp.where(den > 0.0, den, 1.0)
        psum += p
        o_ref[0, :, r * HEAD_DIM:(r + 1) * HEAD_DIM] = jnp.dot(
            p.astype(jnp.bfloat16), vc, preferred_element_type=jnp.float32)

    ov = ov_ref[...]
    imp = jnp.zeros((tq, LANES), jnp.float32)
    rem = psum
    for _ in range(3):
        part = rem.astype(jnp.bfloat16)
        imp += jnp.dot(part, ov, preferred_element_type=jnp.float32)
        rem = rem - part.astype(jnp.float32)

    tb = i * tq + lax.broadcasted_iota(jnp.int32, (tq, LANES), 0)
    blk = lax.broadcasted_iota(jnp.int32, (tq, LANES), 1)
    qblk = lax.shift_right_logical(tb, SLC_SHIFT)
    forced = (blk == 0) | (blk == qblk) | (blk == qblk - 1)
    score = jnp.where(forced, FORCE_SCORE, jnp.where(blk <= qblk, imp, NEG_INF))
    score = jnp.where(blk < n_slc, score, BELOW_NEG_INF)
    st = score.T[0:n_slc]
    sub = lax.broadcasted_iota(jnp.int32, (n_slc, tq), 0)
    cnt = jnp.zeros((n_slc, tq), jnp.float32)
    for m in range(n_slc):
        row = st[m:m + 1, :]
        ahead = (row > st) | ((row == st) & (sub > m))
        cnt += jnp.where(ahead, 1.0, 0.0)
    sel_t = jnp.where(cnt < float(min(N_SELECT, n_slc)), 1.0, 0.0)
    if n_slc < LANES:
        sel_t = jnp.concatenate([sel_t, jnp.zeros((LANES - n_slc, tq), jnp.float32)], axis=0)
    sel_ref[0, 0] = sel_t.T.astype(sel_ref.dtype)


def cmp_attention(z3, kc, vc, q_gain, tq=256):
    b, t, _ = z3.shape
    n_chunk = t // CMP_STRIDE
    n_slc = t // SLC_BLOCK
    assert n_slc <= LANES
    gw = NSA_REP * HEAD_DIM
    grid_spec = pltpu.PrefetchScalarGridSpec(
        num_scalar_prefetch=1,
        grid=(b, NSA_KV_HEADS, t // tq),
        in_specs=[pl.BlockSpec((1, tq, gw), lambda bi, g, i, s: (bi, i, Q_OFF // gw + g)),
                  pl.BlockSpec((1, 1, n_chunk, HEAD_DIM), lambda bi, g, i, s: (bi, g, 0, 0)),
                  pl.BlockSpec((1, 1, n_chunk, HEAD_DIM), lambda bi, g, i, s: (bi, g, 0, 0)),
                  pl.BlockSpec((1, HEAD_DIM), lambda bi, g, i, s: (0, 0)),
                  pl.BlockSpec((n_chunk, LANES), lambda bi, g, i, s: (0, 0))],
        out_specs=[pl.BlockSpec((1, tq, gw), lambda bi, g, i, s: (bi, i, g)),
                   pl.BlockSpec((1, 1, tq, LANES), lambda bi, g, i, s: (bi, g, i, 0))],
    )
    return pl.pallas_call(
        functools.partial(_cmp_kernel, tq=tq, n_chunk=n_chunk, n_slc=n_slc),
        grid_spec=grid_spec,
        out_shape=[jax.ShapeDtypeStruct((b, t, NSA_WIDTH), jnp.float32),
                   jax.ShapeDtypeStruct((b, NSA_KV_HEADS, t, LANES), jnp.bfloat16)],
        compiler_params=_cparams(("parallel", "parallel", "parallel")),
        name="cmp_attention",
    )(jnp.asarray(_alibi_slopes()), z3, kc, vc, q_gain.reshape(1, HEAD_DIM),
      jnp.asarray(_overlap_matrix(n_chunk, n_slc), dtype=jnp.bfloat16))


def _kv_tile(mode, i, j, tq, tk):
    if mode == "slc":
        return j, j * tk <= i * tq + tq - 1
    idx = i * (tq // tk) - WINDOW // tk + j
    return idx, idx >= 0


def _flash_kernel(slopes_ref, *refs, mode, tq, tk, nkv):
    if mode == "slc":
        q_ref, k_ref, v_ref, qg_ref, sel_ref, o_ref, q_sc, m_sc, l_sc, acc_sc = refs
    else:
        q_ref, k_ref, v_ref, qg_ref, o_ref, q_sc, m_sc, l_sc, acc_sc = refs
    g = pl.program_id(1)
    i = pl.program_id(2)
    j = pl.program_id(3)
    kv_idx, live = _kv_tile(mode, i, j, tq, tk)

    @pl.when(j == 0)
    def _():
        for r, qh in enumerate(_q_heads(q_ref, qg_ref)):
            q_sc[r] = qh
        m_sc[...] = jnp.full(m_sc.shape, NEG_INF, jnp.float32)
        l_sc[...] = jnp.zeros(l_sc.shape, jnp.float32)
        acc_sc[...] = jnp.zeros(acc_sc.shape, jnp.float32)

    @pl.when(live)
    def _():
        k = k_ref[0]
        v = v_ref[0]
        t = i * tq + lax.broadcasted_iota(jnp.int32, (tq, tk), 0)
        kpos = kv_idx * tk + lax.broadcasted_iota(jnp.int32, (tq, tk), 1)
        dist = t - kpos
        if mode == "slc":
            m_i = lax.broadcasted_iota(jnp.int32, (LANES, tk), 0)
            c_i = kv_idx * tk + lax.broadcasted_iota(jnp.int32, (LANES, tk), 1)
            expand = jnp.where(m_i == lax.shift_right_logical(c_i, SLC_SHIFT), 1.0, 0.0).astype(jnp.bfloat16)
            chosen = jnp.dot(sel_ref[0, 0], expand, preferred_element_type=jnp.float32)
            mask = jnp.logical_and(dist >= 0, chosen > 0.5)
        else:
            mask = jnp.logical_and(dist >= 0, dist < WINDOW)
        distf = dist.astype(jnp.float32)
        for r in range(NSA_REP):
            s = lax.dot_general(q_sc[r], k, (((1,), (1,)), ((), ())),
                                preferred_element_type=jnp.float32)
            s = jnp.where(mask, s - slopes_ref[g * NSA_REP + r] * distf, NEG_INF)
            m_prev = m_sc[r]
            m_new = jnp.maximum(m_prev, jnp.max(s, axis=-1, keepdims=True))
            m_use = jnp.where(m_new > 0.5 * NEG_INF, m_new, 0.0)
            p = jnp.exp(s - m_use)
            alpha = jnp.exp(m_prev - m_use)
            l_sc[r] = alpha * l_sc[r] + jnp.sum(p, axis=-1, keepdims=True)
            acc_sc[r] = alpha * acc_sc[r] + jnp.dot(p.astype(jnp.bfloat16), v,
                                                    preferred_element_type=jnp.float32)
            m_sc[r] = m_new

    @pl.when(j == nkv - 1)
    def _():
        for r in range(NSA_REP):
            o_ref[0, :, r * HEAD_DIM:(r + 1) * HEAD_DIM] = acc_sc[r] / l_sc[r]


def flash_attention(mode, z3, kv3, q_gain, sel=None, tq=256, tk=None):
    b, t, _ = z3.shape
    if mode == "slc":
        tk = tk or 512
        nkv = t // tk
        k_col, v_col = 0, NSA_KV_HEADS
    else:
        tk = tk or tq
        nkv = WINDOW // tk + tq // tk
        k_col, v_col = 2 * NSA_KV_HEADS, 3 * NSA_KV_HEADS
    gw = NSA_REP * HEAD_DIM

    def kv_map(col):
        def index(bi, g, i, j, s):
            idx, _ = _kv_tile(mode, i, j, tq, tk)
            last = (i * tq + tq - 1) // tk
            return bi, jnp.clip(idx, 0, last), col + g
        return index

    in_specs = [pl.BlockSpec((1, tq, gw), lambda bi, g, i, j, s: (bi, i, Q_OFF // gw + g)),
                pl.BlockSpec((1, tk, HEAD_DIM), kv_map(k_col)),
                pl.BlockSpec((1, tk, HEAD_DIM), kv_map(v_col)),
                pl.BlockSpec((1, HEAD_DIM), lambda bi, g, i, j, s: (0, 0))]
    args = [jnp.asarray(_alibi_slopes()), z3, kv3, kv3, q_gain.reshape(1, HEAD_DIM)]
    if mode == "slc":
        in_specs.append(pl.BlockSpec((1, 1, tq, LANES), lambda bi, g, i, j, s: (bi, g, i, 0)))
        args.append(sel)
    grid_spec = pltpu.PrefetchScalarGridSpec(
        num_scalar_prefetch=1,
        grid=(b, NSA_KV_HEADS, t // tq, nkv),
        in_specs=in_specs,
        out_specs=pl.BlockSpec((1, tq, gw), lambda bi, g, i, j, s: (bi, i, g)),
        scratch_shapes=[pltpu.VMEM((NSA_REP, tq, HEAD_DIM), jnp.bfloat16),
                        pltpu.VMEM((NSA_REP, tq, 1), jnp.float32),
                        pltpu.VMEM((NSA_REP, tq, 1), jnp.float32),
                        pltpu.VMEM((NSA_REP, tq, HEAD_DIM), jnp.float32)],
    )
    return pl.pallas_call(
        functools.partial(_flash_kernel, mode=mode, tq=tq, tk=tk, nkv=nkv),
        grid_spec=grid_spec,
        out_shape=jax.ShapeDtypeStruct((b, t, NSA_WIDTH), jnp.float32),
        compiler_params=_cparams(("parallel", "parallel", "parallel", "arbitrary")),
        name="flash_" + mode,
    )(*args)


def _gate_expand_matrix():
    m = np.zeros((NSA_BRANCHES, LANES, NSA_WIDTH), np.float32)
    for br in range(NSA_BRANCHES):
        for h in range(NSA_HEADS):
            m[br, h * NSA_BRANCHES + br, h * HEAD_DIM:(h + 1) * HEAD_DIM] = 1.0
    return m


def _combine_kernel(oc_ref, os_ref, ow_ref, gl_ref, ex_ref, on_ref, o_ref):
    sig = jax.nn.sigmoid(gl_ref[...])
    hi = sig.astype(jnp.bfloat16)
    lo = (sig - hi.astype(jnp.float32)).astype(jnp.bfloat16)
    out = None
    for br, ref in enumerate((oc_ref, os_ref, ow_ref)):
        ex = ex_ref[br]
        gate = (jnp.dot(hi, ex, preferred_element_type=jnp.float32)
                + jnp.dot(lo, ex, preferred_element_type=jnp.float32))
        term = gate * ref[...]
        out = term if out is None else out + term
    o_ref[...] = _rms(out, on_ref[...]).astype(o_ref.dtype)


def nsa_combine(o_cmp, o_slc, o_win, gate_logits, on_gain, tt=256):
    m, w = o_cmp.shape
    row = pl.BlockSpec((tt, w), lambda i: (i, 0))
    return pl.pallas_call(
        _combine_kernel,
        grid=(m // tt,),
        in_specs=[row, row, row,
                  pl.BlockSpec((tt, LANES), lambda i: (i, 0)),
                  pl.BlockSpec((NSA_BRANCHES, LANES, w), lambda i: (0, 0, 0)),
                  pl.BlockSpec((1, w), lambda i: (0, 0))],
        out_specs=row,
        out_shape=jax.ShapeDtypeStruct((m, w), jnp.bfloat16),
        compiler_params=_cparams(("parallel",)),
        name="nsa_combine",
    )(o_cmp, o_slc, o_win, gate_logits,
      jnp.asarray(_gate_expand_matrix(), dtype=jnp.bfloat16), on_gain.reshape(1, w))


def _layer(h, p_i, wts):
    (mix_norm, w_in, conv_w, q_gain, k_gain, cmp_pos, cmp_w1, cmp_w2, sgu_gain, w_sp, b_sp,
     mix_out_norm, w_out, mlp_norm, w_mlp_in, w_mlp_out, ple_norm, w_ple_proj, w_ple_gate) = wts
    b, t, d = h.shape
    m = b * t
    bf = jnp.bfloat16
    h2 = h.reshape(m, d)

    xn = rmsnorm_cast(h2, mix_norm)
    w_main = jnp.concatenate([w_in[:, :Z_GATE_OFF], w_in[:, Z_GATE_OFF + N_GATES:]], axis=1).astype(bf)
    w_gate = jnp.pad(w_in[:, Z_GATE_OFF:Z_GATE_OFF + N_GATES], ((0, 0), (0, LANES - N_GATES))).astype(bf)
    z2 = matmul(xn, w_main)
    gate_logits = matmul(xn, w_gate, tn=LANES)
    z3 = z2.reshape(b, t, Z_MAIN_COLS)

    on = mix_out_norm
    mixed_a = conv_mixer(z3, conv_w, on[:CONV_WIDTH])
    kc, vc = compress_kv(z3, cmp_pos, cmp_w1, cmp_w2, k_gain)
    o_cmp, sel = cmp_attention(z3, kc, vc, q_gain)
    kv3 = kv_prep(z2, k_gain).reshape(b, t, 4 * KV_WIDTH)
    o_slc = flash_attention("slc", z3, kv3, q_gain, sel=sel)
    o_win = flash_attention("win", z3, kv3, q_gain)
    mixed_b = nsa_combine(o_cmp.reshape(m, NSA_WIDTH), o_slc.reshape(m, NSA_WIDTH),
                          o_win.reshape(m, NSA_WIDTH), gate_logits,
                          on[CONV_WIDTH:CONV_WIDTH + NSA_WIDTH])
    mixed_c = gmlp_mixer(z3, sgu_gain, w_sp, b_sp, on[CONV_WIDTH + NSA_WIDTH:])
    mixed = jnp.concatenate([mixed_a.reshape(m, CONV_WIDTH), mixed_b,
                             mixed_c.reshape(m, GMLP_WIDTH)], axis=1)
    h2 = matmul(mixed, w_out.astype(bf), mode="residual", res=h2)

    hn = rmsnorm_cast(h2, mlp_norm)
    hid = matmul(hn, w_mlp_in.astype(bf), mode="relu2", out_dtype=bf)
    h2 = matmul(hid, w_mlp_out.astype(bf), mode="residual", res=h2)

    hp = rmsnorm_cast(h2, ple_norm)
    h2 = matmul(hp, w_ple_gate.astype(bf), mode="ple", res=h2,
                p=p_i.reshape(m, PLE_DIM).astype(bf), wp=w_ple_proj.astype(bf))
    return h2.reshape(b, t, d)


def kernel(x, p, mix_norm, w_in, conv_w, q_gain, k_gain, cmp_pos, cmp_w1, cmp_w2, sgu_gain, w_sp,
           b_sp, mix_out_norm, w_out, mlp_norm, w_mlp_in, w_mlp_out, ple_norm, w_ple_proj,
           w_ple_gate):
    stacked = (mix_norm, w_in, conv_w, q_gain, k_gain, cmp_pos, cmp_w1, cmp_w2, sgu_gain, w_sp,
               b_sp, mix_out_norm, w_out, mlp_norm, w_mlp_in, w_mlp_out, ple_norm, w_ple_proj,
               w_ple_gate)
    h = x
    for i in range(p.shape[0]):
        h = _layer(h, p[i], tuple(w[i] for w in stacked))
    return h
```

```python
import functools

import jax
import jax.numpy as jnp
import numpy as np
from jax import lax
from jax.experimental import pallas as pl
from jax.experimental.pallas import tpu as pltpu

D_MODEL = 4096
HEAD_DIM = 128
CONV_WIDTH = D_MODEL // 4
CONV_K = 3
NSA_HEADS = (D_MODEL // 2) // HEAD_DIM
NSA_WIDTH = NSA_HEADS * HEAD_DIM
NSA_KV_HEADS = 4
NSA_REP = NSA_HEADS // NSA_KV_HEADS
NSA_BRANCHES = 3
KV_WIDTH = NSA_KV_HEADS * HEAD_DIM
CMP_LEN = 32
CMP_STRIDE = 16
SLC_BLOCK = 64
SLC_SHIFT = 6
N_SELECT = 16
WINDOW = 512
GMLP_WIDTH = D_MODEL // 4
GMLP_GROUPS = GMLP_WIDTH // HEAD_DIM
GMLP_CHUNK = 128
D_FF = 4 * D_MODEL
PLE_DIM = 256
N_GATES = NSA_HEADS * NSA_BRANCHES
EPS = 1e-6
NEG_INF = -1e30
FORCE_SCORE = 1e9
BELOW_NEG_INF = -3e38

Z_A_COLS = 3 * CONV_WIDTH + NSA_WIDTH + 2 * NSA_BRANCHES * KV_WIDTH
Z_G_COLS = 2 * GMLP_WIDTH
Z_G_OFF = Z_A_COLS + N_GATES
Q_OFF = 3 * CONV_WIDTH
KCMP_OFF = Q_OFF + NSA_WIDTH
VCMP_OFF = KCMP_OFF + KV_WIDTH
KV4_OFF = VCMP_OFF + KV_WIDTH

LANES = 128
VMEM_LIMIT_BYTES = 56 * 1024 * 1024


def _cparams(dims):
    return pltpu.CompilerParams(dimension_semantics=dims, vmem_limit_bytes=VMEM_LIMIT_BYTES)


def _rms(x, gain):
    return x * lax.rsqrt(jnp.mean(x * x, axis=-1, keepdims=True) + EPS) * gain


def _gelu(x):
    return jax.nn.gelu(x)


def _rmsnorm_kernel(x_ref, g_ref, o_ref):
    o_ref[...] = _rms(x_ref[...], g_ref[...]).astype(o_ref.dtype)


def rmsnorm_cast(x, gain, tm=512):
    m, d = x.shape
    return pl.pallas_call(
        _rmsnorm_kernel,
        grid=(m // tm,),
        in_specs=[pl.BlockSpec((tm, d), lambda i: (i, 0)),
                  pl.BlockSpec((1, d), lambda i: (0, 0))],
        out_specs=pl.BlockSpec((tm, d), lambda i: (i, 0)),
        out_shape=jax.ShapeDtypeStruct((m, d), jnp.bfloat16),
        compiler_params=_cparams(("parallel",)),
        name="rmsnorm_cast",
    )(x, gain.reshape(1, d))


def _cast_kernel(w_ref, o_ref):
    o_ref[...] = w_ref[...].astype(o_ref.dtype)


def _cast_shift_kernel(a_ref, b_ref, o_ref, *, shift):
    o_ref[...] = jnp.concatenate([a_ref[:, shift:], b_ref[:, :shift]], axis=1).astype(o_ref.dtype)


def _cast_head_kernel(w_ref, o_ref, *, keep):
    lane = lax.broadcasted_iota(jnp.int32, w_ref.shape, 1)
    o_ref[...] = jnp.where(lane < keep, w_ref[...], 0.0).astype(o_ref.dtype)


def cast_weight(w, layer, ncols=None, tk=512, tn=2048):
    _, kdim, n = w.shape
    ncols = ncols or n
    tk, tn = min(tk, kdim), min(tn, ncols)
    assert kdim % tk == 0 and ncols % tn == 0
    return pl.pallas_call(
        _cast_kernel,
        grid=(kdim // tk, ncols // tn),
        in_specs=[pl.BlockSpec((None, tk, tn), lambda i, j: (layer, i, j))],
        out_specs=pl.BlockSpec((tk, tn), lambda i, j: (i, j)),
        out_shape=jax.ShapeDtypeStruct((kdim, ncols), jnp.bfloat16),
        compiler_params=_cparams(("parallel", "parallel")),
        name="cast_weight",
    )(w)


def cast_w_in_tail(w_in, layer, tk=512):
    _, kdim, n = w_in.shape
    assert n == Z_G_OFF + Z_G_COLS and Z_A_COLS % Z_G_COLS == 0
    shift = Z_G_OFF - Z_A_COLS
    w_g = pl.pallas_call(
        functools.partial(_cast_shift_kernel, shift=shift),
        grid=(kdim // tk,),
        in_specs=[pl.BlockSpec((None, tk, Z_G_COLS), lambda i: (layer, i, Z_A_COLS // Z_G_COLS)),
                  pl.BlockSpec((None, tk, LANES), lambda i: (layer, i, (Z_A_COLS + Z_G_COLS) // LANES))],
        out_specs=pl.BlockSpec((tk, Z_G_COLS), lambda i: (i, 0)),
        out_shape=jax.ShapeDtypeStruct((kdim, Z_G_COLS), jnp.bfloat16),
        compiler_params=_cparams(("parallel",)),
        name="cast_w_in_gmlp",
    )(w_in, w_in)
    w_gate = pl.pallas_call(
        functools.partial(_cast_head_kernel, keep=N_GATES),
        grid=(kdim // tk,),
        in_specs=[pl.BlockSpec((None, tk, LANES), lambda i: (layer, i, Z_A_COLS // LANES))],
        out_specs=pl.BlockSpec((tk, LANES), lambda i: (i, 0)),
        out_shape=jax.ShapeDtypeStruct((kdim, LANES), jnp.bfloat16),
        compiler_params=_cparams(("parallel",)),
        name="cast_w_in_gate",
    )(w_in)
    return w_g, w_gate


def _mm_kernel(*refs, mode, nk):
    if mode == "ple":
        x_ref, w_ref, res_ref, p_ref, wp_ref, o_ref = refs[:6]
        rest = refs[6:]
    elif mode == "residual":
        x_ref, w_ref, res_ref, o_ref = refs[:4]
        rest = refs[4:]
    else:
        x_ref, w_ref, o_ref = refs[:3]
        rest = refs[3:]

    def epilogue(acc):
        if mode == "plain":
            o_ref[...] = acc.astype(o_ref.dtype)
        elif mode == "relu2":
            r = jnp.maximum(acc, 0.0)
            o_ref[...] = (r * r).astype(o_ref.dtype)
        elif mode == "residual":
            o_ref[...] = res_ref[...] + acc
        else:
            proj = jnp.dot(p_ref[...].astype(jnp.bfloat16), wp_ref[...].astype(jnp.bfloat16),
                           preferred_element_type=jnp.float32)
            o_ref[...] = res_ref[...] + jax.nn.sigmoid(acc) * proj

    part = jnp.dot(x_ref[...], w_ref[...], preferred_element_type=jnp.float32)
    if nk == 1:
        epilogue(part)
        return
    acc_ref, = rest
    k = pl.program_id(2)

    @pl.when(k == 0)
    def _():
        acc_ref[...] = part

    @pl.when(jnp.logical_and(k > 0, k < nk - 1))
    def _():
        acc_ref[...] += part

    @pl.when(k == nk - 1)
    def _():
        epilogue(acc_ref[...] + part)


def matmul(x, w, mode="plain", out_dtype=jnp.float32, res=None, p=None, wp=None, layer=0,
           tm=1024, tn=1024, tk=2048):
    m, kdim = x.shape
    n = w.shape[1]
    tm, tn, tk = min(tm, m), min(tn, n), min(tk, kdim)
    nk = kdim // tk
    assert m % tm == 0 and n % tn == 0 and kdim % tk == 0
    in_specs = [pl.BlockSpec((tm, tk), lambda i, j, k: (i, k)),
                pl.BlockSpec((tk, tn), lambda i, j, k: (k, j))]
    args = [x, w]
    if mode in ("residual", "ple"):
        in_specs.append(pl.BlockSpec((tm, tn), lambda i, j, k: (i, j)))
        args.append(res)
    if mode == "ple":
        pd = p.shape[2]
        in_specs += [pl.BlockSpec((None, tm, pd), lambda i, j, k: (layer, i, 0)),
                     pl.BlockSpec((None, pd, tn), lambda i, j, k: (layer, 0, j))]
        args += [p, wp]
    scratch = [pltpu.VMEM((tm, tn), jnp.float32)] if nk > 1 else []
    return pl.pallas_call(
        functools.partial(_mm_kernel, mode=mode, nk=nk),
        grid=(m // tm, n // tn, nk),
        in_specs=in_specs,
        out_specs=pl.BlockSpec((tm, tn), lambda i, j, k: (i, j)),
        out_shape=jax.ShapeDtypeStruct((m, n), out_dtype),
        scratch_shapes=scratch,
        compiler_params=_cparams(("parallel", "parallel", "arbitrary")),
        name="matmul_" + mode,
    )(*args)


HALO_ROWS = 8


def _conv_kernel(xa_ref, gb_ref, gc_ref, xap_ref, gcp_ref, cw_ref, on_ref, o_ref):
    i = pl.program_id(1)
    inner = gc_ref[0] * xa_ref[0]
    prev = gcp_ref[0] * xap_ref[0]
    prev = jnp.where(i > 0, prev, 0.0)
    tt = inner.shape[0]
    row = lax.broadcasted_iota(jnp.int32, inner.shape, 0)
    m1 = jnp.where(row == 0, prev[HALO_ROWS - 1:HALO_ROWS], pltpu.roll(inner, 1, 0))
    m2 = pltpu.roll(inner, 2, 0)
    m2 = jnp.where(row == 0, prev[HALO_ROWS - 2:HALO_ROWS - 1], m2)
    m2 = jnp.where(row == 1, prev[HALO_ROWS - 1:HALO_ROWS], m2)
    cw = cw_ref[...]
    conv = cw[0:1] * m2 + cw[1:2] * m1 + cw[2:3] * inner
    o_ref[0] = _rms(gb_ref[0] * conv, on_ref[...]).astype(o_ref.dtype)


def conv_mixer(z3, conv_w, on_gain, tt=512):
    b, t, _ = z3.shape
    c = CONV_WIDTH
    hb = tt // HALO_ROWS
    cur = lambda col: pl.BlockSpec((1, tt, c), lambda bi, i: (bi, i, col))
    halo = lambda col: pl.BlockSpec((1, HALO_ROWS, c),
                                    lambda bi, i: (bi, jnp.maximum(i * hb - 1, 0), col))
    return pl.pallas_call(
        _conv_kernel,
        grid=(b, t // tt),
        in_specs=[cur(0), cur(1), cur(2), halo(0), halo(2),
                  pl.BlockSpec((HALO_ROWS, c), lambda bi, i: (0, 0)),
                  pl.BlockSpec((1, c), lambda bi, i: (0, 0))],
        out_specs=pl.BlockSpec((1, tt, c), lambda bi, i: (bi, i, 0)),
        out_shape=jax.ShapeDtypeStruct((b, t, c), jnp.bfloat16),
        compiler_params=_cparams(("parallel", "parallel")),
        name="conv_mixer",
    )(z3, z3, z3, z3, z3,
      jnp.pad(conv_w, ((0, HALO_ROWS - CONV_K), (0, 0))), on_gain.reshape(1, c))


def _gmlp_kernel(gu_ref, gv_ref, sg_ref, wsp_ref, bexp_ref, on_ref, o_ref, *, n_chunks):
    u = _gelu(gu_ref[0])
    v = _rms(_gelu(gv_ref[0]), sg_ref[...]).astype(jnp.bfloat16)
    ck = GMLP_CHUNK
    r_i = lax.broadcasted_iota(jnp.int32, (ck, ck), 0)
    c_i = lax.broadcasted_iota(jnp.int32, (ck, ck), 1)
    bexp = bexp_ref[...]
    rows = []
    for c in range(n_chunks):
        cols = []
        for g in range(GMLP_GROUPS):
            w = jnp.where(r_i >= c_i, wsp_ref[g], 0.0).astype(jnp.bfloat16)
            vg = v[c * ck:(c + 1) * ck, g * HEAD_DIM:(g + 1) * HEAD_DIM]
            cols.append(jnp.dot(w, vg, preferred_element_type=jnp.float32))
        rows.append(jnp.concatenate(cols, axis=1) + bexp)
    spatial = jnp.concatenate(rows, axis=0) if n_chunks > 1 else rows[0]
    o_ref[0] = _rms(u * spatial, on_ref[...]).astype(o_ref.dtype)


def gmlp_mixer(z3, sgu_gain, w_sp, b_sp, on_gain, tt=256):
    b, t, _ = z3.shape
    c = GMLP_WIDTH
    bexp = jnp.repeat(b_sp.T, HEAD_DIM, axis=1)
    const2 = lambda shape: pl.BlockSpec(shape, lambda bi, i: (0,) * len(shape))
    return pl.pallas_call(
        functools.partial(_gmlp_kernel, n_chunks=tt // GMLP_CHUNK),
        grid=(b, t // tt),
        in_specs=[pl.BlockSpec((1, tt, c), lambda bi, i: (bi, i, 0)),
                  pl.BlockSpec((1, tt, c), lambda bi, i: (bi, i, 1)),
                  const2((1, c)), const2((GMLP_GROUPS, GMLP_CHUNK, GMLP_CHUNK)),
                  const2((GMLP_CHUNK, c)), const2((1, c))],
        out_specs=pl.BlockSpec((1, tt, c), lambda bi, i: (bi, i, 0)),
        out_shape=jax.ShapeDtypeStruct((b, t, c), jnp.bfloat16),
        compiler_params=_cparams(("parallel", "parallel")),
        name="gmlp_mixer",
    )(z3, z3, sgu_gain.reshape(1, c), w_sp, bexp, on_gain.reshape(1, c))


AUG_DIM = 2 * HEAD_DIM
POS_LANE = 64
ONE_LANE = 70
FEAT_END = 73
MASK_BIG = 2.0 ** 100
MASKED = -MASK_BIG
M_INIT = NEG_INF
LOG2E = 1.4426950408889634


def _kvprep_kernel(z_ref, kg_ref, o_ref, *, tt):
    i = pl.program_id(1)
    z = z_ref[0]
    pos = i * tt + lax.broadcasted_iota(jnp.int32, (tt, HEAD_DIM), 0)
    lane = lax.broadcasted_iota(jnp.int32, (tt, HEAD_DIM), 1)
    blk = lax.shift_right_logical(pos, SLC_SHIFT)
    hi = lax.shift_left(blk, SLC_SHIFT).astype(jnp.float32)
    lo = jnp.bitwise_and(pos, SLC_BLOCK - 1).astype(jnp.float32)
    posf = jnp.where(jnp.bitwise_and(lane, 1) == 0, hi, lo)
    tail = jnp.where(lane < ONE_LANE, posf, jnp.where(lane < FEAT_END, 1.0, 0.0))
    onehot = jnp.where(lane == blk, MASK_BIG, 0.0)
    kfeat = (jnp.where(lane < POS_LANE, onehot, tail).astype(o_ref.dtype),
             jnp.where(lane < POS_LANE, 0.0, tail).astype(o_ref.dtype))
    vfeat = jnp.ones((tt, HEAD_DIM), o_ref.dtype)
    for part in range(4):
        for g in range(NSA_KV_HEADS):
            src = part * KV_WIDTH + g * HEAD_DIM
            dst = (part * NSA_KV_HEADS + g) * AUG_DIM
            x = z[:, src:src + HEAD_DIM]
            if part % 2 == 0:
                x = _rms(x, kg_ref[1 + part // 2:2 + part // 2, :])
            o_ref[0, :, dst:dst + HEAD_DIM] = x.astype(o_ref.dtype)
            o_ref[0, :, dst + HEAD_DIM:dst + AUG_DIM] = kfeat[part // 2] if part % 2 == 0 else vfeat


def kv_prep(z3, k_gain, tt=256):
    b, t, _ = z3.shape
    w = 4 * KV_WIDTH
    wo = 4 * NSA_KV_HEADS * AUG_DIM
    kg = jnp.pad(k_gain, ((0, 8 - NSA_BRANCHES), (0, 0)))
    return pl.pallas_call(
        functools.partial(_kvprep_kernel, tt=tt),
        grid=(b, t // tt),
        in_specs=[pl.BlockSpec((1, tt, w), lambda bi, i: (bi, i, KV4_OFF // w)),
                  pl.BlockSpec((8, HEAD_DIM), lambda bi, i: (0, 0))],
        out_specs=pl.BlockSpec((1, tt, wo), lambda bi, i: (bi, i, 0)),
        out_shape=jax.ShapeDtypeStruct((b, t, wo), jnp.bfloat16),
        compiler_params=_cparams(("parallel", "parallel")),
        name="kv_prep",
    )(z3, kg)


def _compress_kernel(k_ref, v_ref, w1_ref, w2_ref, pe_ref, kg_ref, kc_ref, vc_ref, *, n_chunk):
    def one(src_ref, which):
        ya = jnp.zeros((n_chunk, HEAD_DIM), jnp.float32)
        yb = jnp.zeros((n_chunk, HEAD_DIM), jnp.float32)
        bias = jnp.zeros((8, HEAD_DIM), jnp.float32)
        for l in range(CMP_STRIDE):
            xl = src_ref[0, pl.ds(l, n_chunk, stride=CMP_STRIDE), :].astype(jnp.bfloat16)
            ya += jnp.dot(xl, w1_ref[which, l], preferred_element_type=jnp.float32)
            yb += jnp.dot(xl, w1_ref[which, CMP_STRIDE + l], preferred_element_type=jnp.float32)
        for l in range(CMP_LEN):
            pe_l = jnp.broadcast_to(pe_ref[which, l:l + 1, :], (8, HEAD_DIM)).astype(jnp.bfloat16)
            bias += jnp.dot(pe_l, w1_ref[which, l], preferred_element_type=jnp.float32)
        hid = _gelu(ya + pltpu.roll(yb, n_chunk - 1, 0) + bias[0:1])
        return jnp.dot(hid.astype(jnp.bfloat16), w2_ref[which], preferred_element_type=jnp.float32)

    kc_ref[0, 0] = _rms(one(k_ref, 0), kg_ref[0:1, :]).astype(kc_ref.dtype)
    vc_ref[0, 0] = one(v_ref, 1).astype(vc_ref.dtype)


def compress_kv(z3, cmp_pos, cmp_w1, cmp_w2, k_gain):
    b, t, _ = z3.shape
    n_chunk = t // CMP_STRIDE
    kg = jnp.pad(k_gain, ((0, 8 - NSA_BRANCHES), (0, 0)))
    const = lambda shape: pl.BlockSpec(shape, lambda bi, g: (0,) * len(shape))
    out = jax.ShapeDtypeStruct((b, NSA_KV_HEADS, n_chunk, HEAD_DIM), jnp.bfloat16)
    return pl.pallas_call(
        functools.partial(_compress_kernel, n_chunk=n_chunk),
        grid=(b, NSA_KV_HEADS),
        in_specs=[pl.BlockSpec((1, t, HEAD_DIM), lambda bi, g: (bi, 0, KCMP_OFF // HEAD_DIM + g)),
                  pl.BlockSpec((1, t, HEAD_DIM), lambda bi, g: (bi, 0, VCMP_OFF // HEAD_DIM + g)),
                  const((2, CMP_LEN, HEAD_DIM, HEAD_DIM)), const((2, HEAD_DIM, HEAD_DIM)),
                  const((2, CMP_LEN, HEAD_DIM)), const((8, HEAD_DIM))],
        out_specs=[pl.BlockSpec((1, 1, n_chunk, HEAD_DIM), lambda bi, g: (bi, g, 0, 0))] * 2,
        out_shape=[out, out],
        compiler_params=_cparams(("parallel", "parallel")),
        name="compress_kv",
    )(z3, z3, cmp_w1.astype(jnp.bfloat16), cmp_w2.astype(jnp.bfloat16), cmp_pos, kg)


def _alibi_slopes():
    h = np.arange(1, NSA_HEADS + 1, dtype=np.float32)
    return np.power(np.float32(2.0), -8.0 * h / NSA_HEADS).astype(np.float32)


def _overlap_matrix(n_chunk, n_slc):
    n_cmp = n_chunk - CMP_LEN // CMP_STRIDE + 1
    jc = np.arange(n_cmp)[:, None] * CMP_STRIDE
    bs = np.arange(n_slc)[None, :] * SLC_BLOCK
    ov = np.clip(np.minimum(jc + CMP_LEN, bs + SLC_BLOCK) - np.maximum(jc, bs), 0, None)
    out = np.zeros((n_chunk, LANES), np.float32)
    out[:n_cmp, :n_slc] = ov.astype(np.float32) / CMP_LEN
    return out


def _q_heads(q_ref, qg_ref):
    q = q_ref[0]
    scale = HEAD_DIM ** -0.5
    return [(_rms(q[:, r * HEAD_DIM:(r + 1) * HEAD_DIM], qg_ref[...]) * scale).astype(jnp.bfloat16)
            for r in range(NSA_REP)]


def _cmp_kernel(slopes_ref, q_ref, kc_ref, vc_ref, qg_ref, ov_ref, o_ref, sel_ref,
                *, tq, n_chunk, n_slc):
    g = pl.program_id(1)
    i = pl.program_id(2)
    n_cmp = n_chunk - CMP_LEN // CMP_STRIDE + 1
    qh = _q_heads(q_ref, qg_ref)
    kc = kc_ref[0, 0]
    vc = vc_ref[0, 0]
    t = i * tq + lax.broadcasted_iota(jnp.int32, (tq, n_chunk), 0)
    n = lax.broadcasted_iota(jnp.int32, (tq, n_chunk), 1)
    dist = t - (n * CMP_STRIDE + CMP_LEN - 1)
    mask = jnp.logical_and(dist >= 0, n < n_cmp)
    distf = dist.astype(jnp.float32)
    psum = jnp.zeros((tq, n_chunk), jnp.float32)
    for r in range(NSA_REP):
        s = lax.dot_general(qh[r], kc, (((1,), (1,)), ((), ())),
                            preferred_element_type=jnp.float32)
        s = jnp.where(mask, s - slopes_ref[g * NSA_REP + r] * distf, NEG_INF)
        mx = jnp.max(s, axis=-1, keepdims=True)
        e = jnp.where(mask, jnp.exp(s - mx), 0.0)
        den = jnp.sum(e, axis=-1, keepdims=True)
        p = e / jnp.where(den > 0.0, den, 1.0)
        psum += p
        o_ref[0, :, r * HEAD_DIM:(r + 1) * HEAD_DIM] = jnp.dot(
            p.astype(jnp.bfloat16), vc, preferred_element_type=jnp.float32)

    ov = ov_ref[...]
    imp = jnp.zeros((tq, LANES), jnp.float32)
    rem = psum
    for _ in range(3):
        part = rem.astype(jnp.bfloat16)
        imp += jnp.dot(part, ov, preferred_element_type=jnp.float32)
        rem = rem - part.astype(jnp.float32)

    tb = i * tq + lax.broadcasted_iota(jnp.int32, (tq, LANES), 0)
    blk = lax.broadcasted_iota(jnp.int32, (tq, LANES), 1)
    qblk = lax.shift_right_logical(tb, SLC_SHIFT)
    forced = (blk == 0) | (blk == qblk) | (blk == qblk - 1)
    score = jnp.where(forced, FORCE_SCORE, jnp.where(blk <= qblk, imp, NEG_INF))
    score = jnp.where(blk < n_slc, score, BELOW_NEG_INF)
    st = score.T[0:n_slc]
    sub = lax.broadcasted_iota(jnp.int32, (n_slc, tq), 0)
    cnt = jnp.zeros((n_slc, tq), jnp.float32)
    for m in range(n_slc):
        row = st[m:m + 1, :]
        ahead = (row > st) | ((row == st) & (sub > m))
        cnt += jnp.where(ahead, 1.0, 0.0)
    sel_t = jnp.where(cnt < float(min(N_SELECT, n_slc)), 1.0, 0.0)
    if n_slc < LANES:
        sel_t = jnp.concatenate([sel_t, jnp.zeros((LANES - n_slc, tq), jnp.float32)], axis=0)
    sel_ref[0, 0] = sel_t.T.astype(sel_ref.dtype)


def cmp_attention(z3, kc, vc, q_gain, tq=256):
    b, t, _ = z3.shape
    n_chunk = t // CMP_STRIDE
    n_slc = t // SLC_BLOCK
    assert n_slc <= LANES
    gw = NSA_REP * HEAD_DIM
    grid_spec = pltpu.PrefetchScalarGridSpec(
        num_scalar_prefetch=1,
        grid=(b, NSA_KV_HEADS, t // tq),
        in_specs=[pl.BlockSpec((1, tq, gw), lambda bi, g, i, s: (bi, i, Q_OFF // gw + g)),
                  pl.BlockSpec((1, 1, n_chunk, HEAD_DIM), lambda bi, g, i, s: (bi, g, 0, 0)),
                  pl.BlockSpec((1, 1, n_chunk, HEAD_DIM), lambda bi, g, i, s: (bi, g, 0, 0)),
                  pl.BlockSpec((1, HEAD_DIM), lambda bi, g, i, s: (0, 0)),
                  pl.BlockSpec((n_chunk, LANES), lambda bi, g, i, s: (0, 0))],
        out_specs=[pl.BlockSpec((1, tq, gw), lambda bi, g, i, s: (bi, i, g)),
                   pl.BlockSpec((1, 1, tq, LANES), lambda bi, g, i, s: (bi, g, i, 0))],
    )
    return pl.pallas_call(
        functools.partial(_cmp_kernel, tq=tq, n_chunk=n_chunk, n_slc=n_slc),
        grid_spec=grid_spec,
        out_shape=[jax.ShapeDtypeStruct((b, t, NSA_WIDTH), jnp.float32),
                   jax.ShapeDtypeStruct((b, NSA_KV_HEADS, t, LANES), jnp.bfloat16)],
        compiler_params=_cparams(("parallel", "parallel", "parallel")),
        name="cmp_attention",
    )(jnp.asarray(_alibi_slopes()), z3, kc, vc, q_gain.reshape(1, HEAD_DIM),
      jnp.asarray(_overlap_matrix(n_chunk, n_slc), dtype=jnp.bfloat16))


STEP_FIRST, STEP_LAST, STEP_MASKED = 1, 2, 4


def _flash_steps(mode, t, tq, tk):
    qi, kv, fl = [], [], []
    for i in range(t // tq):
        lo, hi = i * tq, i * tq + tq - 1
        first_key = 0 if mode == "slc" else max(lo - WINDOW + 1, 0)
        tiles = list(range(first_key // tk, hi // tk + 1))
        for n, j in enumerate(tiles):
            k_lo, k_hi = j * tk, j * tk + tk - 1
            causal_ok = k_hi <= lo
            window_ok = mode == "slc" or hi - k_lo < WINDOW
            flag = (STEP_FIRST if n == 0 else 0) | (STEP_LAST if n == len(tiles) - 1 else 0)
            flag |= 0 if (causal_ok and window_ok) else STEP_MASKED
            qi.append(i), kv.append(j), fl.append(flag)
    return [np.asarray(a, np.int32) for a in (qi, kv, fl)]


def _split3(x):
    a = x.astype(jnp.bfloat16).astype(jnp.float32)
    b = (x - a).astype(jnp.bfloat16).astype(jnp.float32)
    c = (x - a - b).astype(jnp.bfloat16).astype(jnp.float32)
    return a, b, c


def _flash_kernel(slopes_ref, qi_ref, kv_ref, fl_ref, *refs, mode, tq, tk):
    if mode == "slc":
        q_ref, k_ref, v_ref, qg_ref, sel_ref, o_ref, q_sc, m_sc, acc_sc = refs
    else:
        q_ref, k_ref, v_ref, qg_ref, o_ref, q_sc, m_sc, acc_sc = refs
    g = pl.program_id(1)
    s = pl.program_id(2)
    i = qi_ref[s]
    kv_idx = kv_ref[s]
    flags = fl_ref[s]

    @pl.when(jnp.bitwise_and(flags, STEP_FIRST) != 0)
    def _():
        q = q_ref[0]
        lane = lax.broadcasted_iota(jnp.int32, (tq, HEAD_DIM), 1)
        if mode == "slc":
            sel_m1 = sel_ref[0, 0].astype(jnp.float32) - 1.0
        else:
            sel_m1 = jnp.zeros((tq, HEAD_DIM), jnp.float32)
        t0 = (i * tq).astype(jnp.float32)
        for r in range(NSA_REP):
            qn = _rms(q[:, r * HEAD_DIM:(r + 1) * HEAD_DIM], qg_ref[...]) * (HEAD_DIM ** -0.5 * LOG2E)
            slope2 = slopes_ref[g * NSA_REP + r] * LOG2E
            a, b, c = _split3(jnp.where(lane < ONE_LANE, slope2, -slope2 * t0))
            k3 = jnp.where(lane < ONE_LANE, lax.shift_right_logical(lane - POS_LANE, 1), lane - ONE_LANE)
            term = jnp.where(k3 == 0, a, jnp.where(k3 == 1, b, c))
            feat = jnp.where(lane < POS_LANE, sel_m1, jnp.where(lane < FEAT_END, term, 0.0))
            q_sc[r, :, 0:HEAD_DIM] = qn.astype(q_sc.dtype)
            q_sc[r, :, HEAD_DIM:AUG_DIM] = feat.astype(q_sc.dtype)
        m_sc[...] = jnp.full(m_sc.shape, M_INIT, jnp.float32)
        acc_sc[...] = jnp.zeros(acc_sc.shape, jnp.float32)

    def step(masked):
        k = k_ref[0]
        v = v_ref[0]
        if masked:
            t = i * tq + lax.broadcasted_iota(jnp.int32, (tq, tk), 0)
            dist = t - (kv_idx * tk + lax.broadcasted_iota(jnp.int32, (tq, tk), 1))
            ok = dist >= 0 if mode == "slc" else jnp.logical_and(dist >= 0, dist < WINDOW)
        for r in range(NSA_REP):
            s2 = lax.dot_general(q_sc[r], k, (((1,), (1,)), ((), ())),
                                 preferred_element_type=jnp.float32)
            if masked:
                s2 = jnp.where(ok, s2, MASKED)
            m_prev = m_sc[r]
            m_new = jnp.maximum(m_prev, jnp.max(s2, axis=-1, keepdims=True))
            p = jnp.exp2(s2 - jnp.tile(m_new, (1, tk // HEAD_DIM))).astype(jnp.bfloat16)
            alpha = jnp.exp2(m_prev - m_new)
            acc_sc[r] = jnp.tile(alpha, (1, AUG_DIM // HEAD_DIM)) * acc_sc[r] + jnp.dot(
                p, v, preferred_element_type=jnp.float32)
            m_sc[r] = m_new

    is_masked = jnp.bitwise_and(flags, STEP_MASKED) != 0
    pl.when(jnp.logical_not(is_masked))(functools.partial(step, False))
    pl.when(is_masked)(functools.partial(step, True))

    @pl.when(jnp.bitwise_and(flags, STEP_LAST) != 0)
    def _():
        for r in range(NSA_REP):
            acc = acc_sc[r]
            o_ref[0, :, r * HEAD_DIM:(r + 1) * HEAD_DIM] = acc[:, 0:HEAD_DIM] / acc[:, HEAD_DIM:AUG_DIM]


def flash_attention(mode, z3, kv3, q_gain, sel=None, tq=None, tk=None):
    b, t, _ = z3.shape
    if mode == "slc":
        tq, tk = tq or 512, tk or 512
        k_col, v_col = 0, NSA_KV_HEADS
    else:
        tq, tk = tq or 256, tk or 256
        k_col, v_col = 2 * NSA_KV_HEADS, 3 * NSA_KV_HEADS
    tq, tk = min(tq, t), min(tk, t)
    gw = NSA_REP * HEAD_DIM
    qi, kv, fl = _flash_steps(mode, t, tq, tk)

    in_specs = [pl.BlockSpec((1, tq, gw), lambda bi, g, s, sl, qi, kv, fl: (bi, qi[s], Q_OFF // gw + g)),
                pl.BlockSpec((1, tk, AUG_DIM), lambda bi, g, s, sl, qi, kv, fl: (bi, kv[s], k_col + g)),
                pl.BlockSpec((1, tk, AUG_DIM), lambda bi, g, s, sl, qi, kv, fl: (bi, kv[s], v_col + g)),
                pl.BlockSpec((1, HEAD_DIM), lambda bi, g, s, sl, qi, kv, fl: (0, 0))]
    args = [jnp.asarray(_alibi_slopes()), jnp.asarray(qi), jnp.asarray(kv), jnp.asarray(fl),
            z3, kv3, kv3, q_gain.reshape(1, HEAD_DIM)]
    if mode == "slc":
        in_specs.append(pl.BlockSpec((1, 1, tq, LANES), lambda bi, g, s, sl, qi, kv, fl: (bi, g, qi[s], 0)))
        args.append(sel)
    grid_spec = pltpu.PrefetchScalarGridSpec(
        num_scalar_prefetch=4,
        grid=(b, NSA_KV_HEADS, len(qi)),
        in_specs=in_specs,
        out_specs=pl.BlockSpec((1, tq, gw), lambda bi, g, s, sl, qi, kv, fl: (bi, qi[s], g)),
        scratch_shapes=[pltpu.VMEM((NSA_REP, tq, AUG_DIM), jnp.bfloat16),
                        pltpu.VMEM((NSA_REP, tq, HEAD_DIM), jnp.float32),
                        pltpu.VMEM((NSA_REP, tq, AUG_DIM), jnp.float32)],
    )
    return pl.pallas_call(
        functools.partial(_flash_kernel, mode=mode, tq=tq, tk=tk),
        grid_spec=grid_spec,
        out_shape=jax.ShapeDtypeStruct((b, t, NSA_WIDTH), jnp.float32),
        compiler_params=_cparams(("parallel", "parallel", "arbitrary")),
        name="flash_" + mode,
    )(*args)


def _gate_expand_matrix():
    m = np.zeros((NSA_BRANCHES, LANES, NSA_WIDTH), np.float32)
    for br in range(NSA_BRANCHES):
        for h in range(NSA_HEADS):
            m[br, h * NSA_BRANCHES + br, h * HEAD_DIM:(h + 1) * HEAD_DIM] = 1.0
    return m


def _combine_kernel(oc_ref, os_ref, ow_ref, gl_ref, ex_ref, on_ref, o_ref):
    sig = jax.nn.sigmoid(gl_ref[...])
    hi = sig.astype(jnp.bfloat16)
    lo = (sig - hi.astype(jnp.float32)).astype(jnp.bfloat16)
    out = None
    for br, ref in enumerate((oc_ref, os_ref, ow_ref)):
        ex = ex_ref[br]
        gate = (jnp.dot(hi, ex, preferred_element_type=jnp.float32)
                + jnp.dot(lo, ex, preferred_element_type=jnp.float32))
        term = gate * ref[...]
        out = term if out is None else out + term
    o_ref[...] = _rms(out, on_ref[...]).astype(o_ref.dtype)


def nsa_combine(o_cmp, o_slc, o_win, gate_logits, on_gain, tt=256):
    m, w = o_cmp.shape
    row = pl.BlockSpec((tt, w), lambda i: (i, 0))
    return pl.pallas_call(
        _combine_kernel,
        grid=(m // tt,),
        in_specs=[row, row, row,
                  pl.BlockSpec((tt, LANES), lambda i: (i, 0)),
                  pl.BlockSpec((NSA_BRANCHES, LANES, w), lambda i: (0, 0, 0)),
                  pl.BlockSpec((1, w), lambda i: (0, 0))],
        out_specs=row,
        out_shape=jax.ShapeDtypeStruct((m, w), jnp.bfloat16),
        compiler_params=_cparams(("parallel",)),
        name="nsa_combine",
    )(o_cmp, o_slc, o_win, gate_logits,
      jnp.asarray(_gate_expand_matrix(), dtype=jnp.bfloat16), on_gain.reshape(1, w))


def kernel(x, p, mix_norm, w_in, conv_w, q_gain, k_gain, cmp_pos, cmp_w1, cmp_w2, sgu_gain, w_sp,
           b_sp, mix_out_norm, w_out, mlp_norm, w_mlp_in, w_mlp_out, ple_norm, w_ple_proj,
           w_ple_gate):
    b, t, d = x.shape
    m = b * t
    bf = jnp.bfloat16
    depth = p.shape[0]
    p_rows = p.reshape(depth, m, PLE_DIM)
    h2 = x.reshape(m, d)
    for i in range(depth):
        xn = rmsnorm_cast(h2, mix_norm[i])
        w_g, w_gate = cast_w_in_tail(w_in, i)
        z_a = matmul(xn, cast_weight(w_in, i, ncols=Z_A_COLS)).reshape(b, t, Z_A_COLS)
        z_g = matmul(xn, w_g).reshape(b, t, Z_G_COLS)
        gate_logits = matmul(xn, w_gate, tn=LANES)

        on = mix_out_norm[i]
        mixed_a = conv_mixer(z_a, conv_w[i], on[:CONV_WIDTH])
        kc, vc = compress_kv(z_a, cmp_pos[i], cmp_w1[i], cmp_w2[i], k_gain[i])
        o_cmp, sel = cmp_attention(z_a, kc, vc, q_gain[i])
        kv3 = kv_prep(z_a, k_gain[i])
        o_slc = flash_attention("slc", z_a, kv3, q_gain[i], sel=sel)
        o_win = flash_attention("win", z_a, kv3, q_gain[i])
        mixed_b = nsa_combine(o_cmp.reshape(m, NSA_WIDTH), o_slc.reshape(m, NSA_WIDTH),
                              o_win.reshape(m, NSA_WIDTH), gate_logits,
                              on[CONV_WIDTH:CONV_WIDTH + NSA_WIDTH])
        mixed_c = gmlp_mixer(z_g, sgu_gain[i], w_sp[i], b_sp[i], on[CONV_WIDTH + NSA_WIDTH:])
        mixed = jnp.concatenate([mixed_a.reshape(m, CONV_WIDTH), mixed_b,
                                 mixed_c.reshape(m, GMLP_WIDTH)], axis=1)
        h2 = matmul(mixed, cast_weight(w_out, i), mode="residual", res=h2)

        hn = rmsnorm_cast(h2, mlp_norm[i])
        hid = matmul(hn, cast_weight(w_mlp_in, i), mode="relu2", out_dtype=bf)
        h2 = matmul(hid, cast_weight(w_mlp_out, i), mode="residual", res=h2)

        hp = rmsnorm_cast(h2, ple_norm[i])
        h2 = matmul(hp, cast_weight(w_ple_gate, i), mode="ple", res=h2,
                    p=p_rows, wp=w_ple_proj, layer=i)
    return h2.reshape(b, t, d)
```

```python
import functools

import jax
import jax.numpy as jnp
import numpy as np
from jax import lax
from jax.experimental import pallas as pl
from jax.experimental.pallas import tpu as pltpu

D_MODEL = 4096
HEAD_DIM = 128
CONV_WIDTH = D_MODEL // 4
CONV_K = 3
NSA_HEADS = (D_MODEL // 2) // HEAD_DIM
NSA_WIDTH = NSA_HEADS * HEAD_DIM
NSA_KV_HEADS = 4
NSA_REP = NSA_HEADS // NSA_KV_HEADS
NSA_BRANCHES = 3
KV_WIDTH = NSA_KV_HEADS * HEAD_DIM
CMP_LEN = 32
CMP_STRIDE = 16
SLC_BLOCK = 64
SLC_SHIFT = 6
N_SELECT = 16
WINDOW = 512
GMLP_WIDTH = D_MODEL // 4
GMLP_GROUPS = GMLP_WIDTH // HEAD_DIM
GMLP_CHUNK = 128
D_FF = 4 * D_MODEL
PLE_DIM = 256
N_GATES = NSA_HEADS * NSA_BRANCHES
EPS = 1e-6
NEG_INF = -1e30
FORCE_SCORE = 1e9
BELOW_NEG_INF = -3e38

Z_A_COLS = 3 * CONV_WIDTH + NSA_WIDTH + 2 * NSA_BRANCHES * KV_WIDTH
Z_G_COLS = 2 * GMLP_WIDTH
Z_G_OFF = Z_A_COLS + N_GATES
Q_OFF = 3 * CONV_WIDTH
KCMP_OFF = Q_OFF + NSA_WIDTH
VCMP_OFF = KCMP_OFF + KV_WIDTH
KV4_OFF = VCMP_OFF + KV_WIDTH

LANES = 128
VMEM_LIMIT_BYTES = 60 * 1024 * 1024


def _cparams(dims):
    return pltpu.CompilerParams(dimension_semantics=dims, vmem_limit_bytes=VMEM_LIMIT_BYTES)


def _rms(x, gain):
    return x * lax.rsqrt(jnp.mean(x * x, axis=-1, keepdims=True) + EPS) * gain


def _gelu(x):
    return jax.nn.gelu(x)


def _rmsnorm_kernel(x_ref, g_ref, o_ref):
    o_ref[...] = _rms(x_ref[...], g_ref[...]).astype(o_ref.dtype)


def rmsnorm_cast(x, gain, tm=512):
    m, d = x.shape
    return pl.pallas_call(
        _rmsnorm_kernel,
        grid=(m // tm,),
        in_specs=[pl.BlockSpec((tm, d), lambda i: (i, 0)),
                  pl.BlockSpec((1, d), lambda i: (0, 0))],
        out_specs=pl.BlockSpec((tm, d), lambda i: (i, 0)),
        out_shape=jax.ShapeDtypeStruct((m, d), jnp.bfloat16),
        compiler_params=_cparams(("parallel",)),
        name="rmsnorm_cast",
    )(x, gain.reshape(1, d))


def _cast_kernel(w_ref, o_ref):
    o_ref[...] = w_ref[...].astype(o_ref.dtype)


def _cast_t_kernel(a_ref, b_ref, o_ref, *, shift, keep):
    x = a_ref[...]
    if shift:
        x = jnp.concatenate([x[shift:, :], b_ref[:shift, :]], axis=0)
    if keep < x.shape[0]:
        row = lax.broadcasted_iota(jnp.int32, x.shape, 0)
        x = jnp.where(row < keep, x, 0.0)
    o_ref[...] = x.T.astype(o_ref.dtype)


def cast_weight(w, layer, ncols=None, tk=512, tn=2048):
    _, kdim, n = w.shape
    ncols = ncols or n
    tk, tn = min(tk, kdim), min(tn, ncols)
    assert kdim % tk == 0 and ncols % tn == 0
    return pl.pallas_call(
        _cast_kernel,
        grid=(kdim // tk, ncols // tn),
        in_specs=[pl.BlockSpec((None, tk, tn), lambda i, j: (layer, i, j))],
        out_specs=pl.BlockSpec((tk, tn), lambda i, j: (i, j)),
        out_shape=jax.ShapeDtypeStruct((kdim, ncols), jnp.bfloat16),
        compiler_params=_cparams(("parallel", "parallel")),
        name="cast_weight",
    )(w)


def cast_weight_t(w_t, layer, row0, nrows, keep=None, tr=512, tc=512):
    _, n, kdim = w_t.shape
    tr, tc = min(tr, nrows), min(tc, kdim)
    shift = row0 % tr
    base = row0 // tr
    assert nrows % tr == 0 and kdim % tc == 0 and shift % 8 == 0
    block = lambda off: pl.BlockSpec((None, tr, tc), lambda i, j: (layer, base + j + off, i))
    return pl.pallas_call(
        functools.partial(_cast_t_kernel, shift=shift, keep=keep or nrows),
        grid=(kdim // tc, nrows // tr),
        in_specs=[block(0), block(1 if shift else 0)],
        out_specs=pl.BlockSpec((tc, tr), lambda i, j: (i, j)),
        out_shape=jax.ShapeDtypeStruct((kdim, nrows), jnp.bfloat16),
        compiler_params=_cparams(("parallel", "parallel")),
        name="cast_weight_t",
    )(w_t, w_t)


def _mm_kernel(*refs, mode, nk, n_x):
    x_refs, refs = refs[:n_x], refs[n_x:]
    if mode == "ple":
        w_ref, res_ref, p_ref, wp_ref, o_ref = refs[:5]
        rest = refs[5:]
    elif mode == "residual":
        w_ref, res_ref, o_ref = refs[:3]
        rest = refs[3:]
    else:
        w_ref, o_ref = refs[:2]
        rest = refs[2:]

    def epilogue(acc):
        if mode == "plain":
            o_ref[...] = acc.astype(o_ref.dtype)
        elif mode == "relu2":
            r = jnp.maximum(acc, 0.0)
            o_ref[...] = (r * r).astype(o_ref.dtype)
        elif mode == "residual":
            o_ref[...] = res_ref[...] + acc
        else:
            proj = jnp.dot(p_ref[...].astype(jnp.bfloat16), wp_ref[...].astype(jnp.bfloat16),
                           preferred_element_type=jnp.float32)
            o_ref[...] = res_ref[...] + jax.nn.sigmoid(acc) * proj

    part, off = None, 0
    for x_ref in x_refs:
        kx = x_ref.shape[1]
        d = jnp.dot(x_ref[...], w_ref[off:off + kx, :], preferred_element_type=jnp.float32)
        part = d if part is None else part + d
        off += kx
    if nk == 1:
        epilogue(part)
        return
    k = pl.program_id(2)
    if mode == "residual":
        @pl.when(k == 0)
        def _():
            o_ref[...] = res_ref[...] + part

        @pl.when(k > 0)
        def _():
            o_ref[...] += part
        return
    acc_ref, = rest

    @pl.when(k == 0)
    def _():
        acc_ref[...] = part

    @pl.when(jnp.logical_and(k > 0, k < nk - 1))
    def _():
        acc_ref[...] += part

    @pl.when(k == nk - 1)
    def _():
        epilogue(acc_ref[...] + part)


def matmul(x, w, mode="plain", out_dtype=jnp.float32, res=None, p=None, wp=None, layer=0,
           tm=1024, tn=1024, tk=4096):
    xs = x if isinstance(x, (tuple, list)) else (x,)
    m = xs[0].shape[0]
    kdim, n = w.shape
    assert sum(xi.shape[1] for xi in xs) == kdim
    tm, tn, tk = min(tm, m), min(tn, n), min(tk, kdim)
    nk = kdim // tk
    assert m % tm == 0 and n % tn == 0 and kdim % tk == 0 and (nk == 1 or len(xs) == 1)
    in_specs = [pl.BlockSpec((tm, xi.shape[1] if len(xs) > 1 else tk), lambda i, j, k: (i, k))
                for xi in xs]
    in_specs.append(pl.BlockSpec((tk, tn), lambda i, j, k: (k, j)))
    args = [*xs, w]
    if mode in ("residual", "ple"):
        in_specs.append(pl.BlockSpec((tm, tn), lambda i, j, k: (i, j)))
        args.append(res)
    if mode == "ple":
        pd = p.shape[2]
        in_specs += [pl.BlockSpec((None, tm, pd), lambda i, j, k: (layer, i, 0)),
                     pl.BlockSpec((None, pd, tn), lambda i, j, k: (layer, 0, j))]
        args += [p, wp]
    scratch = [pltpu.VMEM((tm, tn), jnp.float32)] if nk > 1 and mode != "residual" else []
    return pl.pallas_call(
        functools.partial(_mm_kernel, mode=mode, nk=nk, n_x=len(xs)),
        grid=(m // tm, n // tn, nk),
        in_specs=in_specs,
        out_specs=pl.BlockSpec((tm, tn), lambda i, j, k: (i, j)),
        out_shape=jax.ShapeDtypeStruct((m, n), out_dtype),
        scratch_shapes=scratch,
        compiler_params=_cparams(("parallel", "parallel", "arbitrary")),
        name="matmul_" + mode,
    )(*args)


HALO_ROWS = 8


def _conv_kernel(xa_ref, gb_ref, gc_ref, xap_ref, gcp_ref, cw_ref, on_ref, o_ref):
    i = pl.program_id(1)
    inner = gc_ref[0] * xa_ref[0]
    prev = gcp_ref[0] * xap_ref[0]
    prev = jnp.where(i > 0, prev, 0.0)
    tt = inner.shape[0]
    row = lax.broadcasted_iota(jnp.int32, inner.shape, 0)
    m1 = jnp.where(row == 0, prev[HALO_ROWS - 1:HALO_ROWS], pltpu.roll(inner, 1, 0))
    m2 = pltpu.roll(inner, 2, 0)
    m2 = jnp.where(row == 0, prev[HALO_ROWS - 2:HALO_ROWS - 1], m2)
    m2 = jnp.where(row == 1, prev[HALO_ROWS - 1:HALO_ROWS], m2)
    cw = cw_ref[...]
    conv = cw[0:1] * m2 + cw[1:2] * m1 + cw[2:3] * inner
    o_ref[0] = _rms(gb_ref[0] * conv, on_ref[...]).astype(o_ref.dtype)


def conv_mixer(z3, conv_w, on_gain, tt=512):
    b, t, _ = z3.shape
    c = CONV_WIDTH
    hb = tt // HALO_ROWS
    cur = lambda col: pl.BlockSpec((1, tt, c), lambda bi, i: (bi, i, col))
    halo = lambda col: pl.BlockSpec((1, HALO_ROWS, c),
                                    lambda bi, i: (bi, jnp.maximum(i * hb - 1, 0), col))
    return pl.pallas_call(
        _conv_kernel,
        grid=(b, t // tt),
        in_specs=[cur(0), cur(1), cur(2), halo(0), halo(2),
                  pl.BlockSpec((HALO_ROWS, c), lambda bi, i: (0, 0)),
                  pl.BlockSpec((1, c), lambda bi, i: (0, 0))],
        out_specs=pl.BlockSpec((1, tt, c), lambda bi, i: (bi, i, 0)),
        out_shape=jax.ShapeDtypeStruct((b, t, c), jnp.bfloat16),
        compiler_params=_cparams(("parallel", "parallel")),
        name="conv_mixer",
    )(z3, z3, z3, z3, z3,
      jnp.pad(conv_w, ((0, HALO_ROWS - CONV_K), (0, 0))), on_gain.reshape(1, c))


def _gmlp_kernel(gu_ref, gv_ref, sg_ref, wsp_ref, bexp_ref, on_ref, o_ref, *, n_chunks):
    u = _gelu(gu_ref[0])
    v = _rms(_gelu(gv_ref[0]), sg_ref[...]).astype(jnp.bfloat16)
    ck = GMLP_CHUNK
    r_i = lax.broadcasted_iota(jnp.int32, (ck, ck), 0)
    c_i = lax.broadcasted_iota(jnp.int32, (ck, ck), 1)
    bexp = bexp_ref[...]
    rows = []
    for c in range(n_chunks):
        cols = []
        for g in range(GMLP_GROUPS):
            w = jnp.where(r_i >= c_i, wsp_ref[g], 0.0).astype(jnp.bfloat16)
            vg = v[c * ck:(c + 1) * ck, g * HEAD_DIM:(g + 1) * HEAD_DIM]
            cols.append(jnp.dot(w, vg, preferred_element_type=jnp.float32))
        rows.append(jnp.concatenate(cols, axis=1) + bexp)
    spatial = jnp.concatenate(rows, axis=0) if n_chunks > 1 else rows[0]
    o_ref[0] = _rms(u * spatial, on_ref[...]).astype(o_ref.dtype)


def gmlp_mixer(z3, sgu_gain, w_sp, b_sp, on_gain, tt=256):
    b, t, _ = z3.shape
    c = GMLP_WIDTH
    bexp = jnp.repeat(b_sp.T, HEAD_DIM, axis=1)
    const2 = lambda shape: pl.BlockSpec(shape, lambda bi, i: (0,) * len(shape))
    return pl.pallas_call(
        functools.partial(_gmlp_kernel, n_chunks=tt // GMLP_CHUNK),
        grid=(b, t // tt),
        in_specs=[pl.BlockSpec((1, tt, c), lambda bi, i: (bi, i, 0)),
                  pl.BlockSpec((1, tt, c), lambda bi, i: (bi, i, 1)),
                  const2((1, c)), const2((GMLP_GROUPS, GMLP_CHUNK, GMLP_CHUNK)),
                  const2((GMLP_CHUNK, c)), const2((1, c))],
        out_specs=pl.BlockSpec((1, tt, c), lambda bi, i: (bi, i, 0)),
        out_shape=jax.ShapeDtypeStruct((b, t, c), jnp.bfloat16),
        compiler_params=_cparams(("parallel", "parallel")),
        name="gmlp_mixer",
    )(z3, z3, sgu_gain.reshape(1, c), w_sp, bexp, on_gain.reshape(1, c))


AUG_DIM = 2 * HEAD_DIM
POS_LANE = 64
ONE_LANE = 70
FEAT_END = 73
MASK_BIG = 2.0 ** 100
MASKED = -MASK_BIG
M_INIT = NEG_INF
LOG2E = 1.4426950408889634


def _kvprep_kernel(z_ref, kg_ref, o_ref, *, tt):
    i = pl.program_id(1)
    z = z_ref[0]
    pos = i * tt + lax.broadcasted_iota(jnp.int32, (tt, HEAD_DIM), 0)
    lane = lax.broadcasted_iota(jnp.int32, (tt, HEAD_DIM), 1)
    blk = lax.shift_right_logical(pos, SLC_SHIFT)
    hi = lax.shift_left(blk, SLC_SHIFT).astype(jnp.float32)
    lo = jnp.bitwise_and(pos, SLC_BLOCK - 1).astype(jnp.float32)
    posf = jnp.where(jnp.bitwise_and(lane, 1) == 0, hi, lo)
    tail = jnp.where(lane < ONE_LANE, posf, jnp.where(lane < FEAT_END, 1.0, 0.0))
    onehot = jnp.where(lane == blk, MASK_BIG, 0.0)
    kfeat = (jnp.where(lane < POS_LANE, onehot, tail).astype(o_ref.dtype),
             jnp.where(lane < POS_LANE, 0.0, tail).astype(o_ref.dtype))
    vfeat = jnp.ones((tt, HEAD_DIM), o_ref.dtype)
    for part in range(4):
        for g in range(NSA_KV_HEADS):
            src = part * KV_WIDTH + g * HEAD_DIM
            dst = (part * NSA_KV_HEADS + g) * AUG_DIM
            x = z[:, src:src + HEAD_DIM]
            if part % 2 == 0:
                x = _rms(x, kg_ref[1 + part // 2:2 + part // 2, :])
            o_ref[0, :, dst:dst + HEAD_DIM] = x.astype(o_ref.dtype)
            o_ref[0, :, dst + HEAD_DIM:dst + AUG_DIM] = kfeat[part // 2] if part % 2 == 0 else vfeat


def kv_prep(z3, k_gain, tt=256):
    b, t, _ = z3.shape
    w = 4 * KV_WIDTH
    wo = 4 * NSA_KV_HEADS * AUG_DIM
    kg = jnp.pad(k_gain, ((0, 8 - NSA_BRANCHES), (0, 0)))
    return pl.pallas_call(
        functools.partial(_kvprep_kernel, tt=tt),
        grid=(b, t // tt),
        in_specs=[pl.BlockSpec((1, tt, w), lambda bi, i: (bi, i, KV4_OFF // w)),
                  pl.BlockSpec((8, HEAD_DIM), lambda bi, i: (0, 0))],
        out_specs=pl.BlockSpec((1, tt, wo), lambda bi, i: (bi, i, 0)),
        out_shape=jax.ShapeDtypeStruct((b, t, wo), jnp.bfloat16),
        compiler_params=_cparams(("parallel", "parallel")),
        name="kv_prep",
    )(z3, kg)


def _compress_kernel(k_ref, v_ref, w1_ref, w2_ref, pe_ref, kg_ref, kc_ref, vc_ref, *, n_chunk):
    def one(src_ref, which):
        ya = jnp.zeros((n_chunk, HEAD_DIM), jnp.float32)
        yb = jnp.zeros((n_chunk, HEAD_DIM), jnp.float32)
        bias = jnp.zeros((8, HEAD_DIM), jnp.float32)
        for l in range(CMP_STRIDE):
            xl = src_ref[0, pl.ds(l, n_chunk, stride=CMP_STRIDE), :].astype(jnp.bfloat16)
            ya += jnp.dot(xl, w1_ref[which, l], preferred_element_type=jnp.float32)
            yb += jnp.dot(xl, w1_ref[which, CMP_STRIDE + l], preferred_element_type=jnp.float32)
        for l in range(CMP_LEN):
            pe_l = jnp.broadcast_to(pe_ref[which, l:l + 1, :], (8, HEAD_DIM)).astype(jnp.bfloat16)
            bias += jnp.dot(pe_l, w1_ref[which, l], preferred_element_type=jnp.float32)
        hid = _gelu(ya + pltpu.roll(yb, n_chunk - 1, 0) + bias[0:1])
        return jnp.dot(hid.astype(jnp.bfloat16), w2_ref[which], preferred_element_type=jnp.float32)

    kc_ref[0, 0] = _rms(one(k_ref, 0), kg_ref[0:1, :]).astype(kc_ref.dtype)
    vc_ref[0, 0] = one(v_ref, 1).astype(vc_ref.dtype)


def compress_kv(z3, cmp_pos, cmp_w1, cmp_w2, k_gain):
    b, t, _ = z3.shape
    n_chunk = t // CMP_STRIDE
    kg = jnp.pad(k_gain, ((0, 8 - NSA_BRANCHES), (0, 0)))
    const = lambda shape: pl.BlockSpec(shape, lambda bi, g: (0,) * len(shape))
    out = jax.ShapeDtypeStruct((b, NSA_KV_HEADS, n_chunk, HEAD_DIM), jnp.bfloat16)
    return pl.pallas_call(
        functools.partial(_compress_kernel, n_chunk=n_chunk),
        grid=(b, NSA_KV_HEADS),
        in_specs=[pl.BlockSpec((1, t, HEAD_DIM), lambda bi, g: (bi, 0, KCMP_OFF // HEAD_DIM + g)),
                  pl.BlockSpec((1, t, HEAD_DIM), lambda bi, g: (bi, 0, VCMP_OFF // HEAD_DIM + g)),
                  const((2, CMP_LEN, HEAD_DIM, HEAD_DIM)), const((2, HEAD_DIM, HEAD_DIM)),
                  const((2, CMP_LEN, HEAD_DIM)), const((8, HEAD_DIM))],
        out_specs=[pl.BlockSpec((1, 1, n_chunk, HEAD_DIM), lambda bi, g: (bi, g, 0, 0))] * 2,
        out_shape=[out, out],
        compiler_params=_cparams(("parallel", "parallel")),
        name="compress_kv",
    )(z3, z3, cmp_w1.astype(jnp.bfloat16), cmp_w2.astype(jnp.bfloat16), cmp_pos, kg)


def _alibi_slopes():
    h = np.arange(1, NSA_HEADS + 1, dtype=np.float32)
    return np.power(np.float32(2.0), -8.0 * h / NSA_HEADS).astype(np.float32)


def _overlap_matrix(n_chunk, n_slc):
    n_cmp = n_chunk - CMP_LEN // CMP_STRIDE + 1
    jc = np.arange(n_cmp)[:, None] * CMP_STRIDE
    bs = np.arange(n_slc)[None, :] * SLC_BLOCK
    ov = np.clip(np.minimum(jc + CMP_LEN, bs + SLC_BLOCK) - np.maximum(jc, bs), 0, None)
    out = np.zeros((n_chunk, LANES), np.float32)
    out[:n_cmp, :n_slc] = ov.astype(np.float32) / CMP_LEN
    return out


def _q_heads(q_ref, qg_ref):
    q = q_ref[0]
    scale = HEAD_DIM ** -0.5
    return [(_rms(q[:, r * HEAD_DIM:(r + 1) * HEAD_DIM], qg_ref[...]) * scale).astype(jnp.bfloat16)
            for r in range(NSA_REP)]


def _cmp_kernel(slopes_ref, q_ref, kc_ref, vc_ref, qg_ref, ov_ref, o_ref, sel_ref,
                *, tq, n_chunk, n_slc):
    g = pl.program_id(1)
    i = pl.program_id(2)
    n_cmp = n_chunk - CMP_LEN // CMP_STRIDE + 1
    qh = _q_heads(q_ref, qg_ref)
    kc = kc_ref[0, 0]
    vc = vc_ref[0, 0]
    t = i * tq + lax.broadcasted_iota(jnp.int32, (tq, n_chunk), 0)
    n = lax.broadcasted_iota(jnp.int32, (tq, n_chunk), 1)
    dist = t - (n * CMP_STRIDE + CMP_LEN - 1)
    mask = jnp.logical_and(dist >= 0, n < n_cmp)
    distf = dist.astype(jnp.float32)
    psum = jnp.zeros((tq, n_chunk), jnp.float32)
    for r in range(NSA_REP):
        s = lax.dot_general(qh[r], kc, (((1,), (1,)), ((), ())),
                            preferred_element_type=jnp.float32)
        s = jnp.where(mask, s - slopes_ref[g * NSA_REP + r] * distf, NEG_INF)
        mx = jnp.max(s, axis=-1, keepdims=True)
        e = jnp.where(mask, jnp.exp(s - mx), 0.0)
        den = jnp.sum(e, axis=-1, keepdims=True)
        p = e / jnp.where(den > 0.0, den, 1.0)
        psum += p
        o_ref[0, :, r * HEAD_DIM:(r + 1) * HEAD_DIM] = jnp.dot(
            p.astype(jnp.bfloat16), vc, preferred_element_type=jnp.float32)

    ov = ov_ref[...]
    imp = jnp.zeros((tq, LANES), jnp.float32)
    rem = psum
    for _ in range(3):
        part = rem.astype(jnp.bfloat16)
        imp += jnp.dot(part, ov, preferred_element_type=jnp.float32)
        rem = rem - part.astype(jnp.float32)

    tb = i * tq + lax.broadcasted_iota(jnp.int32, (tq, LANES), 0)
    blk = lax.broadcasted_iota(jnp.int32, (tq, LANES), 1)
    qblk = lax.shift_right_logical(tb, SLC_SHIFT)
    forced = (blk == 0) | (blk == qblk) | (blk == qblk - 1)
    score = jnp.where(forced, FORCE_SCORE, jnp.where(blk <= qblk, imp, NEG_INF))
    score = jnp.where(blk < n_slc, score, BELOW_NEG_INF)
    st = score.T[0:n_slc]
    sub = lax.broadcasted_iota(jnp.int32, (n_slc, tq), 0)
    cnt = jnp.zeros((n_slc, tq), jnp.float32)
    for m in range(n_slc):
        row = st[m:m + 1, :]
        ahead = (row > st) | ((row == st) & (sub > m))
        cnt += jnp.where(ahead, 1.0, 0.0)
    sel_t = jnp.where(cnt < float(min(N_SELECT, n_slc)), 1.0, 0.0)
    if n_slc < LANES:
        sel_t = jnp.concatenate([sel_t, jnp.zeros((LANES - n_slc, tq), jnp.float32)], axis=0)
    sel_ref[0, 0] = sel_t.T.astype(sel_ref.dtype)


def cmp_attention(z3, kc, vc, q_gain, tq=256):
    b, t, _ = z3.shape
    n_chunk = t // CMP_STRIDE
    n_slc = t // SLC_BLOCK
    assert n_slc <= LANES
    gw = NSA_REP * HEAD_DIM
    grid_spec = pltpu.PrefetchScalarGridSpec(
        num_scalar_prefetch=1,
        grid=(b, NSA_KV_HEADS, t // tq),
        in_specs=[pl.BlockSpec((1, tq, gw), lambda bi, g, i, s: (bi, i, Q_OFF // gw + g)),
                  pl.BlockSpec((1, 1, n_chunk, HEAD_DIM), lambda bi, g, i, s: (bi, g, 0, 0)),
                  pl.BlockSpec((1, 1, n_chunk, HEAD_DIM), lambda bi, g, i, s: (bi, g, 0, 0)),
                  pl.BlockSpec((1, HEAD_DIM), lambda bi, g, i, s: (0, 0)),
                  pl.BlockSpec((n_chunk, LANES), lambda bi, g, i, s: (0, 0))],
        out_specs=[pl.BlockSpec((1, tq, gw), lambda bi, g, i, s: (bi, i, g)),
                   pl.BlockSpec((1, 1, tq, LANES), lambda bi, g, i, s: (bi, g, i, 0))],
    )
    return pl.pallas_call(
        functools.partial(_cmp_kernel, tq=tq, n_chunk=n_chunk, n_slc=n_slc),
        grid_spec=grid_spec,
        out_shape=[jax.ShapeDtypeStruct((b, t, NSA_WIDTH), jnp.float32),
                   jax.ShapeDtypeStruct((b, NSA_KV_HEADS, t, LANES), jnp.bfloat16)],
        compiler_params=_cparams(("parallel", "parallel", "parallel")),
        name="cmp_attention",
    )(jnp.asarray(_alibi_slopes()), z3, kc, vc, q_gain.reshape(1, HEAD_DIM),
      jnp.asarray(_overlap_matrix(n_chunk, n_slc), dtype=jnp.bfloat16))


STEP_FIRST, STEP_LAST, STEP_MASKED = 1, 2, 4


def _flash_steps(mode, t, tq, tk):
    qi, kv, fl = [], [], []
    for i in range(t // tq):
        lo, hi = i * tq, i * tq + tq - 1
        first_key = 0 if mode == "slc" else max(lo - WINDOW + 1, 0)
        tiles = list(range(first_key // tk, hi // tk + 1))
        for n, j in enumerate(tiles):
            k_lo, k_hi = j * tk, j * tk + tk - 1
            causal_ok = k_hi <= lo
            window_ok = mode == "slc" or hi - k_lo < WINDOW
            flag = (STEP_FIRST if n == 0 else 0) | (STEP_LAST if n == len(tiles) - 1 else 0)
            flag |= 0 if (causal_ok and window_ok) else STEP_MASKED
            qi.append(i), kv.append(j), fl.append(flag)
    return [np.asarray(a, np.int32) for a in (qi, kv, fl)]


def _split3(x):
    a = x.astype(jnp.bfloat16).astype(jnp.float32)
    b = (x - a).astype(jnp.bfloat16).astype(jnp.float32)
    c = (x - a - b).astype(jnp.bfloat16).astype(jnp.float32)
    return a, b, c


def _flash_kernel(slopes_ref, qi_ref, kv_ref, fl_ref, *refs, mode, tq, tk):
    if mode == "slc":
        q_ref, k_ref, v_ref, qg_ref, sel_ref, o_ref, q_sc, m_sc, acc_sc = refs
    else:
        q_ref, k_ref, v_ref, qg_ref, o_ref, q_sc, m_sc, acc_sc = refs
    g = pl.program_id(1)
    s = pl.program_id(2)
    i = qi_ref[s]
    kv_idx = kv_ref[s]
    flags = fl_ref[s]

    @pl.when(jnp.bitwise_and(flags, STEP_FIRST) != 0)
    def _():
        q = q_ref[0]
        lane = lax.broadcasted_iota(jnp.int32, (tq, HEAD_DIM), 1)
        if mode == "slc":
            sel_m1 = sel_ref[0, 0].astype(jnp.float32) - 1.0
        else:
            sel_m1 = jnp.zeros((tq, HEAD_DIM), jnp.float32)
        t0 = (i * tq).astype(jnp.float32)
        for r in range(NSA_REP):
            qn = _rms(q[:, r * HEAD_DIM:(r + 1) * HEAD_DIM], qg_ref[...]) * (HEAD_DIM ** -0.5 * LOG2E)
            slope2 = slopes_ref[g * NSA_REP + r] * LOG2E
            a, b, c = _split3(jnp.where(lane < ONE_LANE, slope2, -slope2 * t0))
            k3 = jnp.where(lane < ONE_LANE, lax.shift_right_logical(lane - POS_LANE, 1), lane - ONE_LANE)
            term = jnp.where(k3 == 0, a, jnp.where(k3 == 1, b, c))
            feat = jnp.where(lane < POS_LANE, sel_m1, jnp.where(lane < FEAT_END, term, 0.0))
            q_sc[r, :, 0:HEAD_DIM] = qn.astype(q_sc.dtype)
            q_sc[r, :, HEAD_DIM:AUG_DIM] = feat.astype(q_sc.dtype)
        m_sc[...] = jnp.full(m_sc.shape, M_INIT, jnp.float32)
        acc_sc[...] = jnp.zeros(acc_sc.shape, jnp.float32)

    def step(masked):
        k = k_ref[0]
        v = v_ref[0]
        if masked:
            t = i * tq + lax.broadcasted_iota(jnp.int32, (tq, tk), 0)
            dist = t - (kv_idx * tk + lax.broadcasted_iota(jnp.int32, (tq, tk), 1))
            ok = dist >= 0 if mode == "slc" else jnp.logical_and(dist >= 0, dist < WINDOW)
        for r in range(NSA_REP):
            s2 = lax.dot_general(q_sc[r], k, (((1,), (1,)), ((), ())),
                                 preferred_element_type=jnp.float32)
            if masked:
                s2 = jnp.where(ok, s2, MASKED)
            m_prev = m_sc[r]
            m_new = jnp.maximum(m_prev, jnp.max(s2, axis=-1, keepdims=True))
            p = jnp.exp2(s2 - jnp.tile(m_new, (1, tk // HEAD_DIM))).astype(jnp.bfloat16)
            alpha = jnp.exp2(m_prev - m_new)
            acc_sc[r] = jnp.tile(alpha, (1, AUG_DIM // HEAD_DIM)) * acc_sc[r] + jnp.dot(
                p, v, preferred_element_type=jnp.float32)
            m_sc[r] = m_new

    is_masked = jnp.bitwise_and(flags, STEP_MASKED) != 0
    pl.when(jnp.logical_not(is_masked))(functools.partial(step, False))
    pl.when(is_masked)(functools.partial(step, True))

    @pl.when(jnp.bitwise_and(flags, STEP_LAST) != 0)
    def _():
        for r in range(NSA_REP):
            acc = acc_sc[r]
            o_ref[0, :, r * HEAD_DIM:(r + 1) * HEAD_DIM] = acc[:, 0:HEAD_DIM] / acc[:, HEAD_DIM:AUG_DIM]


def flash_attention(mode, z3, kv3, q_gain, sel=None, tq=None, tk=None):
    b, t, _ = z3.shape
    if mode == "slc":
        tq, tk = tq or 512, tk or 512
        k_col, v_col = 0, NSA_KV_HEADS
    else:
        tq, tk = tq or 256, tk or 256
        k_col, v_col = 2 * NSA_KV_HEADS, 3 * NSA_KV_HEADS
    tq, tk = min(tq, t), min(tk, t)
    gw = NSA_REP * HEAD_DIM
    qi, kv, fl = _flash_steps(mode, t, tq, tk)

    in_specs = [pl.BlockSpec((1, tq, gw), lambda bi, g, s, sl, qi, kv, fl: (bi, qi[s], Q_OFF // gw + g)),
                pl.BlockSpec((1, tk, AUG_DIM), lambda bi, g, s, sl, qi, kv, fl: (bi, kv[s], k_col + g)),
                pl.BlockSpec((1, tk, AUG_DIM), lambda bi, g, s, sl, qi, kv, fl: (bi, kv[s], v_col + g)),
                pl.BlockSpec((1, HEAD_DIM), lambda bi, g, s, sl, qi, kv, fl: (0, 0))]
    args = [jnp.asarray(_alibi_slopes()), jnp.asarray(qi), jnp.asarray(kv), jnp.asarray(fl),
            z3, kv3, kv3, q_gain.reshape(1, HEAD_DIM)]
    if mode == "slc":
        in_specs.append(pl.BlockSpec((1, 1, tq, LANES), lambda bi, g, s, sl, qi, kv, fl: (bi, g, qi[s], 0)))
        args.append(sel)
    grid_spec = pltpu.PrefetchScalarGridSpec(
        num_scalar_prefetch=4,
        grid=(b, NSA_KV_HEADS, len(qi)),
        in_specs=in_specs,
        out_specs=pl.BlockSpec((1, tq, gw), lambda bi, g, s, sl, qi, kv, fl: (bi, qi[s], g)),
        scratch_shapes=[pltpu.VMEM((NSA_REP, tq, AUG_DIM), jnp.bfloat16),
                        pltpu.VMEM((NSA_REP, tq, HEAD_DIM), jnp.float32),
                        pltpu.VMEM((NSA_REP, tq, AUG_DIM), jnp.float32)],
    )
    return pl.pallas_call(
        functools.partial(_flash_kernel, mode=mode, tq=tq, tk=tk),
        grid_spec=grid_spec,
        out_shape=jax.ShapeDtypeStruct((b, t, NSA_WIDTH), jnp.float32),
        compiler_params=_cparams(("parallel", "parallel", "arbitrary")),
        name="flash_" + mode,
    )(*args)


def _gate_expand_matrix():
    m = np.zeros((NSA_BRANCHES, LANES, NSA_WIDTH), np.float32)
    for br in range(NSA_BRANCHES):
        for h in range(NSA_HEADS):
            m[br, h * NSA_BRANCHES + br, h * HEAD_DIM:(h + 1) * HEAD_DIM] = 1.0
    return m


def _combine_kernel(oc_ref, os_ref, ow_ref, gl_ref, ex_ref, on_ref, o_ref):
    sig = jax.nn.sigmoid(gl_ref[...])
    hi = sig.astype(jnp.bfloat16)
    lo = (sig - hi.astype(jnp.float32)).astype(jnp.bfloat16)
    out = None
    for br, ref in enumerate((oc_ref, os_ref, ow_ref)):
        ex = ex_ref[br]
        gate = (jnp.dot(hi, ex, preferred_element_type=jnp.float32)
                + jnp.dot(lo, ex, preferred_element_type=jnp.float32))
        term = gate * ref[...]
        out = term if out is None else out + term
    o_ref[...] = _rms(out, on_ref[...]).astype(o_ref.dtype)


def nsa_combine(o_cmp, o_slc, o_win, gate_logits, on_gain, tt=256):
    m, w = o_cmp.shape
    row = pl.BlockSpec((tt, w), lambda i: (i, 0))
    return pl.pallas_call(
        _combine_kernel,
        grid=(m // tt,),
        in_specs=[row, row, row,
                  pl.BlockSpec((tt, LANES), lambda i: (i, 0)),
                  pl.BlockSpec((NSA_BRANCHES, LANES, w), lambda i: (0, 0, 0)),
                  pl.BlockSpec((1, w), lambda i: (0, 0))],
        out_specs=row,
        out_shape=jax.ShapeDtypeStruct((m, w), jnp.bfloat16),
        compiler_params=_cparams(("parallel",)),
        name="nsa_combine",
    )(o_cmp, o_slc, o_win, gate_logits,
      jnp.asarray(_gate_expand_matrix(), dtype=jnp.bfloat16), on_gain.reshape(1, w))


def kernel(x, p, mix_norm, w_in, conv_w, q_gain, k_gain, cmp_pos, cmp_w1, cmp_w2, sgu_gain, w_sp,
           b_sp, mix_out_norm, w_out, mlp_norm, w_mlp_in, w_mlp_out, ple_norm, w_ple_proj,
           w_ple_gate):
    b, t, d = x.shape
    m = b * t
    bf = jnp.bfloat16
    depth = p.shape[0]
    p_rows = p.reshape(depth, m, PLE_DIM)
    w_in_t = jnp.swapaxes(w_in, 1, 2)
    h2 = x.reshape(m, d)
    for i in range(depth):
        xn = rmsnorm_cast(h2, mix_norm[i])
        z_a = matmul(xn, cast_weight_t(w_in_t, i, 0, Z_A_COLS)).reshape(b, t, Z_A_COLS)
        z_g = matmul(xn, cast_weight_t(w_in_t, i, Z_G_OFF, Z_G_COLS)).reshape(b, t, Z_G_COLS)
        w_gate = cast_weight_t(w_in_t, i, Z_A_COLS, LANES, keep=N_GATES)
        gate_logits = matmul(xn, w_gate, tn=LANES)

        on = mix_out_norm[i]
        mixed_a = conv_mixer(z_a, conv_w[i], on[:CONV_WIDTH])
        kc, vc = compress_kv(z_a, cmp_pos[i], cmp_w1[i], cmp_w2[i], k_gain[i])
        o_cmp, sel = cmp_attention(z_a, kc, vc, q_gain[i])
        kv3 = kv_prep(z_a, k_gain[i])
        o_slc = flash_attention("slc", z_a, kv3, q_gain[i], sel=sel)
        o_win = flash_attention("win", z_a, kv3, q_gain[i])
        mixed_b = nsa_combine(o_cmp.reshape(m, NSA_WIDTH), o_slc.reshape(m, NSA_WIDTH),
                              o_win.reshape(m, NSA_WIDTH), gate_logits,
                              on[CONV_WIDTH:CONV_WIDTH + NSA_WIDTH])
        mixed_c = gmlp_mixer(z_g, sgu_gain[i], w_sp[i], b_sp[i], on[CONV_WIDTH + NSA_WIDTH:])
        mixed = (mixed_a.reshape(m, CONV_WIDTH), mixed_b, mixed_c.reshape(m, GMLP_WIDTH))
        h2 = matmul(mixed, cast_weight(w_out, i), mode="residual", res=h2)

        hn = rmsnorm_cast(h2, mlp_norm[i])
        hid = matmul(hn, cast_weight(w_mlp_in, i), mode="relu2", out_dtype=bf)
        h2 = matmul(hid, cast_weight(w_mlp_out, i), mode="residual", res=h2)

        hp = rmsnorm_cast(h2, ple_norm[i])
        h2 = matmul(hp, cast_weight(w_ple_gate, i), mode="ple", res=h2,
                    p=p_rows, wp=w_ple_proj, layer=i, tn=512)
    return h2.reshape(b, t, d)
```

```python
import functools

import jax
import jax.numpy as jnp
import numpy as np
from jax import lax
from jax.experimental import pallas as pl
from jax.experimental.pallas import tpu as pltpu

D_MODEL = 4096
HEAD_DIM = 128
CONV_WIDTH = D_MODEL // 4
CONV_K = 3
NSA_HEADS = (D_MODEL // 2) // HEAD_DIM
NSA_WIDTH = NSA_HEADS * HEAD_DIM
NSA_KV_HEADS = 4
NSA_REP = NSA_HEADS // NSA_KV_HEADS
NSA_BRANCHES = 3
KV_WIDTH = NSA_KV_HEADS * HEAD_DIM
CMP_LEN = 32
CMP_STRIDE = 16
SLC_BLOCK = 64
SLC_SHIFT = 6
N_SELECT = 16
WINDOW = 512
GMLP_WIDTH = D_MODEL // 4
GMLP_GROUPS = GMLP_WIDTH // HEAD_DIM
GMLP_CHUNK = 128
D_FF = 4 * D_MODEL
PLE_DIM = 256
N_GATES = NSA_HEADS * NSA_BRANCHES
EPS = 1e-6
NEG_INF = -1e30
FORCE_SCORE = 1e9
BELOW_NEG_INF = -3e38

Z_A_COLS = 3 * CONV_WIDTH + NSA_WIDTH + 2 * NSA_BRANCHES * KV_WIDTH
Z_G_COLS = 2 * GMLP_WIDTH
Z_G_OFF = Z_A_COLS + N_GATES
Q_OFF = 3 * CONV_WIDTH
KCMP_OFF = Q_OFF + NSA_WIDTH
VCMP_OFF = KCMP_OFF + KV_WIDTH
KV4_OFF = VCMP_OFF + KV_WIDTH

LANES = 128
VMEM_LIMIT_BYTES = 60 * 1024 * 1024


def _cparams(dims):
    return pltpu.CompilerParams(dimension_semantics=dims, vmem_limit_bytes=VMEM_LIMIT_BYTES)


def _rms(x, gain):
    return x * lax.rsqrt(jnp.mean(x * x, axis=-1, keepdims=True) + EPS) * gain


def _gelu(x):
    return jax.nn.gelu(x)


def _rmsnorm_kernel(x_ref, g_ref, o_ref):
    o_ref[...] = _rms(x_ref[...], g_ref[...]).astype(o_ref.dtype)


def rmsnorm_cast(x, gain, tm=512):
    m, d = x.shape
    return pl.pallas_call(
        _rmsnorm_kernel,
        grid=(m // tm,),
        in_specs=[pl.BlockSpec((tm, d), lambda i: (i, 0)),
                  pl.BlockSpec((1, d), lambda i: (0, 0))],
        out_specs=pl.BlockSpec((tm, d), lambda i: (i, 0)),
        out_shape=jax.ShapeDtypeStruct((m, d), jnp.bfloat16),
        compiler_params=_cparams(("parallel",)),
        name="rmsnorm_cast",
    )(x, gain.reshape(1, d))


def _cast_kernel(w_ref, o_ref):
    o_ref[...] = w_ref[...].astype(o_ref.dtype)


def _cast_t_kernel(a_ref, b_ref, o_ref, *, shift, keep):
    x = a_ref[...]
    if shift:
        x = jnp.concatenate([x[shift:, :], b_ref[:shift, :]], axis=0)
    if keep < x.shape[0]:
        row = lax.broadcasted_iota(jnp.int32, x.shape, 0)
        x = jnp.where(row < keep, x, 0.0)
    o_ref[...] = x.T.astype(o_ref.dtype)


def cast_weight(w, layer, ncols=None, tk=512, tn=2048):
    _, kdim, n = w.shape
    ncols = ncols or n
    tk, tn = min(tk, kdim), min(tn, ncols)
    assert kdim % tk == 0 and ncols % tn == 0
    return pl.pallas_call(
        _cast_kernel,
        grid=(kdim // tk, ncols // tn),
        in_specs=[pl.BlockSpec((None, tk, tn), lambda i, j: (layer, i, j))],
        out_specs=pl.BlockSpec((tk, tn), lambda i, j: (i, j)),
        out_shape=jax.ShapeDtypeStruct((kdim, ncols), jnp.bfloat16),
        compiler_params=_cparams(("parallel", "parallel")),
        name="cast_weight",
    )(w)


def cast_weight_t(w_t, layer, row0, nrows, keep=None, tr=512, tc=2048):
    _, n, kdim = w_t.shape
    tr, tc = min(tr, nrows), min(tc, kdim)
    shift = row0 % tr
    base = row0 // tr
    assert nrows % tr == 0 and kdim % tc == 0 and shift % 8 == 0
    block = lambda off: pl.BlockSpec((None, tr, tc), lambda i, j: (layer, base + j + off, i))
    return pl.pallas_call(
        functools.partial(_cast_t_kernel, shift=shift, keep=keep or nrows),
        grid=(kdim // tc, nrows // tr),
        in_specs=[block(0), block(1 if shift else 0)],
        out_specs=pl.BlockSpec((tc, tr), lambda i, j: (i, j)),
        out_shape=jax.ShapeDtypeStruct((kdim, nrows), jnp.bfloat16),
        compiler_params=_cparams(("parallel", "parallel")),
        name="cast_weight_t",
    )(w_t, w_t)


def _mm_kernel(*refs, mode, nk, n_x):
    x_refs, refs = refs[:n_x], refs[n_x:]
    if mode == "ple":
        w_ref, res_ref, p_ref, wp_ref, o_ref = refs[:5]
        rest = refs[5:]
    elif mode == "residual":
        w_ref, res_ref, o_ref = refs[:3]
        rest = refs[3:]
    else:
        w_ref, o_ref = refs[:2]
        rest = refs[2:]

    def epilogue(acc):
        if mode == "plain":
            o_ref[...] = acc.astype(o_ref.dtype)
        elif mode == "relu2":
            r = jnp.maximum(acc, 0.0)
            o_ref[...] = (r * r).astype(o_ref.dtype)
        elif mode == "residual":
            o_ref[...] = res_ref[...] + acc
        else:
            proj = jnp.dot(p_ref[...].astype(jnp.bfloat16), wp_ref[...].astype(jnp.bfloat16),
                           preferred_element_type=jnp.float32)
            o_ref[...] = res_ref[...] + jax.nn.sigmoid(acc) * proj

    part, off = None, 0
    for x_ref in x_refs:
        kx = x_ref.shape[1]
        d = jnp.dot(x_ref[...], w_ref[off:off + kx, :], preferred_element_type=jnp.float32)
        part = d if part is None else part + d
        off += kx
    if nk == 1:
        epilogue(part)
        return
    k = pl.program_id(2)
    if mode == "residual":
        @pl.when(k == 0)
        def _():
            o_ref[...] = res_ref[...] + part

        @pl.when(k > 0)
        def _():
            o_ref[...] += part
        return
    acc_ref, = rest

    @pl.when(k == 0)
    def _():
        acc_ref[...] = part

    @pl.when(jnp.logical_and(k > 0, k < nk - 1))
    def _():
        acc_ref[...] += part

    @pl.when(k == nk - 1)
    def _():
        epilogue(acc_ref[...] + part)


def matmul(x, w, mode="plain", out_dtype=jnp.float32, res=None, p=None, wp=None, layer=0,
           tm=1024, tn=1024, tk=4096):
    xs = x if isinstance(x, (tuple, list)) else (x,)
    m = xs[0].shape[0]
    kdim, n = w.shape
    assert sum(xi.shape[1] for xi in xs) == kdim
    tm, tn, tk = min(tm, m), min(tn, n), min(tk, kdim)
    nk = kdim // tk
    assert m % tm == 0 and n % tn == 0 and kdim % tk == 0 and (nk == 1 or len(xs) == 1)
    in_specs = [pl.BlockSpec((tm, xi.shape[1] if len(xs) > 1 else tk), lambda i, j, k: (i, k))
                for xi in xs]
    in_specs.append(pl.BlockSpec((tk, tn), lambda i, j, k: (k, j)))
    args = [*xs, w]
    if mode in ("residual", "ple"):
        in_specs.append(pl.BlockSpec((tm, tn), lambda i, j, k: (i, j)))
        args.append(res)
    if mode == "ple":
        pd = p.shape[2]
        in_specs += [pl.BlockSpec((None, tm, pd), lambda i, j, k: (layer, i, 0)),
                     pl.BlockSpec((None, pd, tn), lambda i, j, k: (layer, 0, j))]
        args += [p, wp]
    scratch = [pltpu.VMEM((tm, tn), jnp.float32)] if nk > 1 and mode != "residual" else []
    return pl.pallas_call(
        functools.partial(_mm_kernel, mode=mode, nk=nk, n_x=len(xs)),
        grid=(m // tm, n // tn, nk),
        in_specs=in_specs,
        out_specs=pl.BlockSpec((tm, tn), lambda i, j, k: (i, j)),
        out_shape=jax.ShapeDtypeStruct((m, n), out_dtype),
        scratch_shapes=scratch,
        compiler_params=_cparams(("parallel", "parallel", "arbitrary")),
        name="matmul_" + mode,
    )(*args)


HALO_ROWS = 8


def _conv_kernel(xa_ref, gb_ref, gc_ref, xap_ref, gcp_ref, cw_ref, on_ref, o_ref):
    i = pl.program_id(1)
    inner = gc_ref[0] * xa_ref[0]
    prev = gcp_ref[0] * xap_ref[0]
    prev = jnp.where(i > 0, prev, 0.0)
    tt = inner.shape[0]
    row = lax.broadcasted_iota(jnp.int32, inner.shape, 0)
    m1 = jnp.where(row == 0, prev[HALO_ROWS - 1:HALO_ROWS], pltpu.roll(inner, 1, 0))
    m2 = pltpu.roll(inner, 2, 0)
    m2 = jnp.where(row == 0, prev[HALO_ROWS - 2:HALO_ROWS - 1], m2)
    m2 = jnp.where(row == 1, prev[HALO_ROWS - 1:HALO_ROWS], m2)
    cw = cw_ref[...]
    conv = cw[0:1] * m2 + cw[1:2] * m1 + cw[2:3] * inner
    o_ref[0] = _rms(gb_ref[0] * conv, on_ref[...]).astype(o_ref.dtype)


def conv_mixer(z3, conv_w, on_gain, tt=512):
    b, t, _ = z3.shape
    c = CONV_WIDTH
    hb = tt // HALO_ROWS
    cur = lambda col: pl.BlockSpec((1, tt, c), lambda bi, i: (bi, i, col))
    halo = lambda col: pl.BlockSpec((1, HALO_ROWS, c),
                                    lambda bi, i: (bi, jnp.maximum(i * hb - 1, 0), col))
    return pl.pallas_call(
        _conv_kernel,
        grid=(b, t // tt),
        in_specs=[cur(0), cur(1), cur(2), halo(0), halo(2),
                  pl.BlockSpec((HALO_ROWS, c), lambda bi, i: (0, 0)),
                  pl.BlockSpec((1, c), lambda bi, i: (0, 0))],
        out_specs=pl.BlockSpec((1, tt, c), lambda bi, i: (bi, i, 0)),
        out_shape=jax.ShapeDtypeStruct((b, t, c), jnp.bfloat16),
        compiler_params=_cparams(("parallel", "parallel")),
        name="conv_mixer",
    )(z3, z3, z3, z3, z3,
      jnp.pad(conv_w, ((0, HALO_ROWS - CONV_K), (0, 0))), on_gain.reshape(1, c))


def _gmlp_kernel(gu_ref, gv_ref, sg_ref, wsp_ref, bexp_ref, on_ref, o_ref, *, n_chunks):
    u = _gelu(gu_ref[0])
    v = _rms(_gelu(gv_ref[0]), sg_ref[...]).astype(jnp.bfloat16)
    ck = GMLP_CHUNK
    r_i = lax.broadcasted_iota(jnp.int32, (ck, ck), 0)
    c_i = lax.broadcasted_iota(jnp.int32, (ck, ck), 1)
    bexp = bexp_ref[...]
    rows = []
    for c in range(n_chunks):
        cols = []
        for g in range(GMLP_GROUPS):
            w = jnp.where(r_i >= c_i, wsp_ref[g], 0.0).astype(jnp.bfloat16)
            vg = v[c * ck:(c + 1) * ck, g * HEAD_DIM:(g + 1) * HEAD_DIM]
            cols.append(jnp.dot(w, vg, preferred_element_type=jnp.float32))
        rows.append(jnp.concatenate(cols, axis=1) + bexp)
    spatial = jnp.concatenate(rows, axis=0) if n_chunks > 1 else rows[0]
    o_ref[0] = _rms(u * spatial, on_ref[...]).astype(o_ref.dtype)


def gmlp_mixer(z3, sgu_gain, w_sp, b_sp, on_gain, tt=256):
    b, t, _ = z3.shape
    c = GMLP_WIDTH
    bexp = jnp.repeat(b_sp.T, HEAD_DIM, axis=1)
    const2 = lambda shape: pl.BlockSpec(shape, lambda bi, i: (0,) * len(shape))
    return pl.pallas_call(
        functools.partial(_gmlp_kernel, n_chunks=tt // GMLP_CHUNK),
        grid=(b, t // tt),
        in_specs=[pl.BlockSpec((1, tt, c), lambda bi, i: (bi, i, 0)),
                  pl.BlockSpec((1, tt, c), lambda bi, i: (bi, i, 1)),
                  const2((1, c)), const2((GMLP_GROUPS, GMLP_CHUNK, GMLP_CHUNK)),
                  const2((GMLP_CHUNK, c)), const2((1, c))],
        out_specs=pl.BlockSpec((1, tt, c), lambda bi, i: (bi, i, 0)),
        out_shape=jax.ShapeDtypeStruct((b, t, c), jnp.bfloat16),
        compiler_params=_cparams(("parallel", "parallel")),
        name="gmlp_mixer",
    )(z3, z3, sgu_gain.reshape(1, c), w_sp, bexp, on_gain.reshape(1, c))


AUG_DIM = 2 * HEAD_DIM
POS_LANE = 64
ONE_LANE = 70
FEAT_END = 73
MASK_BIG = 2.0 ** 100
MASKED = -MASK_BIG
M_INIT = NEG_INF
LOG2E = 1.4426950408889634


def _kvprep_kernel(z_ref, kg_ref, o_ref, *, tt):
    i = pl.program_id(1)
    z = z_ref[0]
    pos = i * tt + lax.broadcasted_iota(jnp.int32, (tt, HEAD_DIM), 0)
    lane = lax.broadcasted_iota(jnp.int32, (tt, HEAD_DIM), 1)
    blk = lax.shift_right_logical(pos, SLC_SHIFT)
    hi = lax.shift_left(blk, SLC_SHIFT).astype(jnp.float32)
    lo = jnp.bitwise_and(pos, SLC_BLOCK - 1).astype(jnp.float32)
    posf = jnp.where(jnp.bitwise_and(lane, 1) == 0, hi, lo)
    tail = jnp.where(lane < ONE_LANE, posf, jnp.where(lane < FEAT_END, 1.0, 0.0))
    onehot = jnp.where(lane == blk, MASK_BIG, 0.0)
    kfeat = (jnp.where(lane < POS_LANE, onehot, tail).astype(o_ref.dtype),
             jnp.where(lane < POS_LANE, 0.0, tail).astype(o_ref.dtype))
    vfeat = jnp.ones((tt, HEAD_DIM), o_ref.dtype)
    for part in range(4):
        for g in range(NSA_KV_HEADS):
            src = part * KV_WIDTH + g * HEAD_DIM
            dst = (part * NSA_KV_HEADS + g) * AUG_DIM
            x = z[:, src:src + HEAD_DIM]
            if part % 2 == 0:
                x = _rms(x, kg_ref[1 + part // 2:2 + part // 2, :])
            o_ref[0, :, dst:dst + HEAD_DIM] = x.astype(o_ref.dtype)
            o_ref[0, :, dst + HEAD_DIM:dst + AUG_DIM] = kfeat[part // 2] if part % 2 == 0 else vfeat


def kv_prep(z3, k_gain, tt=256):
    b, t, _ = z3.shape
    w = 4 * KV_WIDTH
    wo = 4 * NSA_KV_HEADS * AUG_DIM
    kg = jnp.pad(k_gain, ((0, 8 - NSA_BRANCHES), (0, 0)))
    return pl.pallas_call(
        functools.partial(_kvprep_kernel, tt=tt),
        grid=(b, t // tt),
        in_specs=[pl.BlockSpec((1, tt, w), lambda bi, i: (bi, i, KV4_OFF // w)),
                  pl.BlockSpec((8, HEAD_DIM), lambda bi, i: (0, 0))],
        out_specs=pl.BlockSpec((1, tt, wo), lambda bi, i: (bi, i, 0)),
        out_shape=jax.ShapeDtypeStruct((b, t, wo), jnp.bfloat16),
        compiler_params=_cparams(("parallel", "parallel")),
        name="kv_prep",
    )(z3, kg)


def _compress_kernel(k_ref, v_ref, w1_ref, w2_ref, pe_ref, kg_ref, kc_ref, vc_ref, *, n_chunk):
    def one(src_ref, which):
        ya = jnp.zeros((n_chunk, HEAD_DIM), jnp.float32)
        yb = jnp.zeros((n_chunk, HEAD_DIM), jnp.float32)
        bias = jnp.zeros((8, HEAD_DIM), jnp.float32)
        for l in range(CMP_STRIDE):
            xl = src_ref[0, pl.ds(l, n_chunk, stride=CMP_STRIDE), :].astype(jnp.bfloat16)
            ya += jnp.dot(xl, w1_ref[which, l], preferred_element_type=jnp.float32)
            yb += jnp.dot(xl, w1_ref[which, CMP_STRIDE + l], preferred_element_type=jnp.float32)
        for l in range(CMP_LEN):
            pe_l = jnp.broadcast_to(pe_ref[which, l:l + 1, :], (8, HEAD_DIM)).astype(jnp.bfloat16)
            bias += jnp.dot(pe_l, w1_ref[which, l], preferred_element_type=jnp.float32)
        hid = _gelu(ya + pltpu.roll(yb, n_chunk - 1, 0) + bias[0:1])
        return jnp.dot(hid.astype(jnp.bfloat16), w2_ref[which], preferred_element_type=jnp.float32)

    kc_ref[0, 0] = _rms(one(k_ref, 0), kg_ref[0:1, :]).astype(kc_ref.dtype)
    vc_ref[0, 0] = one(v_ref, 1).astype(vc_ref.dtype)


def compress_kv(z3, cmp_pos, cmp_w1, cmp_w2, k_gain):
    b, t, _ = z3.shape
    n_chunk = t // CMP_STRIDE
    kg = jnp.pad(k_gain, ((0, 8 - NSA_BRANCHES), (0, 0)))
    const = lambda shape: pl.BlockSpec(shape, lambda bi, g: (0,) * len(shape))
    out = jax.ShapeDtypeStruct((b, NSA_KV_HEADS, n_chunk, HEAD_DIM), jnp.bfloat16)
    return pl.pallas_call(
        functools.partial(_compress_kernel, n_chunk=n_chunk),
        grid=(b, NSA_KV_HEADS),
        in_specs=[pl.BlockSpec((1, t, HEAD_DIM), lambda bi, g: (bi, 0, KCMP_OFF // HEAD_DIM + g)),
                  pl.BlockSpec((1, t, HEAD_DIM), lambda bi, g: (bi, 0, VCMP_OFF // HEAD_DIM + g)),
                  const((2, CMP_LEN, HEAD_DIM, HEAD_DIM)), const((2, HEAD_DIM, HEAD_DIM)),
                  const((2, CMP_LEN, HEAD_DIM)), const((8, HEAD_DIM))],
        out_specs=[pl.BlockSpec((1, 1, n_chunk, HEAD_DIM), lambda bi, g: (bi, g, 0, 0))] * 2,
        out_shape=[out, out],
        compiler_params=_cparams(("parallel", "parallel")),
        name="compress_kv",
    )(z3, z3, cmp_w1.astype(jnp.bfloat16), cmp_w2.astype(jnp.bfloat16), cmp_pos, kg)


def _alibi_slopes():
    h = np.arange(1, NSA_HEADS + 1, dtype=np.float32)
    return np.power(np.float32(2.0), -8.0 * h / NSA_HEADS).astype(np.float32)


def _overlap_matrix(n_chunk, n_slc):
    n_cmp = n_chunk - CMP_LEN // CMP_STRIDE + 1
    jc = np.arange(n_cmp)[:, None] * CMP_STRIDE
    bs = np.arange(n_slc)[None, :] * SLC_BLOCK
    ov = np.clip(np.minimum(jc + CMP_LEN, bs + SLC_BLOCK) - np.maximum(jc, bs), 0, None)
    out = np.zeros((n_chunk, LANES), np.float32)
    out[:n_cmp, :n_slc] = ov.astype(np.float32) / CMP_LEN
    return out


def _q_heads(q_ref, qg_ref):
    q = q_ref[0]
    scale = HEAD_DIM ** -0.5
    return [(_rms(q[:, r * HEAD_DIM:(r + 1) * HEAD_DIM], qg_ref[...]) * scale).astype(jnp.bfloat16)
            for r in range(NSA_REP)]


def _cmp_kernel(slopes_ref, q_ref, kc_ref, vc_ref, qg_ref, ov_ref, o_ref, sel_ref,
                *, tq, n_chunk, n_slc):
    g = pl.program_id(1)
    i = pl.program_id(2)
    n_cmp = n_chunk - CMP_LEN // CMP_STRIDE + 1
    qh = _q_heads(q_ref, qg_ref)
    kc = kc_ref[0, 0]
    vc = vc_ref[0, 0]
    t = i * tq + lax.broadcasted_iota(jnp.int32, (tq, n_chunk), 0)
    n = lax.broadcasted_iota(jnp.int32, (tq, n_chunk), 1)
    dist = t - (n * CMP_STRIDE + CMP_LEN - 1)
    mask = jnp.logical_and(dist >= 0, n < n_cmp)
    distf = dist.astype(jnp.float32)
    psum = jnp.zeros((tq, n_chunk), jnp.float32)
    for r in range(NSA_REP):
        s = lax.dot_general(qh[r], kc, (((1,), (1,)), ((), ())),
                            preferred_element_type=jnp.float32)
        s = jnp.where(mask, s - slopes_ref[g * NSA_REP + r] * distf, NEG_INF)
        mx = jnp.max(s, axis=-1, keepdims=True)
        e = jnp.where(mask, jnp.exp(s - mx), 0.0)
        den = jnp.sum(e, axis=-1, keepdims=True)
        p = e / jnp.where(den > 0.0, den, 1.0)
        psum += p
        o_ref[0, :, r * HEAD_DIM:(r + 1) * HEAD_DIM] = jnp.dot(
            p.astype(jnp.bfloat16), vc, preferred_element_type=jnp.float32)

    ov = ov_ref[...]
    imp = jnp.zeros((tq, LANES), jnp.float32)
    rem = psum
    for _ in range(3):
        part = rem.astype(jnp.bfloat16)
        imp += jnp.dot(part, ov, preferred_element_type=jnp.float32)
        rem = rem - part.astype(jnp.float32)

    tb = i * tq + lax.broadcasted_iota(jnp.int32, (tq, LANES), 0)
    blk = lax.broadcasted_iota(jnp.int32, (tq, LANES), 1)
    qblk = lax.shift_right_logical(tb, SLC_SHIFT)
    forced = (blk == 0) | (blk == qblk) | (blk == qblk - 1)
    score = jnp.where(forced, FORCE_SCORE, jnp.where(blk <= qblk, imp, NEG_INF))
    score = jnp.where(blk < n_slc, score, BELOW_NEG_INF)
    st = score.T[0:n_slc]
    n_tiles = n_slc // 8
    tiles = [st[8 * j:8 * j + 8] for j in range(n_tiles)]
    sub8 = lax.broadcasted_iota(jnp.int32, (8, tq), 0)
    cnt = [jnp.zeros((8, tq), jnp.float32) for _ in range(n_tiles)]
    for m in range(n_slc):
        row = st[m:m + 1, :]
        own, off = divmod(m, 8)
        for j in range(n_tiles):
            if j < own:
                inc = jnp.where(row > tiles[j], 1.0, 0.0)
            elif j > own:
                inc = jnp.where(row >= tiles[j], 1.0, 0.0)
            else:
                later = jnp.where(sub8 > off, 1.0, 0.0)
                inc = (jnp.where(row >= tiles[j], later, 0.0)
                       + jnp.where(row > tiles[j], 1.0 - later, 0.0))
            cnt[j] = cnt[j] + inc
    keep = float(min(N_SELECT, n_slc))
    sel_t = jnp.concatenate([jnp.where(c < keep, 1.0, 0.0) for c in cnt]
                            + [jnp.zeros((LANES - n_slc, tq), jnp.float32)] * (n_slc < LANES), axis=0)
    sel_ref[0, 0] = sel_t.T.astype(sel_ref.dtype)


def cmp_attention(z3, kc, vc, q_gain, tq=512):
    b, t, _ = z3.shape
    n_chunk = t // CMP_STRIDE
    n_slc = t // SLC_BLOCK
    assert n_slc <= LANES
    gw = NSA_REP * HEAD_DIM
    grid_spec = pltpu.PrefetchScalarGridSpec(
        num_scalar_prefetch=1,
        grid=(b, NSA_KV_HEADS, t // tq),
        in_specs=[pl.BlockSpec((1, tq, gw), lambda bi, g, i, s: (bi, i, Q_OFF // gw + g)),
                  pl.BlockSpec((1, 1, n_chunk, HEAD_DIM), lambda bi, g, i, s: (bi, g, 0, 0)),
                  pl.BlockSpec((1, 1, n_chunk, HEAD_DIM), lambda bi, g, i, s: (bi, g, 0, 0)),
                  pl.BlockSpec((1, HEAD_DIM), lambda bi, g, i, s: (0, 0)),
                  pl.BlockSpec((n_chunk, LANES), lambda bi, g, i, s: (0, 0))],
        out_specs=[pl.BlockSpec((1, tq, gw), lambda bi, g, i, s: (bi, i, g)),
                   pl.BlockSpec((1, 1, tq, LANES), lambda bi, g, i, s: (bi, g, i, 0))],
    )
    return pl.pallas_call(
        functools.partial(_cmp_kernel, tq=tq, n_chunk=n_chunk, n_slc=n_slc),
        grid_spec=grid_spec,
        out_shape=[jax.ShapeDtypeStruct((b, t, NSA_WIDTH), jnp.float32),
                   jax.ShapeDtypeStruct((b, NSA_KV_HEADS, t, LANES), jnp.bfloat16)],
        compiler_params=_cparams(("parallel", "parallel", "parallel")),
        name="cmp_attention",
    )(jnp.asarray(_alibi_slopes()), z3, kc, vc, q_gain.reshape(1, HEAD_DIM),
      jnp.asarray(_overlap_matrix(n_chunk, n_slc), dtype=jnp.bfloat16))


STEP_FIRST, STEP_LAST, STEP_MASKED = 1, 2, 4


def _flash_steps(mode, t, tq, tk):
    qi, kv, fl = [], [], []
    for i in range(t // tq):
        lo, hi = i * tq, i * tq + tq - 1
        first_key = 0 if mode == "slc" else max(lo - WINDOW + 1, 0)
        tiles = list(range(first_key // tk, hi // tk + 1))
        for n, j in enumerate(tiles):
            k_lo, k_hi = j * tk, j * tk + tk - 1
            causal_ok = k_hi <= lo
            window_ok = mode == "slc" or hi - k_lo < WINDOW
            flag = (STEP_FIRST if n == 0 else 0) | (STEP_LAST if n == len(tiles) - 1 else 0)
            flag |= 0 if (causal_ok and window_ok) else STEP_MASKED
            qi.append(i), kv.append(j), fl.append(flag)
    return [np.asarray(a, np.int32) for a in (qi, kv, fl)]


def _split3(x):
    a = x.astype(jnp.bfloat16).astype(jnp.float32)
    b = (x - a).astype(jnp.bfloat16).astype(jnp.float32)
    c = (x - a - b).astype(jnp.bfloat16).astype(jnp.float32)
    return a, b, c


def _flash_kernel(slopes_ref, qi_ref, kv_ref, fl_ref, *refs, mode, tq, tk):
    if mode == "slc":
        q_ref, k_ref, v_ref, qg_ref, sel_ref, o_ref, q_sc, m_sc, acc_sc = refs
    else:
        q_ref, k_ref, v_ref, qg_ref, o_ref, q_sc, m_sc, acc_sc = refs
    g = pl.program_id(1)
    s = pl.program_id(2)
    i = qi_ref[s]
    kv_idx = kv_ref[s]
    flags = fl_ref[s]

    @pl.when(jnp.bitwise_and(flags, STEP_FIRST) != 0)
    def _():
        q = q_ref[0]
        lane = lax.broadcasted_iota(jnp.int32, (tq, HEAD_DIM), 1)
        if mode == "slc":
            sel_m1 = sel_ref[0, 0].astype(jnp.float32) - 1.0
        else:
            sel_m1 = jnp.zeros((tq, HEAD_DIM), jnp.float32)
        t0 = (i * tq).astype(jnp.float32)
        for r in range(NSA_REP):
            qn = _rms(q[:, r * HEAD_DIM:(r + 1) * HEAD_DIM], qg_ref[...]) * (HEAD_DIM ** -0.5 * LOG2E)
            slope2 = slopes_ref[g * NSA_REP + r] * LOG2E
            a, b, c = _split3(jnp.where(lane < ONE_LANE, slope2, -slope2 * t0))
            k3 = jnp.where(lane < ONE_LANE, lax.shift_right_logical(lane - POS_LANE, 1), lane - ONE_LANE)
            term = jnp.where(k3 == 0, a, jnp.where(k3 == 1, b, c))
            feat = jnp.where(lane < POS_LANE, sel_m1, jnp.where(lane < FEAT_END, term, 0.0))
            q_sc[r, :, 0:HEAD_DIM] = qn.astype(q_sc.dtype)
            q_sc[r, :, HEAD_DIM:AUG_DIM] = feat.astype(q_sc.dtype)
        m_sc[...] = jnp.full(m_sc.shape, M_INIT, jnp.float32)
        acc_sc[...] = jnp.zeros(acc_sc.shape, jnp.float32)

    def step(masked):
        k = k_ref[0]
        v = v_ref[0]
        if masked:
            t = i * tq + lax.broadcasted_iota(jnp.int32, (tq, tk), 0)
            dist = t - (kv_idx * tk + lax.broadcasted_iota(jnp.int32, (tq, tk), 1))
            ok = dist >= 0 if mode == "slc" else jnp.logical_and(dist >= 0, dist < WINDOW)
        for r in range(NSA_REP):
            s2 = lax.dot_general(q_sc[r], k, (((1,), (1,)), ((), ())),
                                 preferred_element_type=jnp.float32)
            if masked:
                s2 = jnp.where(ok, s2, MASKED)
            m_prev = m_sc[r]
            m_new = jnp.maximum(m_prev, jnp.max(s2, axis=-1, keepdims=True))
            p = jnp.exp2(s2 - jnp.tile(m_new, (1, tk // HEAD_DIM))).astype(jnp.bfloat16)
            alpha = jnp.exp2(m_prev - m_new)
            acc_sc[r] = jnp.tile(alpha, (1, AUG_DIM // HEAD_DIM)) * acc_sc[r] + jnp.dot(
                p, v, preferred_element_type=jnp.float32)
            m_sc[r] = m_new

    is_masked = jnp.bitwise_and(flags, STEP_MASKED) != 0
    pl.when(jnp.logical_not(is_masked))(functools.partial(step, False))
    pl.when(is_masked)(functools.partial(step, True))

    @pl.when(jnp.bitwise_and(flags, STEP_LAST) != 0)
    def _():
        for r in range(NSA_REP):
            acc = acc_sc[r]
            o_ref[0, :, r * HEAD_DIM:(r + 1) * HEAD_DIM] = acc[:, 0:HEAD_DIM] / acc[:, HEAD_DIM:AUG_DIM]


def flash_attention(mode, z3, kv3, q_gain, sel=None, tq=None, tk=None):
    b, t, _ = z3.shape
    if mode == "slc":
        tq, tk = tq or 512, tk or 512
        k_col, v_col = 0, NSA_KV_HEADS
    else:
        tq, tk = tq or 512, tk or 512
        k_col, v_col = 2 * NSA_KV_HEADS, 3 * NSA_KV_HEADS
    tq, tk = min(tq, t), min(tk, t)
    gw = NSA_REP * HEAD_DIM
    qi, kv, fl = _flash_steps(mode, t, tq, tk)

    in_specs = [pl.BlockSpec((1, tq, gw), lambda bi, g, s, sl, qi, kv, fl: (bi, qi[s], Q_OFF // gw + g)),
                pl.BlockSpec((1, tk, AUG_DIM), lambda bi, g, s, sl, qi, kv, fl: (bi, kv[s], k_col + g)),
                pl.BlockSpec((1, tk, AUG_DIM), lambda bi, g, s, sl, qi, kv, fl: (bi, kv[s], v_col + g)),
                pl.BlockSpec((1, HEAD_DIM), lambda bi, g, s, sl, qi, kv, fl: (0, 0))]
    args = [jnp.asarray(_alibi_slopes()), jnp.asarray(qi), jnp.asarray(kv), jnp.asarray(fl),
            z3, kv3, kv3, q_gain.reshape(1, HEAD_DIM)]
    if mode == "slc":
        in_specs.append(pl.BlockSpec((1, 1, tq, LANES), lambda bi, g, s, sl, qi, kv, fl: (bi, g, qi[s], 0)))
        args.append(sel)
    grid_spec = pltpu.PrefetchScalarGridSpec(
        num_scalar_prefetch=4,
        grid=(b, NSA_KV_HEADS, len(qi)),
        in_specs=in_specs,
        out_specs=pl.BlockSpec((1, tq, gw), lambda bi, g, s, sl, qi, kv, fl: (bi, qi[s], g)),
        scratch_shapes=[pltpu.VMEM((NSA_REP, tq, AUG_DIM), jnp.bfloat16),
                        pltpu.VMEM((NSA_REP, tq, HEAD_DIM), jnp.float32),
                        pltpu.VMEM((NSA_REP, tq, AUG_DIM), jnp.float32)],
    )
    return pl.pallas_call(
        functools.partial(_flash_kernel, mode=mode, tq=tq, tk=tk),
        grid_spec=grid_spec,
        out_shape=jax.ShapeDtypeStruct((b, t, NSA_WIDTH), jnp.float32),
        compiler_params=_cparams(("parallel", "parallel", "arbitrary")),
        name="flash_" + mode,
    )(*args)


def _gate_expand_matrix():
    m = np.zeros((NSA_BRANCHES, LANES, NSA_WIDTH), np.float32)
    for br in range(NSA_BRANCHES):
        for h in range(NSA_HEADS):
            m[br, h * NSA_BRANCHES + br, h * HEAD_DIM:(h + 1) * HEAD_DIM] = 1.0
    return m


def _combine_kernel(oc_ref, os_ref, ow_ref, gl_ref, ex_ref, on_ref, o_ref):
    sig = jax.nn.sigmoid(gl_ref[...])
    hi = sig.astype(jnp.bfloat16)
    lo = (sig - hi.astype(jnp.float32)).astype(jnp.bfloat16)
    out = None
    for br, ref in enumerate((oc_ref, os_ref, ow_ref)):
        ex = ex_ref[br]
        gate = (jnp.dot(hi, ex, preferred_element_type=jnp.float32)
                + jnp.dot(lo, ex, preferred_element_type=jnp.float32))
        term = gate * ref[...]
        out = term if out is None else out + term
    o_ref[...] = _rms(out, on_ref[...]).astype(o_ref.dtype)


def nsa_combine(o_cmp, o_slc, o_win, gate_logits, on_gain, tt=256):
    m, w = o_cmp.shape
    row = pl.BlockSpec((tt, w), lambda i: (i, 0))
    return pl.pallas_call(
        _combine_kernel,
        grid=(m // tt,),
        in_specs=[row, row, row,
                  pl.BlockSpec((tt, LANES), lambda i: (i, 0)),
                  pl.BlockSpec((NSA_BRANCHES, LANES, w), lambda i: (0, 0, 0)),
                  pl.BlockSpec((1, w), lambda i: (0, 0))],
        out_specs=row,
        out_shape=jax.ShapeDtypeStruct((m, w), jnp.bfloat16),
        compiler_params=_cparams(("parallel",)),
        name="nsa_combine",
    )(o_cmp, o_slc, o_win, gate_logits,
      jnp.asarray(_gate_expand_matrix(), dtype=jnp.bfloat16), on_gain.reshape(1, w))


def kernel(x, p, mix_norm, w_in, conv_w, q_gain, k_gain, cmp_pos, cmp_w1, cmp_w2, sgu_gain, w_sp,
           b_sp, mix_out_norm, w_out, mlp_norm, w_mlp_in, w_mlp_out, ple_norm, w_ple_proj,
           w_ple_gate):
    b, t, d = x.shape
    m = b * t
    bf = jnp.bfloat16
    depth = p.shape[0]
    p_rows = p.reshape(depth, m, PLE_DIM)
    w_in_t = jnp.swapaxes(w_in, 1, 2)
    h2 = x.reshape(m, d)
    for i in range(depth):
        xn = rmsnorm_cast(h2, mix_norm[i])
        z_a = matmul(xn, cast_weight_t(w_in_t, i, 0, Z_A_COLS)).reshape(b, t, Z_A_COLS)
        z_g = matmul(xn, cast_weight_t(w_in_t, i, Z_G_OFF, Z_G_COLS)).reshape(b, t, Z_G_COLS)
        w_gate = cast_weight_t(w_in_t, i, Z_A_COLS, LANES, keep=N_GATES)
        gate_logits = matmul(xn, w_gate, tn=LANES)

        on = mix_out_norm[i]
        mixed_a = conv_mixer(z_a, conv_w[i], on[:CONV_WIDTH])
        kc, vc = compress_kv(z_a, cmp_pos[i], cmp_w1[i], cmp_w2[i], k_gain[i])
        o_cmp, sel = cmp_attention(z_a, kc, vc, q_gain[i])
        kv3 = kv_prep(z_a, k_gain[i])
        o_slc = flash_attention("slc", z_a, kv3, q_gain[i], sel=sel)
        o_win = flash_attention("win", z_a, kv3, q_gain[i])
        mixed_b = nsa_combine(o_cmp.reshape(m, NSA_WIDTH), o_slc.reshape(m, NSA_WIDTH),
                              o_win.reshape(m, NSA_WIDTH), gate_logits,
                              on[CONV_WIDTH:CONV_WIDTH + NSA_WIDTH])
        mixed_c = gmlp_mixer(z_g, sgu_gain[i], w_sp[i], b_sp[i], on[CONV_WIDTH + NSA_WIDTH:])
        mixed = (mixed_a.reshape(m, CONV_WIDTH), mixed_b, mixed_c.reshape(m, GMLP_WIDTH))
        h2 = matmul(mixed, cast_weight(w_out, i), mode="residual", res=h2)

        hn = rmsnorm_cast(h2, mlp_norm[i])
        hid = matmul(hn, cast_weight(w_mlp_in, i), mode="relu2", out_dtype=bf)
        h2 = matmul(hid, cast_weight(w_mlp_out, i), mode="residual", res=h2)

        hp = rmsnorm_cast(h2, ple_norm[i])
        h2 = matmul(hp, cast_weight(w_ple_gate, i), mode="ple", res=h2,
                    p=p_rows, wp=w_ple_proj, layer=i, tn=512)
    return h2.reshape(b, t, d)
```

```python
import functools

import jax
import jax.numpy as jnp
import numpy as np
from jax import lax
from jax.experimental import pallas as pl
from jax.experimental.pallas import tpu as pltpu

D_MODEL = 4096
HEAD_DIM = 128
CONV_WIDTH = D_MODEL // 4
CONV_K = 3
NSA_HEADS = (D_MODEL // 2) // HEAD_DIM
NSA_WIDTH = NSA_HEADS * HEAD_DIM
NSA_KV_HEADS = 4
NSA_REP = NSA_HEADS // NSA_KV_HEADS
NSA_BRANCHES = 3
KV_WIDTH = NSA_KV_HEADS * HEAD_DIM
CMP_LEN = 32
CMP_STRIDE = 16
SLC_BLOCK = 64
SLC_SHIFT = 6
N_SELECT = 16
WINDOW = 512
GMLP_WIDTH = D_MODEL // 4
GMLP_GROUPS = GMLP_WIDTH // HEAD_DIM
GMLP_CHUNK = 128
D_FF = 4 * D_MODEL
PLE_DIM = 256
N_GATES = NSA_HEADS * NSA_BRANCHES
EPS = 1e-6
NEG_INF = -1e30
FORCE_SCORE = 1e9
BELOW_NEG_INF = -3e38

Z_A_COLS = 3 * CONV_WIDTH + NSA_WIDTH + 2 * NSA_BRANCHES * KV_WIDTH
Z_G_COLS = 2 * GMLP_WIDTH
Z_G_OFF = Z_A_COLS + N_GATES
Q_OFF = 3 * CONV_WIDTH
KCMP_OFF = Q_OFF + NSA_WIDTH
VCMP_OFF = KCMP_OFF + KV_WIDTH
KV4_OFF = VCMP_OFF + KV_WIDTH

LANES = 128
VMEM_LIMIT_BYTES = 60 * 1024 * 1024


def _cparams(dims):
    return pltpu.CompilerParams(dimension_semantics=dims, vmem_limit_bytes=VMEM_LIMIT_BYTES)


def _rms(x, gain):
    return x * lax.rsqrt(jnp.mean(x * x, axis=-1, keepdims=True) + EPS) * gain


def _gelu(x):
    return jax.nn.gelu(x)


def _row_scale(ssq, width, n_groups=1):
    tot = ssq[:, 0:LANES]
    for gi in range(1, n_groups):
        tot = tot + ssq[:, gi * LANES:(gi + 1) * LANES]
    return lax.rsqrt(tot * (1.0 / width) + EPS)


def _lane_sum_sq(x):
    return jnp.broadcast_to(jnp.sum(x * x, axis=-1, keepdims=True), (x.shape[0], LANES))


def _stats_kernel(x_ref, hb_ref, ssq_ref):
    x = x_ref[...]
    hb_ref[...] = x.astype(hb_ref.dtype)
    ssq_ref[...] = _lane_sum_sq(x)


def stats_cast(x, tm=512):
    m, d = x.shape
    return pl.pallas_call(
        _stats_kernel,
        grid=(m // tm,),
        in_specs=[pl.BlockSpec((tm, d), lambda i: (i, 0))],
        out_specs=[pl.BlockSpec((tm, d), lambda i: (i, 0)),
                   pl.BlockSpec((tm, LANES), lambda i: (i, 0))],
        out_shape=[jax.ShapeDtypeStruct((m, d), jnp.bfloat16),
                   jax.ShapeDtypeStruct((m, LANES), jnp.float32)],
        compiler_params=_cparams(("parallel",)),
        name="stats_cast",
    )(x)


def _cast_kernel(w_ref, g_ref, o_ref):
    o_ref[...] = (w_ref[...] * g_ref[...]).astype(o_ref.dtype)


def _cast_t_kernel(a_ref, b_ref, g_ref, o_ref, *, shift, keep):
    x = a_ref[...]
    if shift:
        x = jnp.concatenate([x[shift:, :], b_ref[:shift, :]], axis=0)
    if keep < x.shape[0]:
        row = lax.broadcasted_iota(jnp.int32, x.shape, 0)
        x = jnp.where(row < keep, x, 0.0)
    o_ref[...] = (x * g_ref[...]).T.astype(o_ref.dtype)


def cast_weight(w, layer, gain, tk=512, tn=2048):
    _, kdim, n = w.shape
    tk, tn = min(tk, kdim), min(tn, n)
    assert kdim % tk == 0 and n % tn == 0
    return pl.pallas_call(
        _cast_kernel,
        grid=(kdim // tk, n // tn),
        in_specs=[pl.BlockSpec((None, tk, tn), lambda i, j: (layer, i, j)),
                  pl.BlockSpec((tk, 1), lambda i, j: (i, 0))],
        out_specs=pl.BlockSpec((tk, tn), lambda i, j: (i, j)),
        out_shape=jax.ShapeDtypeStruct((kdim, n), jnp.bfloat16),
        compiler_params=_cparams(("parallel", "parallel")),
        name="cast_weight",
    )(w, gain.reshape(kdim, 1))


def cast_weight_t(w_t, layer, gain, row0, nrows, keep=None, tr=512, tc=2048):
    _, n, kdim = w_t.shape
    tr, tc = min(tr, nrows), min(tc, kdim)
    shift = row0 % tr
    base = row0 // tr
    assert nrows % tr == 0 and kdim % tc == 0 and shift % 8 == 0
    block = lambda off: pl.BlockSpec((None, tr, tc), lambda i, j: (layer, base + j + off, i))
    return pl.pallas_call(
        functools.partial(_cast_t_kernel, shift=shift, keep=keep or nrows),
        grid=(kdim // tc, nrows // tr),
        in_specs=[block(0), block(1 if shift else 0), pl.BlockSpec((1, tc), lambda i, j: (0, i))],
        out_specs=pl.BlockSpec((tc, tr), lambda i, j: (i, j)),
        out_shape=jax.ShapeDtypeStruct((kdim, nrows), jnp.bfloat16),
        compiler_params=_cparams(("parallel", "parallel")),
        name="cast_weight_t",
    )(w_t, w_t, gain.reshape(1, kdim))


def _mm_kernel(*refs, mode, nk, n_x, scale_groups, stats):
    x_refs, refs = refs[:n_x], refs[n_x:]
    refs = list(refs)
    take = lambda cond: refs.pop(0) if cond else None
    w_ref = refs.pop(0)
    res_ref = take(mode in ("residual", "ple"))
    p_ref, wp_ref = take(mode == "ple"), take(mode == "ple")
    ssq_refs = [take(g > 0) for g in scale_groups]
    o_ref = refs.pop(0)
    hb_ref, ssq_out_ref = take(stats), take(stats)
    tn = o_ref.shape[1]
    j = pl.program_id(1)
    k = pl.program_id(2)

    def scaled(d, n):
        if ssq_refs[n] is None:
            return d
        s = _row_scale(ssq_refs[n][...], x_refs[n].shape[1] * (nk if len(x_refs) == 1 else 1),
                       scale_groups[n])
        return d * jnp.tile(s, (1, tn // LANES))

    def finish(new):
        o_ref[...] = new
        if stats:
            hb_ref[...] = new.astype(hb_ref.dtype)
            ssq = _lane_sum_sq(new)

            @pl.when(j == 0)
            def _():
                ssq_out_ref[...] = ssq

            @pl.when(j > 0)
            def _():
                ssq_out_ref[...] += ssq

    def epilogue(acc):
        if mode == "plain":
            o_ref[...] = acc.astype(o_ref.dtype)
        elif mode == "relu2":
            r = jnp.maximum(acc, 0.0)
            o_ref[...] = (r * r).astype(o_ref.dtype)
        elif mode == "residual":
            finish(res_ref[...] + acc)
        else:
            proj = jnp.dot(p_ref[...].astype(jnp.bfloat16), wp_ref[...].astype(jnp.bfloat16),
                           preferred_element_type=jnp.float32)
            finish(res_ref[...] + jax.nn.sigmoid(acc) * proj)

    part, off = None, 0
    for n, x_ref in enumerate(x_refs):
        kx = x_ref.shape[1]
        d = scaled(jnp.dot(x_ref[...], w_ref[off:off + kx, :], preferred_element_type=jnp.float32), n)
        part = d if part is None else part + d
        off += kx
    if nk == 1:
        epilogue(part)
        return
    if mode == "residual":
        @pl.when(k == 0)
        def _():
            o_ref[...] = res_ref[...] + part

        @pl.when(jnp.logical_and(k > 0, k < nk - 1))
        def _():
            o_ref[...] += part

        @pl.when(k == nk - 1)
        def _():
            finish(o_ref[...] + part)
        return
    acc_ref, = refs

    @pl.when(k == 0)
    def _():
        acc_ref[...] = part

    @pl.when(jnp.logical_and(k > 0, k < nk - 1))
    def _():
        acc_ref[...] += part

    @pl.when(k == nk - 1)
    def _():
        epilogue(acc_ref[...] + part)


def matmul(x, w, mode="plain", out_dtype=jnp.float32, res=None, p=None, wp=None, layer=0,
           ssq=None, stats=False, tm=1024, tn=1024, tk=4096):
    xs = x if isinstance(x, (tuple, list)) else (x,)
    ssqs = ssq if isinstance(ssq, (tuple, list)) else (ssq,) * len(xs)
    m = xs[0].shape[0]
    kdim, n = w.shape
    assert sum(xi.shape[1] for xi in xs) == kdim and len(ssqs) == len(xs)
    tm, tn, tk = min(tm, m), min(tn, n), min(tk, kdim)
    nk = kdim // tk
    assert m % tm == 0 and n % tn == 0 and kdim % tk == 0 and (nk == 1 or len(xs) == 1)
    assert not stats or mode in ("residual", "ple")
    in_specs = [pl.BlockSpec((tm, xi.shape[1] if len(xs) > 1 else tk), lambda i, j, k: (i, k))
                for xi in xs]
    in_specs.append(pl.BlockSpec((tk, tn), lambda i, j, k: (k, j)))
    args = [*xs, w]
    if mode in ("residual", "ple"):
        in_specs.append(pl.BlockSpec((tm, tn), lambda i, j, k: (i, j)))
        args.append(res)
    if mode == "ple":
        pd = p.shape[2]
        in_specs += [pl.BlockSpec((None, tm, pd), lambda i, j, k: (layer, i, 0)),
                     pl.BlockSpec((None, pd, tn), lambda i, j, k: (layer, 0, j))]
        args += [p, wp]
    scale_groups = tuple(0 if s is None else s.shape[1] // LANES for s in ssqs)
    for s in ssqs:
        if s is not None:
            in_specs.append(pl.BlockSpec((tm, s.shape[1]), lambda i, j, k: (i, 0)))
            args.append(s)
    tile = pl.BlockSpec((tm, tn), lambda i, j, k: (i, j))
    out_specs, out_shape = tile, jax.ShapeDtypeStruct((m, n), out_dtype)
    if stats:
        out_specs = [tile, tile, pl.BlockSpec((tm, LANES), lambda i, j, k: (i, 0))]
        out_shape = [out_shape, jax.ShapeDtypeStruct((m, n), jnp.bfloat16),
                     jax.ShapeDtypeStruct((m, LANES), jnp.float32)]
    scratch = [pltpu.VMEM((tm, tn), jnp.float32)] if nk > 1 and mode != "residual" else []
    return pl.pallas_call(
        functools.partial(_mm_kernel, mode=mode, nk=nk, n_x=len(xs), scale_groups=scale_groups,
                          stats=stats),
        grid=(m // tm, n // tn, nk),
        in_specs=in_specs,
        out_specs=out_specs,
        out_shape=out_shape,
        scratch_shapes=scratch,
        compiler_params=_cparams(("parallel", "arbitrary" if stats else "parallel", "arbitrary")),
        name="matmul_" + mode,
    )(*args)


HALO_ROWS = 8


def _conv_kernel(xa_ref, gb_ref, gc_ref, xap_ref, gcp_ref, cw_ref, on_ref, o_ref):
    i = pl.program_id(1)
    inner = gc_ref[0] * xa_ref[0]
    prev = gcp_ref[0] * xap_ref[0]
    prev = jnp.where(i > 0, prev, 0.0)
    tt = inner.shape[0]
    row = lax.broadcasted_iota(jnp.int32, inner.shape, 0)
    m1 = jnp.where(row == 0, prev[HALO_ROWS - 1:HALO_ROWS], pltpu.roll(inner, 1, 0))
    m2 = pltpu.roll(inner, 2, 0)
    m2 = jnp.where(row == 0, prev[HALO_ROWS - 2:HALO_ROWS - 1], m2)
    m2 = jnp.where(row == 1, prev[HALO_ROWS - 1:HALO_ROWS], m2)
    cw = cw_ref[...]
    conv = cw[0:1] * m2 + cw[1:2] * m1 + cw[2:3] * inner
    o_ref[0] = _rms(gb_ref[0] * conv, on_ref[...]).astype(o_ref.dtype)


def conv_mixer(z3, conv_w, on_gain, tt=512):
    b, t, _ = z3.shape
    c = CONV_WIDTH
    hb = tt // HALO_ROWS
    cur = lambda col: pl.BlockSpec((1, tt, c), lambda bi, i: (bi, i, col))
    halo = lambda col: pl.BlockSpec((1, HALO_ROWS, c),
                                    lambda bi, i: (bi, jnp.maximum(i * hb - 1, 0), col))
    return pl.pallas_call(
        _conv_kernel,
        grid=(b, t // tt),
        in_specs=[cur(0), cur(1), cur(2), halo(0), halo(2),
                  pl.BlockSpec((HALO_ROWS, c), lambda bi, i: (0, 0)),
                  pl.BlockSpec((1, c), lambda bi, i: (0, 0))],
        out_specs=pl.BlockSpec((1, tt, c), lambda bi, i: (bi, i, 0)),
        out_shape=jax.ShapeDtypeStruct((b, t, c), jnp.bfloat16),
        compiler_params=_cparams(("parallel", "parallel")),
        name="conv_mixer",
    )(z3, z3, z3, z3, z3,
      jnp.pad(conv_w, ((0, HALO_ROWS - CONV_K), (0, 0))), on_gain.reshape(1, c))


def _gmlp_kernel(gu_ref, gv_ref, sg_ref, wsp_ref, bexp_ref, on_ref, o_ref, *, n_chunks):
    u = _gelu(gu_ref[0])
    v = _rms(_gelu(gv_ref[0]), sg_ref[...]).astype(jnp.bfloat16)
    ck = GMLP_CHUNK
    r_i = lax.broadcasted_iota(jnp.int32, (ck, ck), 0)
    c_i = lax.broadcasted_iota(jnp.int32, (ck, ck), 1)
    bexp = bexp_ref[...]
    rows = []
    for c in range(n_chunks):
        cols = []
        for g in range(GMLP_GROUPS):
            w = jnp.where(r_i >= c_i, wsp_ref[g], 0.0).astype(jnp.bfloat16)
            vg = v[c * ck:(c + 1) * ck, g * HEAD_DIM:(g + 1) * HEAD_DIM]
            cols.append(jnp.dot(w, vg, preferred_element_type=jnp.float32))
        rows.append(jnp.concatenate(cols, axis=1) + bexp)
    spatial = jnp.concatenate(rows, axis=0) if n_chunks > 1 else rows[0]
    o_ref[0] = _rms(u * spatial, on_ref[...]).astype(o_ref.dtype)


def gmlp_mixer(z3, sgu_gain, w_sp, b_sp, on_gain, tt=256):
    b, t, _ = z3.shape
    c = GMLP_WIDTH
    bexp = jnp.repeat(b_sp.T, HEAD_DIM, axis=1)
    const2 = lambda shape: pl.BlockSpec(shape, lambda bi, i: (0,) * len(shape))
    return pl.pallas_call(
        functools.partial(_gmlp_kernel, n_chunks=tt // GMLP_CHUNK),
        grid=(b, t // tt),
        in_specs=[pl.BlockSpec((1, tt, c), lambda bi, i: (bi, i, 0)),
                  pl.BlockSpec((1, tt, c), lambda bi, i: (bi, i, 1)),
                  const2((1, c)), const2((GMLP_GROUPS, GMLP_CHUNK, GMLP_CHUNK)),
                  const2((GMLP_CHUNK, c)), const2((1, c))],
        out_specs=pl.BlockSpec((1, tt, c), lambda bi, i: (bi, i, 0)),
        out_shape=jax.ShapeDtypeStruct((b, t, c), jnp.bfloat16),
        compiler_params=_cparams(("parallel", "parallel")),
        name="gmlp_mixer",
    )(z3, z3, sgu_gain.reshape(1, c), w_sp, bexp, on_gain.reshape(1, c))


AUG_DIM = 2 * HEAD_DIM
POS_LANE = 64
ONE_LANE = 70
FEAT_END = 73
MASK_BIG = 2.0 ** 100
MASKED = -MASK_BIG
M_INIT = NEG_INF
LOG2E = 1.4426950408889634


def _kvprep_kernel(z_ref, kg_ref, o_ref, *, tt):
    i = pl.program_id(1)
    z = z_ref[0]
    pos = i * tt + lax.broadcasted_iota(jnp.int32, (tt, HEAD_DIM), 0)
    lane = lax.broadcasted_iota(jnp.int32, (tt, HEAD_DIM), 1)
    blk = lax.shift_right_logical(pos, SLC_SHIFT)
    hi = lax.shift_left(blk, SLC_SHIFT).astype(jnp.float32)
    lo = jnp.bitwise_and(pos, SLC_BLOCK - 1).astype(jnp.float32)
    posf = jnp.where(jnp.bitwise_and(lane, 1) == 0, hi, lo)
    tail = jnp.where(lane < ONE_LANE, posf, jnp.where(lane < FEAT_END, 1.0, 0.0))
    onehot = jnp.where(lane == blk, MASK_BIG, 0.0)
    kfeat = (jnp.where(lane < POS_LANE, onehot, tail).astype(o_ref.dtype),
             jnp.where(lane < POS_LANE, 0.0, tail).astype(o_ref.dtype))
    vfeat = jnp.ones((tt, HEAD_DIM), o_ref.dtype)
    for part in range(4):
        for g in range(NSA_KV_HEADS):
            src = part * KV_WIDTH + g * HEAD_DIM
            dst = (part * NSA_KV_HEADS + g) * AUG_DIM
            x = z[:, src:src + HEAD_DIM]
            if part % 2 == 0:
                x = _rms(x, kg_ref[1 + part // 2:2 + part // 2, :])
            o_ref[0, :, dst:dst + HEAD_DIM] = x.astype(o_ref.dtype)
            o_ref[0, :, dst + HEAD_DIM:dst + AUG_DIM] = kfeat[part // 2] if part % 2 == 0 else vfeat


def kv_prep(z3, k_gain, tt=256):
    b, t, _ = z3.shape
    w = 4 * KV_WIDTH
    wo = 4 * NSA_KV_HEADS * AUG_DIM
    kg = jnp.pad(k_gain, ((0, 8 - NSA_BRANCHES), (0, 0)))
    return pl.pallas_call(
        functools.partial(_kvprep_kernel, tt=tt),
        grid=(b, t // tt),
        in_specs=[pl.BlockSpec((1, tt, w), lambda bi, i: (bi, i, KV4_OFF // w)),
                  pl.BlockSpec((8, HEAD_DIM), lambda bi, i: (0, 0))],
        out_specs=pl.BlockSpec((1, tt, wo), lambda bi, i: (bi, i, 0)),
        out_shape=jax.ShapeDtypeStruct((b, t, wo), jnp.bfloat16),
        compiler_params=_cparams(("parallel", "parallel")),
        name="kv_prep",
    )(z3, kg)


def _compress_kernel(k_ref, v_ref, w1_ref, w2_ref, pe_ref, kg_ref, kc_ref, vc_ref, *, n_chunk):
    def one(src_ref, which):
        ya = jnp.zeros((n_chunk, HEAD_DIM), jnp.float32)
        yb = jnp.zeros((n_chunk, HEAD_DIM), jnp.float32)
        bias = jnp.zeros((8, HEAD_DIM), jnp.float32)
        for l in range(CMP_STRIDE):
            xl = src_ref[0, pl.ds(l, n_chunk, stride=CMP_STRIDE), :].astype(jnp.bfloat16)
            ya += jnp.dot(xl, w1_ref[which, l], preferred_element_type=jnp.float32)
            yb += jnp.dot(xl, w1_ref[which, CMP_STRIDE + l], preferred_element_type=jnp.float32)
        for l in range(CMP_LEN):
            pe_l = jnp.broadcast_to(pe_ref[which, l:l + 1, :], (8, HEAD_DIM)).astype(jnp.bfloat16)
            bias += jnp.dot(pe_l, w1_ref[which, l], preferred_element_type=jnp.float32)
        hid = _gelu(ya + pltpu.roll(yb, n_chunk - 1, 0) + bias[0:1])
        return jnp.dot(hid.astype(jnp.bfloat16), w2_ref[which], preferred_element_type=jnp.float32)

    kc_ref[0, 0] = _rms(one(k_ref, 0), kg_ref[0:1, :]).astype(kc_ref.dtype)
    vc_ref[0, 0] = one(v_ref, 1).astype(vc_ref.dtype)


def compress_kv(z3, cmp_pos, cmp_w1, cmp_w2, k_gain):
    b, t, _ = z3.shape
    n_chunk = t // CMP_STRIDE
    kg = jnp.pad(k_gain, ((0, 8 - NSA_BRANCHES), (0, 0)))
    const = lambda shape: pl.BlockSpec(shape, lambda bi, g: (0,) * len(shape))
    out = jax.ShapeDtypeStruct((b, NSA_KV_HEADS, n_chunk, HEAD_DIM), jnp.bfloat16)
    return pl.pallas_call(
        functools.partial(_compress_kernel, n_chunk=n_chunk),
        grid=(b, NSA_KV_HEADS),
        in_specs=[pl.BlockSpec((1, t, HEAD_DIM), lambda bi, g: (bi, 0, KCMP_OFF // HEAD_DIM + g)),
                  pl.BlockSpec((1, t, HEAD_DIM), lambda bi, g: (bi, 0, VCMP_OFF // HEAD_DIM + g)),
                  const((2, CMP_LEN, HEAD_DIM, HEAD_DIM)), const((2, HEAD_DIM, HEAD_DIM)),
                  const((2, CMP_LEN, HEAD_DIM)), const((8, HEAD_DIM))],
        out_specs=[pl.BlockSpec((1, 1, n_chunk, HEAD_DIM), lambda bi, g: (bi, g, 0, 0))] * 2,
        out_shape=[out, out],
        compiler_params=_cparams(("parallel", "parallel")),
        name="compress_kv",
    )(z3, z3, cmp_w1.astype(jnp.bfloat16), cmp_w2.astype(jnp.bfloat16), cmp_pos, kg)


def _alibi_slopes():
    h = np.arange(1, NSA_HEADS + 1, dtype=np.float32)
    return np.power(np.float32(2.0), -8.0 * h / NSA_HEADS).astype(np.float32)


def _overlap_matrix(n_chunk, n_slc):
    n_cmp = n_chunk - CMP_LEN // CMP_STRIDE + 1
    jc = np.arange(n_cmp)[:, None] * CMP_STRIDE
    bs = np.arange(n_slc)[None, :] * SLC_BLOCK
    ov = np.clip(np.minimum(jc + CMP_LEN, bs + SLC_BLOCK) - np.maximum(jc, bs), 0, None)
    out = np.zeros((n_chunk, LANES), np.float32)
    out[:n_cmp, :n_slc] = ov.astype(np.float32) / CMP_LEN
    return out


def _q_heads(q_ref, qg_ref):
    q = q_ref[0]
    scale = HEAD_DIM ** -0.5
    return [(_rms(q[:, r * HEAD_DIM:(r + 1) * HEAD_DIM], qg_ref[...]) * scale).astype(jnp.bfloat16)
            for r in range(NSA_REP)]


def _gate_expand_matrix():
    m = np.zeros((NSA_BRANCHES, NSA_KV_HEADS, LANES, NSA_REP * HEAD_DIM), np.float32)
    for br in range(NSA_BRANCHES):
        for g in range(NSA_KV_HEADS):
            for r in range(NSA_REP):
                m[br, g, (g * NSA_REP + r) * NSA_BRANCHES + br, r * HEAD_DIM:(r + 1) * HEAD_DIM] = 1.0
    return m


def _branch_gate(gl_ref, ex_ref):
    sig = jax.nn.sigmoid(gl_ref[0])
    hi = sig.astype(jnp.bfloat16)
    lo = (sig - hi.astype(jnp.float32)).astype(jnp.bfloat16)
    ex = ex_ref[...]
    return (jnp.dot(hi, ex, preferred_element_type=jnp.float32)
            + jnp.dot(lo, ex, preferred_element_type=jnp.float32))


def _cmp_kernel(slopes_ref, q_ref, kc_ref, vc_ref, qg_ref, ov_ref, gl_ref, ex_ref, o_ref, sel_ref,
                *, tq, n_chunk, n_slc):
    g = pl.program_id(1)
    i = pl.program_id(2)
    gate = _branch_gate(gl_ref, ex_ref)
    n_cmp = n_chunk - CMP_LEN // CMP_STRIDE + 1
    qh = _q_heads(q_ref, qg_ref)
    kc = kc_ref[0, 0]
    vc = vc_ref[0, 0]
    t = i * tq + lax.broadcasted_iota(jnp.int32, (tq, n_chunk), 0)
    n = lax.broadcasted_iota(jnp.int32, (tq, n_chunk), 1)
    dist = t - (n * CMP_STRIDE + CMP_LEN - 1)
    mask = jnp.logical_and(dist >= 0, n < n_cmp)
    distf = dist.astype(jnp.float32)
    psum = jnp.zeros((tq, n_chunk), jnp.float32)
    for r in range(NSA_REP):
        s = lax.dot_general(qh[r], kc, (((1,), (1,)), ((), ())),
                            preferred_element_type=jnp.float32)
        s = jnp.where(mask, s - slopes_ref[g * NSA_REP + r] * distf, NEG_INF)
        mx = jnp.max(s, axis=-1, keepdims=True)
        e = jnp.where(mask, jnp.exp(s - mx), 0.0)
        den = jnp.sum(e, axis=-1, keepdims=True)
        p = e / jnp.where(den > 0.0, den, 1.0)
        psum += p
        o_ref[0, :, r * HEAD_DIM:(r + 1) * HEAD_DIM] = gate[:, r * HEAD_DIM:(r + 1) * HEAD_DIM] * jnp.dot(
            p.astype(jnp.bfloat16), vc, preferred_element_type=jnp.float32)

    ov = ov_ref[...]
    imp = jnp.zeros((tq, LANES), jnp.float32)
    rem = psum
    for _ in range(3):
        part = rem.astype(jnp.bfloat16)
        imp += jnp.dot(part, ov, preferred_element_type=jnp.float32)
        rem = rem - part.astype(jnp.float32)

    tb = i * tq + lax.broadcasted_iota(jnp.int32, (tq, LANES), 0)
    blk = lax.broadcasted_iota(jnp.int32, (tq, LANES), 1)
    qblk = lax.shift_right_logical(tb, SLC_SHIFT)
    forced = (blk == 0) | (blk == qblk) | (blk == qblk - 1)
    score = jnp.where(forced, FORCE_SCORE, jnp.where(blk <= qblk, imp, NEG_INF))
    score = jnp.where(blk < n_slc, score, BELOW_NEG_INF)
    st = score.T[0:n_slc]
    n_tiles = n_slc // 8
    tiles = [st[8 * j:8 * j + 8] for j in range(n_tiles)]
    sub8 = lax.broadcasted_iota(jnp.int32, (8, tq), 0)
    cnt = [jnp.zeros((8, tq), jnp.float32) for _ in range(n_tiles)]
    for m in range(n_slc):
        row = st[m:m + 1, :]
        own, off = divmod(m, 8)
        for j in range(n_tiles):
            if j < own:
                inc = jnp.where(row > tiles[j], 1.0, 0.0)
            elif j > own:
                inc = jnp.where(row >= tiles[j], 1.0, 0.0)
            else:
                later = jnp.where(sub8 > off, 1.0, 0.0)
                inc = (jnp.where(row >= tiles[j], later, 0.0)
                       + jnp.where(row > tiles[j], 1.0 - later, 0.0))
            cnt[j] = cnt[j] + inc
    keep = float(min(N_SELECT, n_slc))
    sel_t = jnp.concatenate([jnp.where(c < keep, 1.0, 0.0) for c in cnt]
                            + [jnp.zeros((LANES - n_slc, tq), jnp.float32)] * (n_slc < LANES), axis=0)
    sel_ref[0, 0] = sel_t.T.astype(sel_ref.dtype)


def cmp_attention(z3, kc, vc, q_gain, gate_logits, tq=512):
    b, t, _ = z3.shape
    tq = min(tq, t)
    n_chunk = t // CMP_STRIDE
    n_slc = t // SLC_BLOCK
    assert n_slc <= LANES
    gw = NSA_REP * HEAD_DIM
    grid_spec = pltpu.PrefetchScalarGridSpec(
        num_scalar_prefetch=1,
        grid=(b, NSA_KV_HEADS, t // tq),
        in_specs=[pl.BlockSpec((1, tq, gw), lambda bi, g, i, s: (bi, i, Q_OFF // gw + g)),
                  pl.BlockSpec((1, 1, n_chunk, HEAD_DIM), lambda bi, g, i, s: (bi, g, 0, 0)),
                  pl.BlockSpec((1, 1, n_chunk, HEAD_DIM), lambda bi, g, i, s: (bi, g, 0, 0)),
                  pl.BlockSpec((1, HEAD_DIM), lambda bi, g, i, s: (0, 0)),
                  pl.BlockSpec((n_chunk, LANES), lambda bi, g, i, s: (0, 0)),
                  pl.BlockSpec((1, tq, LANES), lambda bi, g, i, s: (bi, i, 0)),
                  pl.BlockSpec((None, None, LANES, gw), lambda bi, g, i, s: (0, g, 0, 0))],
        out_specs=[pl.BlockSpec((1, tq, gw), lambda bi, g, i, s: (bi, i, g)),
                   pl.BlockSpec((1, 1, tq, LANES), lambda bi, g, i, s: (bi, g, i, 0))],
    )
    return pl.pallas_call(
        functools.partial(_cmp_kernel, tq=tq, n_chunk=n_chunk, n_slc=n_slc),
        grid_spec=grid_spec,
        out_shape=[jax.ShapeDtypeStruct((b, t, NSA_WIDTH), jnp.float32),
                   jax.ShapeDtypeStruct((b, NSA_KV_HEADS, t, LANES), jnp.bfloat16)],
        compiler_params=_cparams(("parallel", "parallel", "parallel")),
        name="cmp_attention",
    )(jnp.asarray(_alibi_slopes()), z3, kc, vc, q_gain.reshape(1, HEAD_DIM),
      jnp.asarray(_overlap_matrix(n_chunk, n_slc), dtype=jnp.bfloat16), gate_logits,
      jnp.asarray(_gate_expand_matrix(), dtype=jnp.bfloat16))


STEP_FIRST, STEP_LAST, STEP_MASKED = 1, 2, 4


def _flash_steps(mode, t, tq, tk):
    qi, kv, fl = [], [], []
    for i in range(t // tq):
        lo, hi = i * tq, i * tq + tq - 1
        first_key = 0 if mode == "slc" else max(lo - WINDOW + 1, 0)
        tiles = list(range(first_key // tk, hi // tk + 1))
        for n, j in enumerate(tiles):
            k_lo, k_hi = j * tk, j * tk + tk - 1
            causal_ok = k_hi <= lo
            window_ok = mode == "slc" or hi - k_lo < WINDOW
            flag = (STEP_FIRST if n == 0 else 0) | (STEP_LAST if n == len(tiles) - 1 else 0)
            flag |= 0 if (causal_ok and window_ok) else STEP_MASKED
            qi.append(i), kv.append(j), fl.append(flag)
    return [np.asarray(a, np.int32) for a in (qi, kv, fl)]


def _split3(x):
    a = x.astype(jnp.bfloat16).astype(jnp.float32)
    b = (x - a).astype(jnp.bfloat16).astype(jnp.float32)
    c = (x - a - b).astype(jnp.bfloat16).astype(jnp.float32)
    return a, b, c


def _flash_kernel(slopes_ref, qi_ref, kv_ref, fl_ref, *refs, mode, tq, tk):
    if mode == "slc":
        q_ref, k_ref, v_ref, qg_ref, gl_ref, ex_ref, prev_ref, sel_ref, o_ref, q_sc, m_sc, acc_sc = refs
    else:
        q_ref, k_ref, v_ref, qg_ref, gl_ref, ex_ref, prev_ref, o_ref, ssq_ref, q_sc, m_sc, acc_sc = refs
    g = pl.program_id(1)
    s = pl.program_id(2)
    i = qi_ref[s]
    kv_idx = kv_ref[s]
    flags = fl_ref[s]

    @pl.when(jnp.bitwise_and(flags, STEP_FIRST) != 0)
    def _():
        q = q_ref[0]
        lane = lax.broadcasted_iota(jnp.int32, (tq, HEAD_DIM), 1)
        if mode == "slc":
            sel_m1 = sel_ref[0, 0].astype(jnp.float32) - 1.0
        else:
            sel_m1 = jnp.zeros((tq, HEAD_DIM), jnp.float32)
        t0 = (i * tq).astype(jnp.float32)
        for r in range(NSA_REP):
            qn = _rms(q[:, r * HEAD_DIM:(r + 1) * HEAD_DIM], qg_ref[...]) * (HEAD_DIM ** -0.5 * LOG2E)
            slope2 = slopes_ref[g * NSA_REP + r] * LOG2E
            a, b, c = _split3(jnp.where(lane < ONE_LANE, slope2, -slope2 * t0))
            k3 = jnp.where(lane < ONE_LANE, lax.shift_right_logical(lane - POS_LANE, 1), lane - ONE_LANE)
            term = jnp.where(k3 == 0, a, jnp.where(k3 == 1, b, c))
            feat = jnp.where(lane < POS_LANE, sel_m1, jnp.where(lane < FEAT_END, term, 0.0))
            q_sc[r, :, 0:HEAD_DIM] = qn.astype(q_sc.dtype)
            q_sc[r, :, HEAD_DIM:AUG_DIM] = feat.astype(q_sc.dtype)
        m_sc[...] = jnp.full(m_sc.shape, M_INIT, jnp.float32)
        acc_sc[...] = jnp.zeros(acc_sc.shape, jnp.float32)

    def step(masked):
        k = k_ref[0]
        v = v_ref[0]
        if masked:
            t = i * tq + lax.broadcasted_iota(jnp.int32, (tq, tk), 0)
            dist = t - (kv_idx * tk + lax.broadcasted_iota(jnp.int32, (tq, tk), 1))
            ok = dist >= 0 if mode == "slc" else jnp.logical_and(dist >= 0, dist < WINDOW)
        for r in range(NSA_REP):
            s2 = lax.dot_general(q_sc[r], k, (((1,), (1,)), ((), ())),
                                 preferred_element_type=jnp.float32)
            if masked:
                s2 = jnp.where(ok, s2, MASKED)
            m_prev = m_sc[r]
            m_new = jnp.maximum(m_prev, jnp.max(s2, axis=-1, keepdims=True))
            p = jnp.exp2(s2 - jnp.tile(m_new, (1, tk // HEAD_DIM))).astype(jnp.bfloat16)
            alpha = jnp.exp2(m_prev - m_new)
            acc_sc[r] = jnp.tile(alpha, (1, AUG_DIM // HEAD_DIM)) * acc_sc[r] + jnp.dot(
                p, v, preferred_element_type=jnp.float32)
            m_sc[r] = m_new

    is_masked = jnp.bitwise_and(flags, STEP_MASKED) != 0
    pl.when(jnp.logical_not(is_masked))(functools.partial(step, False))
    pl.when(is_masked)(functools.partial(step, True))

    @pl.when(jnp.bitwise_and(flags, STEP_LAST) != 0)
    def _():
        gate = _branch_gate(gl_ref, ex_ref)
        prev = prev_ref[0]
        ssq = jnp.zeros((tq, LANES), jnp.float32)
        for r in range(NSA_REP):
            acc = acc_sc[r]
            cols = slice(r * HEAD_DIM, (r + 1) * HEAD_DIM)
            tot = prev[:, cols] + gate[:, cols] * (acc[:, 0:HEAD_DIM] / acc[:, HEAD_DIM:AUG_DIM])
            o_ref[0, :, cols] = tot.astype(o_ref.dtype)
            ssq = ssq + _lane_sum_sq(tot)
        if mode == "win":
            ssq_ref[0] = ssq


def flash_attention(mode, z3, kv3, q_gain, gate_logits, prev, sel=None, tq=None, tk=None):
    b, t, _ = z3.shape
    if mode == "slc":
        tq, tk = tq or 512, tk or 512
        k_col, v_col, branch = 0, NSA_KV_HEADS, 1
    else:
        tq, tk = tq or 512, tk or 512
        k_col, v_col, branch = 2 * NSA_KV_HEADS, 3 * NSA_KV_HEADS, 2
    tq, tk = min(tq, t), min(tk, t)
    gw = NSA_REP * HEAD_DIM
    qi, kv, fl = _flash_steps(mode, t, tq, tk)

    q_tile = lambda col: (lambda bi, g, s, sl, qi, kv, fl: (bi, qi[s], col + g))
    in_specs = [pl.BlockSpec((1, tq, gw), q_tile(Q_OFF // gw)),
                pl.BlockSpec((1, tk, AUG_DIM), lambda bi, g, s, sl, qi, kv, fl: (bi, kv[s], k_col + g)),
                pl.BlockSpec((1, tk, AUG_DIM), lambda bi, g, s, sl, qi, kv, fl: (bi, kv[s], v_col + g)),
                pl.BlockSpec((1, HEAD_DIM), lambda bi, g, s, sl, qi, kv, fl: (0, 0)),
                pl.BlockSpec((1, tq, LANES), lambda bi, g, s, sl, qi, kv, fl: (bi, qi[s], 0)),
                pl.BlockSpec((None, None, LANES, gw), lambda bi, g, s, sl, qi, kv, fl: (branch, g, 0, 0)),
                pl.BlockSpec((1, tq, gw), q_tile(0))]
    args = [jnp.asarray(_alibi_slopes()), jnp.asarray(qi), jnp.asarray(kv), jnp.asarray(fl),
            z3, kv3, kv3, q_gain.reshape(1, HEAD_DIM), gate_logits,
            jnp.asarray(_gate_expand_matrix(), dtype=jnp.bfloat16), prev]
    out_specs = pl.BlockSpec((1, tq, gw), q_tile(0))
    if mode == "slc":
        in_specs.append(pl.BlockSpec((1, 1, tq, LANES), lambda bi, g, s, sl, qi, kv, fl: (bi, g, qi[s], 0)))
        args.append(sel)
        out_shape = jax.ShapeDtypeStruct((b, t, NSA_WIDTH), jnp.float32)
    else:
        out_specs = [out_specs, pl.BlockSpec((1, tq, LANES), q_tile(0))]
        out_shape = [jax.ShapeDtypeStruct((b, t, NSA_WIDTH), jnp.bfloat16),
                     jax.ShapeDtypeStruct((b, t, NSA_KV_HEADS * LANES), jnp.float32)]
    grid_spec = pltpu.PrefetchScalarGridSpec(
        num_scalar_prefetch=4,
        grid=(b, NSA_KV_HEADS, len(qi)),
        in_specs=in_specs,
        out_specs=out_specs,
        scratch_shapes=[pltpu.VMEM((NSA_REP, tq, AUG_DIM), jnp.bfloat16),
                        pltpu.VMEM((NSA_REP, tq, HEAD_DIM), jnp.float32),
                        pltpu.VMEM((NSA_REP, tq, AUG_DIM), jnp.float32)],
    )
    return pl.pallas_call(
        functools.partial(_flash_kernel, mode=mode, tq=tq, tk=tk),
        grid_spec=grid_spec,
        out_shape=out_shape,
        compiler_params=_cparams(("parallel", "parallel", "arbitrary")),
        name="flash_" + mode,
    )(*args)


def kernel(x, p, mix_norm, w_in, conv_w, q_gain, k_gain, cmp_pos, cmp_w1, cmp_w2, sgu_gain, w_sp,
           b_sp, mix_out_norm, w_out, mlp_norm, w_mlp_in, w_mlp_out, ple_norm, w_ple_proj,
           w_ple_gate):
    b, t, d = x.shape
    m = b * t
    bf = jnp.bfloat16
    depth = p.shape[0]
    p_rows = p.reshape(depth, m, PLE_DIM)
    w_in_t = jnp.swapaxes(w_in, 1, 2)
    ones = jnp.ones((D_FF,), jnp.float32)
    h2 = x.reshape(m, d)
    hb, ssq = stats_cast(h2)
    for i in range(depth):
        g_in = mix_norm[i]
        z_a = matmul(hb, cast_weight_t(w_in_t, i, g_in, 0, Z_A_COLS), ssq=ssq).reshape(b, t, Z_A_COLS)
        z_g = matmul(hb, cast_weight_t(w_in_t, i, g_in, Z_G_OFF, Z_G_COLS), ssq=ssq).reshape(b, t, Z_G_COLS)
        w_gate = cast_weight_t(w_in_t, i, g_in, Z_A_COLS, LANES, keep=N_GATES)
        gate_logits = matmul(hb, w_gate, ssq=ssq, tn=LANES).reshape(b, t, LANES)

        on = mix_out_norm[i]
        mixed_a = conv_mixer(z_a, conv_w[i], on[:CONV_WIDTH])
        kc, vc = compress_kv(z_a, cmp_pos[i], cmp_w1[i], cmp_w2[i], k_gain[i])
        o_b, sel = cmp_attention(z_a, kc, vc, q_gain[i], gate_logits)
        kv3 = kv_prep(z_a, k_gain[i])
        o_b = flash_attention("slc", z_a, kv3, q_gain[i], gate_logits, o_b, sel=sel)
        o_b, ssq_b = flash_attention("win", z_a, kv3, q_gain[i], gate_logits, o_b)
        mixed_c = gmlp_mixer(z_g, sgu_gain[i], w_sp[i], b_sp[i], on[CONV_WIDTH + NSA_WIDTH:])
        g_out = jnp.concatenate([ones[:CONV_WIDTH], on[CONV_WIDTH:CONV_WIDTH + NSA_WIDTH],
                                 ones[:GMLP_WIDTH]])
        mixed = (mixed_a.reshape(m, CONV_WIDTH), o_b.reshape(m, NSA_WIDTH),
                 mixed_c.reshape(m, GMLP_WIDTH))
        h2, hb, ssq = matmul(mixed, cast_weight(w_out, i, g_out), mode="residual", res=h2,
                             ssq=(None, ssq_b.reshape(m, NSA_KV_HEADS * LANES), None),
                             stats=True, tn=512)

        hid = matmul(hb, cast_weight(w_mlp_in, i, mlp_norm[i]), mode="relu2", out_dtype=bf, ssq=ssq)
        h2, hb, ssq = matmul(hid, cast_weight(w_mlp_out, i, ones), mode="residual", res=h2,
                             stats=True, tn=512)

        out = matmul(hb, cast_weight(w_ple_gate, i, ple_norm[i]), mode="ple", res=h2, p=p_rows,
                     wp=w_ple_proj, layer=i, ssq=ssq, stats=i + 1 < depth, tn=512)
        h2, hb, ssq = out if i + 1 < depth else (out, None, None)
    return h2.reshape(b, t, d)
```

```python
import functools

import jax
import jax.numpy as jnp
import numpy as np
from jax import lax
from jax.experimental import pallas as pl
from jax.experimental.pallas import tpu as pltpu

D_MODEL = 4096
HEAD_DIM = 128
CONV_WIDTH = D_MODEL // 4
CONV_K = 3
NSA_HEADS = (D_MODEL // 2) // HEAD_DIM
NSA_WIDTH = NSA_HEADS * HEAD_DIM
NSA_KV_HEADS = 4
NSA_REP = NSA_HEADS // NSA_KV_HEADS
NSA_BRANCHES = 3
KV_WIDTH = NSA_KV_HEADS * HEAD_DIM
CMP_LEN = 32
CMP_STRIDE = 16
SLC_BLOCK = 64
SLC_SHIFT = 6
N_SELECT = 16
WINDOW = 512
GMLP_WIDTH = D_MODEL // 4
GMLP_GROUPS = GMLP_WIDTH // HEAD_DIM
GMLP_CHUNK = 128
D_FF = 4 * D_MODEL
PLE_DIM = 256
N_GATES = NSA_HEADS * NSA_BRANCHES
EPS = 1e-6
NEG_INF = -1e30
FORCE_SCORE = 1e9
BELOW_NEG_INF = -3e38

Z_A_COLS = 3 * CONV_WIDTH + NSA_WIDTH + 2 * NSA_BRANCHES * KV_WIDTH
Z_G_COLS = 2 * GMLP_WIDTH
Z_G_OFF = Z_A_COLS + N_GATES
Q_OFF = 3 * CONV_WIDTH
KCMP_OFF = Q_OFF + NSA_WIDTH
VCMP_OFF = KCMP_OFF + KV_WIDTH
KV4_OFF = VCMP_OFF + KV_WIDTH

LANES = 128
VMEM_LIMIT_BYTES = 60 * 1024 * 1024


def _cparams(dims):
    return pltpu.CompilerParams(dimension_semantics=dims, vmem_limit_bytes=VMEM_LIMIT_BYTES)


def _rms(x, gain):
    return x * lax.rsqrt(jnp.mean(x * x, axis=-1, keepdims=True) + EPS) * gain


def _gelu(x):
    return jax.nn.gelu(x)


def _row_scale(ssq, width, n_groups=1):
    tot = ssq[:, 0:LANES]
    for gi in range(1, n_groups):
        tot = tot + ssq[:, gi * LANES:(gi + 1) * LANES]
    return lax.rsqrt(tot * (1.0 / width) + EPS)


def _lane_sum_sq(x):
    return jnp.broadcast_to(jnp.sum(x * x, axis=-1, keepdims=True), (x.shape[0], LANES))


def _stats_kernel(x_ref, hb_ref, ssq_ref):
    x = x_ref[...]
    hb_ref[...] = x.astype(hb_ref.dtype)
    ssq_ref[...] = _lane_sum_sq(x)


def stats_cast(x, tm=512):
    m, d = x.shape
    return pl.pallas_call(
        _stats_kernel,
        grid=(m // tm,),
        in_specs=[pl.BlockSpec((tm, d), lambda i: (i, 0))],
        out_specs=[pl.BlockSpec((tm, d), lambda i: (i, 0)),
                   pl.BlockSpec((tm, LANES), lambda i: (i, 0))],
        out_shape=[jax.ShapeDtypeStruct((m, d), jnp.bfloat16),
                   jax.ShapeDtypeStruct((m, LANES), jnp.float32)],
        compiler_params=_cparams(("parallel",)),
        name="stats_cast",
    )(x)


def _cast_kernel(w_ref, g_ref, o_ref):
    o_ref[...] = (w_ref[...] * g_ref[...]).astype(o_ref.dtype)


def _cast_t_kernel(a_ref, b_ref, g_ref, o_ref, *, shift, keep):
    x = a_ref[...]
    if shift:
        x = jnp.concatenate([x[shift:, :], b_ref[:shift, :]], axis=0)
    if keep < x.shape[0]:
        row = lax.broadcasted_iota(jnp.int32, x.shape, 0)
        x = jnp.where(row < keep, x, 0.0)
    o_ref[...] = (x * g_ref[...]).T.astype(o_ref.dtype)


def cast_weight(w, layer, gain, tk=512, tn=2048):
    _, kdim, n = w.shape
    tk, tn = min(tk, kdim), min(tn, n)
    assert kdim % tk == 0 and n % tn == 0
    return pl.pallas_call(
        _cast_kernel,
        grid=(kdim // tk, n // tn),
        in_specs=[pl.BlockSpec((None, tk, tn), lambda i, j: (layer, i, j)),
                  pl.BlockSpec((tk, 1), lambda i, j: (i, 0))],
        out_specs=pl.BlockSpec((tk, tn), lambda i, j: (i, j)),
        out_shape=jax.ShapeDtypeStruct((kdim, n), jnp.bfloat16),
        compiler_params=_cparams(("parallel", "parallel")),
        name="cast_weight",
    )(w, gain.reshape(kdim, 1))


def cast_weight_t(w_t, layer, gain, row0, nrows, keep=None, tr=512, tc=2048):
    _, n, kdim = w_t.shape
    tr, tc = min(tr, nrows), min(tc, kdim)
    shift = row0 % tr
    base = row0 // tr
    assert nrows % tr == 0 and kdim % tc == 0 and shift % 8 == 0
    block = lambda off: pl.BlockSpec((None, tr, tc), lambda i, j: (layer, base + j + off, i))
    return pl.pallas_call(
        functools.partial(_cast_t_kernel, shift=shift, keep=keep or nrows),
        grid=(kdim // tc, nrows // tr),
        in_specs=[block(0), block(1 if shift else 0), pl.BlockSpec((1, tc), lambda i, j: (0, i))],
        out_specs=pl.BlockSpec((tc, tr), lambda i, j: (i, j)),
        out_shape=jax.ShapeDtypeStruct((kdim, nrows), jnp.bfloat16),
        compiler_params=_cparams(("parallel", "parallel")),
        name="cast_weight_t",
    )(w_t, w_t, gain.reshape(1, kdim))


def _mm_kernel(*refs, mode, nk, n_x, scale_groups, stats):
    x_refs, refs = refs[:n_x], refs[n_x:]
    refs = list(refs)
    take = lambda cond: refs.pop(0) if cond else None
    w_ref = refs.pop(0)
    res_ref = take(mode in ("residual", "ple"))
    p_ref, wp_ref = take(mode == "ple"), take(mode == "ple")
    ssq_refs = [take(g > 0) for g in scale_groups]
    o_ref = refs.pop(0)
    hb_ref, ssq_out_ref = take(stats), take(stats)
    tn = o_ref.shape[1]
    j = pl.program_id(1)
    k = pl.program_id(2)

    def scaled(d, n):
        if ssq_refs[n] is None:
            return d
        s = _row_scale(ssq_refs[n][...], x_refs[n].shape[1] * (nk if len(x_refs) == 1 else 1),
                       scale_groups[n])
        return d * jnp.tile(s, (1, tn // LANES))

    def finish(new):
        o_ref[...] = new
        if stats:
            hb_ref[...] = new.astype(hb_ref.dtype)
            ssq = _lane_sum_sq(new)

            @pl.when(j == 0)
            def _():
                ssq_out_ref[...] = ssq

            @pl.when(j > 0)
            def _():
                ssq_out_ref[...] += ssq

    def epilogue(acc):
        if mode == "plain":
            o_ref[...] = acc.astype(o_ref.dtype)
        elif mode == "relu2":
            r = jnp.maximum(acc, 0.0)
            o_ref[...] = (r * r).astype(o_ref.dtype)
        elif mode == "residual":
            finish(res_ref[...] + acc)
        else:
            proj = jnp.dot(p_ref[...].astype(jnp.bfloat16), wp_ref[...].astype(jnp.bfloat16),
                           preferred_element_type=jnp.float32)
            finish(res_ref[...] + jax.nn.sigmoid(acc) * proj)

    def product():
        part, off = None, 0
        for n, x_ref in enumerate(x_refs):
            kx = x_ref.shape[1]
            d = scaled(jnp.dot(x_ref[...], w_ref[off:off + kx, :], preferred_element_type=jnp.float32), n)
            part = d if part is None else part + d
            off += kx
        return part

    if nk == 1:
        epilogue(product())
        return
    if mode == "residual":
        @pl.when(k == 0)
        def _():
            o_ref[...] = res_ref[...] + product()

        @pl.when(jnp.logical_and(k > 0, k < nk - 1))
        def _():
            o_ref[...] += product()

        @pl.when(k == nk - 1)
        def _():
            finish(o_ref[...] + product())
        return
    acc_ref, = refs

    @pl.when(k == 0)
    def _():
        acc_ref[...] = product()

    @pl.when(jnp.logical_and(k > 0, k < nk - 1))
    def _():
        acc_ref[...] += product()

    @pl.when(k == nk - 1)
    def _():
        epilogue(acc_ref[...] + product())


def matmul(x, w, mode="plain", out_dtype=jnp.float32, res=None, p=None, wp=None, layer=0,
           ssq=None, stats=False, tm=1024, tn=1024, tk=4096):
    xs = x if isinstance(x, (tuple, list)) else (x,)
    ssqs = ssq if isinstance(ssq, (tuple, list)) else (ssq,) * len(xs)
    m = xs[0].shape[0]
    kdim, n = w.shape
    assert sum(xi.shape[1] for xi in xs) == kdim and len(ssqs) == len(xs)
    tm, tn, tk = min(tm, m), min(tn, n), min(tk, kdim)
    nk = kdim // tk
    assert m % tm == 0 and n % tn == 0 and kdim % tk == 0 and (nk == 1 or len(xs) == 1)
    assert not stats or mode in ("residual", "ple")
    in_specs = [pl.BlockSpec((tm, xi.shape[1] if len(xs) > 1 else tk), lambda i, j, k: (i, k))
                for xi in xs]
    in_specs.append(pl.BlockSpec((tk, tn), lambda i, j, k: (k, j)))
    args = [*xs, w]
    if mode in ("residual", "ple"):
        in_specs.append(pl.BlockSpec((tm, tn), lambda i, j, k: (i, j)))
        args.append(res)
    if mode == "ple":
        pd = p.shape[2]
        in_specs += [pl.BlockSpec((None, tm, pd), lambda i, j, k: (layer, i, 0)),
                     pl.BlockSpec((None, pd, tn), lambda i, j, k: (layer, 0, j))]
        args += [p, wp]
    scale_groups = tuple(0 if s is None else s.shape[1] // LANES for s in ssqs)
    for s in ssqs:
        if s is not None:
            in_specs.append(pl.BlockSpec((tm, s.shape[1]), lambda i, j, k: (i, 0)))
            args.append(s)
    tile = pl.BlockSpec((tm, tn), lambda i, j, k: (i, j))
    out_specs, out_shape = tile, jax.ShapeDtypeStruct((m, n), out_dtype)
    if stats:
        out_specs = [tile, tile, pl.BlockSpec((tm, LANES), lambda i, j, k: (i, 0))]
        out_shape = [out_shape, jax.ShapeDtypeStruct((m, n), jnp.bfloat16),
                     jax.ShapeDtypeStruct((m, LANES), jnp.float32)]
    scratch = [pltpu.VMEM((tm, tn), jnp.float32)] if nk > 1 and mode != "residual" else []
    return pl.pallas_call(
        functools.partial(_mm_kernel, mode=mode, nk=nk, n_x=len(xs), scale_groups=scale_groups,
                          stats=stats),
        grid=(m // tm, n // tn, nk),
        in_specs=in_specs,
        out_specs=out_specs,
        out_shape=out_shape,
        scratch_shapes=scratch,
        compiler_params=_cparams(("parallel", "arbitrary" if stats else "parallel", "arbitrary")),
        name="matmul_" + mode,
    )(*args)


HALO_ROWS = 8


def _conv_kernel(xa_ref, gb_ref, gc_ref, xap_ref, gcp_ref, cw_ref, on_ref, o_ref):
    i = pl.program_id(1)
    inner = gc_ref[0] * xa_ref[0]
    prev = gcp_ref[0] * xap_ref[0]
    prev = jnp.where(i > 0, prev, 0.0)
    tt = inner.shape[0]
    row = lax.broadcasted_iota(jnp.int32, inner.shape, 0)
    m1 = jnp.where(row == 0, prev[HALO_ROWS - 1:HALO_ROWS], pltpu.roll(inner, 1, 0))
    m2 = pltpu.roll(inner, 2, 0)
    m2 = jnp.where(row == 0, prev[HALO_ROWS - 2:HALO_ROWS - 1], m2)
    m2 = jnp.where(row == 1, prev[HALO_ROWS - 1:HALO_ROWS], m2)
    cw = cw_ref[...]
    conv = cw[0:1] * m2 + cw[1:2] * m1 + cw[2:3] * inner
    o_ref[0] = _rms(gb_ref[0] * conv, on_ref[...]).astype(o_ref.dtype)


def conv_mixer(z3, conv_w, on_gain, tt=512):
    b, t, _ = z3.shape
    c = CONV_WIDTH
    hb = tt // HALO_ROWS
    cur = lambda col: pl.BlockSpec((1, tt, c), lambda bi, i: (bi, i, col))
    halo = lambda col: pl.BlockSpec((1, HALO_ROWS, c),
                                    lambda bi, i: (bi, jnp.maximum(i * hb - 1, 0), col))
    return pl.pallas_call(
        _conv_kernel,
        grid=(b, t // tt),
        in_specs=[cur(0), cur(1), cur(2), halo(0), halo(2),
                  pl.BlockSpec((HALO_ROWS, c), lambda bi, i: (0, 0)),
                  pl.BlockSpec((1, c), lambda bi, i: (0, 0))],
        out_specs=pl.BlockSpec((1, tt, c), lambda bi, i: (bi, i, 0)),
        out_shape=jax.ShapeDtypeStruct((b, t, c), jnp.bfloat16),
        compiler_params=_cparams(("parallel", "parallel")),
        name="conv_mixer",
    )(z3, z3, z3, z3, z3,
      jnp.pad(conv_w, ((0, HALO_ROWS - CONV_K), (0, 0))), on_gain.reshape(1, c))


def _gmlp_kernel(gu_ref, gv_ref, sg_ref, wsp_ref, bexp_ref, on_ref, o_ref, *, n_chunks):
    u = _gelu(gu_ref[0])
    v = _rms(_gelu(gv_ref[0]), sg_ref[...]).astype(jnp.bfloat16)
    ck = GMLP_CHUNK
    r_i = lax.broadcasted_iota(jnp.int32, (ck, ck), 0)
    c_i = lax.broadcasted_iota(jnp.int32, (ck, ck), 1)
    bexp = bexp_ref[...]
    rows = []
    for c in range(n_chunks):
        cols = []
        for g in range(GMLP_GROUPS):
            w = jnp.where(r_i >= c_i, wsp_ref[g], 0.0).astype(jnp.bfloat16)
            vg = v[c * ck:(c + 1) * ck, g * HEAD_DIM:(g + 1) * HEAD_DIM]
            cols.append(jnp.dot(w, vg, preferred_element_type=jnp.float32))
        rows.append(jnp.concatenate(cols, axis=1) + bexp)
    spatial = jnp.concatenate(rows, axis=0) if n_chunks > 1 else rows[0]
    o_ref[0] = _rms(u * spatial, on_ref[...]).astype(o_ref.dtype)


def gmlp_mixer(z3, sgu_gain, w_sp, b_sp, on_gain, tt=256):
    b, t, _ = z3.shape
    c = GMLP_WIDTH
    bexp = jnp.repeat(b_sp.T, HEAD_DIM, axis=1)
    const2 = lambda shape: pl.BlockSpec(shape, lambda bi, i: (0,) * len(shape))
    return pl.pallas_call(
        functools.partial(_gmlp_kernel, n_chunks=tt // GMLP_CHUNK),
        grid=(b, t // tt),
        in_specs=[pl.BlockSpec((1, tt, c), lambda bi, i: (bi, i, 0)),
                  pl.BlockSpec((1, tt, c), lambda bi, i: (bi, i, 1)),
                  const2((1, c)), const2((GMLP_GROUPS, GMLP_CHUNK, GMLP_CHUNK)),
                  const2((GMLP_CHUNK, c)), const2((1, c))],
        out_specs=pl.BlockSpec((1, tt, c), lambda bi, i: (bi, i, 0)),
        out_shape=jax.ShapeDtypeStruct((b, t, c), jnp.bfloat16),
        compiler_params=_cparams(("parallel", "parallel")),
        name="gmlp_mixer",
    )(z3, z3, sgu_gain.reshape(1, c), w_sp, bexp, on_gain.reshape(1, c))


AUG_DIM = 2 * HEAD_DIM
POS_LANE = 64
ONE_LANE = 70
FEAT_END = 73
MASK_BIG = 2.0 ** 100
MASKED = -MASK_BIG
M_INIT = NEG_INF
LOG2E = 1.4426950408889634


def _kvprep_kernel(z_ref, kg_ref, o_ref, *, tt):
    i = pl.program_id(1)
    z = z_ref[0]
    pos = i * tt + lax.broadcasted_iota(jnp.int32, (tt, HEAD_DIM), 0)
    lane = lax.broadcasted_iota(jnp.int32, (tt, HEAD_DIM), 1)
    blk = lax.shift_right_logical(pos, SLC_SHIFT)
    hi = lax.shift_left(blk, SLC_SHIFT).astype(jnp.float32)
    lo = jnp.bitwise_and(pos, SLC_BLOCK - 1).astype(jnp.float32)
    posf = jnp.where(jnp.bitwise_and(lane, 1) == 0, hi, lo)
    tail = jnp.where(lane < ONE_LANE, posf, jnp.where(lane < FEAT_END, 1.0, 0.0))
    onehot = jnp.where(lane == blk, MASK_BIG, 0.0)
    kfeat = (jnp.where(lane < POS_LANE, onehot, tail).astype(o_ref.dtype),
             jnp.where(lane < POS_LANE, 0.0, tail).astype(o_ref.dtype))
    vfeat = jnp.ones((tt, HEAD_DIM), o_ref.dtype)
    for part in range(4):
        for g in range(NSA_KV_HEADS):
            src = part * KV_WIDTH + g * HEAD_DIM
            dst = (part * NSA_KV_HEADS + g) * AUG_DIM
            x = z[:, src:src + HEAD_DIM]
            if part % 2 == 0:
                x = _rms(x, kg_ref[1 + part // 2:2 + part // 2, :])
            o_ref[0, :, dst:dst + HEAD_DIM] = x.astype(o_ref.dtype)
            o_ref[0, :, dst + HEAD_DIM:dst + AUG_DIM] = kfeat[part // 2] if part % 2 == 0 else vfeat


def kv_prep(z3, k_gain, tt=256):
    b, t, _ = z3.shape
    w = 4 * KV_WIDTH
    wo = 4 * NSA_KV_HEADS * AUG_DIM
    kg = jnp.pad(k_gain, ((0, 8 - NSA_BRANCHES), (0, 0)))
    return pl.pallas_call(
        functools.partial(_kvprep_kernel, tt=tt),
        grid=(b, t // tt),
        in_specs=[pl.BlockSpec((1, tt, w), lambda bi, i: (bi, i, KV4_OFF // w)),
                  pl.BlockSpec((8, HEAD_DIM), lambda bi, i: (0, 0))],
        out_specs=pl.BlockSpec((1, tt, wo), lambda bi, i: (bi, i, 0)),
        out_shape=jax.ShapeDtypeStruct((b, t, wo), jnp.bfloat16),
        compiler_params=_cparams(("parallel", "parallel")),
        name="kv_prep",
    )(z3, kg)


def _compress_kernel(k_ref, v_ref, w1_ref, w2_ref, pe_ref, kg_ref, kc_ref, vc_ref, *, n_chunk):
    def one(src_ref, which):
        ya = jnp.zeros((n_chunk, HEAD_DIM), jnp.float32)
        yb = jnp.zeros((n_chunk, HEAD_DIM), jnp.float32)
        bias = jnp.zeros((8, HEAD_DIM), jnp.float32)
        for l in range(CMP_STRIDE):
            xl = src_ref[0, pl.ds(l, n_chunk, stride=CMP_STRIDE), :].astype(jnp.bfloat16)
            ya += jnp.dot(xl, w1_ref[which, l], preferred_element_type=jnp.float32)
            yb += jnp.dot(xl, w1_ref[which, CMP_STRIDE + l], preferred_element_type=jnp.float32)
        for l in range(CMP_LEN):
            pe_l = jnp.broadcast_to(pe_ref[which, l:l + 1, :], (8, HEAD_DIM)).astype(jnp.bfloat16)
            bias += jnp.dot(pe_l, w1_ref[which, l], preferred_element_type=jnp.float32)
        hid = _gelu(ya + pltpu.roll(yb, n_chunk - 1, 0) + bias[0:1])
        return jnp.dot(hid.astype(jnp.bfloat16), w2_ref[which], preferred_element_type=jnp.float32)

    kc_ref[0, 0] = _rms(one(k_ref, 0), kg_ref[0:1, :]).astype(kc_ref.dtype)
    vc_ref[0, 0] = one(v_ref, 1).astype(vc_ref.dtype)


def compress_kv(z3, cmp_pos, cmp_w1, cmp_w2, k_gain):
    b, t, _ = z3.shape
    n_chunk = t // CMP_STRIDE
    kg = jnp.pad(k_gain, ((0, 8 - NSA_BRANCHES), (0, 0)))
    const = lambda shape: pl.BlockSpec(shape, lambda bi, g: (0,) * len(shape))
    out = jax.ShapeDtypeStruct((b, NSA_KV_HEADS, n_chunk, HEAD_DIM), jnp.bfloat16)
    return pl.pallas_call(
        functools.partial(_compress_kernel, n_chunk=n_chunk),
        grid=(b, NSA_KV_HEADS),
        in_specs=[pl.BlockSpec((1, t, HEAD_DIM), lambda bi, g: (bi, 0, KCMP_OFF // HEAD_DIM + g)),
                  pl.BlockSpec((1, t, HEAD_DIM), lambda bi, g: (bi, 0, VCMP_OFF // HEAD_DIM + g)),
                  const((2, CMP_LEN, HEAD_DIM, HEAD_DIM)), const((2, HEAD_DIM, HEAD_DIM)),
                  const((2, CMP_LEN, HEAD_DIM)), const((8, HEAD_DIM))],
        out_specs=[pl.BlockSpec((1, 1, n_chunk, HEAD_DIM), lambda bi, g: (bi, g, 0, 0))] * 2,
        out_shape=[out, out],
        compiler_params=_cparams(("parallel", "parallel")),
        name="compress_kv",
    )(z3, z3, cmp_w1.astype(jnp.bfloat16), cmp_w2.astype(jnp.bfloat16), cmp_pos, kg)


def _alibi_slopes():
    h = np.arange(1, NSA_HEADS + 1, dtype=np.float32)
    return np.power(np.float32(2.0), -8.0 * h / NSA_HEADS).astype(np.float32)


def _overlap_matrix(n_chunk, n_slc):
    n_cmp = n_chunk - CMP_LEN // CMP_STRIDE + 1
    jc = np.arange(n_cmp)[:, None] * CMP_STRIDE
    bs = np.arange(n_slc)[None, :] * SLC_BLOCK
    ov = np.clip(np.minimum(jc + CMP_LEN, bs + SLC_BLOCK) - np.maximum(jc, bs), 0, None)
    out = np.zeros((n_chunk, LANES), np.float32)
    out[:n_cmp, :n_slc] = ov.astype(np.float32) / CMP_LEN
    return out


def _q_heads(q_ref, qg_ref):
    q = q_ref[0]
    scale = HEAD_DIM ** -0.5
    return [(_rms(q[:, r * HEAD_DIM:(r + 1) * HEAD_DIM], qg_ref[...]) * scale).astype(jnp.bfloat16)
            for r in range(NSA_REP)]


def _gate_expand_matrix():
    m = np.zeros((NSA_BRANCHES, NSA_KV_HEADS, LANES, NSA_REP * HEAD_DIM), np.float32)
    for br in range(NSA_BRANCHES):
        for g in range(NSA_KV_HEADS):
            for r in range(NSA_REP):
                m[br, g, (g * NSA_REP + r) * NSA_BRANCHES + br, r * HEAD_DIM:(r + 1) * HEAD_DIM] = 1.0
    return m


def _branch_gate(gl_ref, ex_ref):
    sig = jax.nn.sigmoid(gl_ref[0])
    hi = sig.astype(jnp.bfloat16)
    lo = (sig - hi.astype(jnp.float32)).astype(jnp.bfloat16)
    ex = ex_ref[...]
    return (jnp.dot(hi, ex, preferred_element_type=jnp.float32)
            + jnp.dot(lo, ex, preferred_element_type=jnp.float32))


def _cmp_kernel(slopes_ref, q_ref, kc_ref, vc_ref, qg_ref, ov_ref, gl_ref, ex_ref, o_ref, sel_ref,
                *, tq, n_chunk, n_slc):
    g = pl.program_id(1)
    i = pl.program_id(2)
    gate = _branch_gate(gl_ref, ex_ref)
    n_cmp = n_chunk - CMP_LEN // CMP_STRIDE + 1
    qh = _q_heads(q_ref, qg_ref)
    kc = kc_ref[0, 0]
    vc = vc_ref[0, 0]
    t = i * tq + lax.broadcasted_iota(jnp.int32, (tq, n_chunk), 0)
    n = lax.broadcasted_iota(jnp.int32, (tq, n_chunk), 1)
    dist = t - (n * CMP_STRIDE + CMP_LEN - 1)
    mask = jnp.logical_and(dist >= 0, n < n_cmp)
    distf = dist.astype(jnp.float32)
    psum = jnp.zeros((tq, n_chunk), jnp.float32)
    for r in range(NSA_REP):
        s = lax.dot_general(qh[r], kc, (((1,), (1,)), ((), ())),
                            preferred_element_type=jnp.float32)
        s = jnp.where(mask, s - slopes_ref[g * NSA_REP + r] * distf, NEG_INF)
        mx = jnp.max(s, axis=-1, keepdims=True)
        e = jnp.where(mask, jnp.exp(s - mx), 0.0)
        den = jnp.sum(e, axis=-1, keepdims=True)
        p = e / jnp.where(den > 0.0, den, 1.0)
        psum += p
        o_ref[0, :, r * HEAD_DIM:(r + 1) * HEAD_DIM] = gate[:, r * HEAD_DIM:(r + 1) * HEAD_DIM] * jnp.dot(
            p.astype(jnp.bfloat16), vc, preferred_element_type=jnp.float32)

    ov = ov_ref[...]
    imp = jnp.zeros((tq, LANES), jnp.float32)
    rem = psum
    for _ in range(3):
        part = rem.astype(jnp.bfloat16)
        imp += jnp.dot(part, ov, preferred_element_type=jnp.float32)
        rem = rem - part.astype(jnp.float32)

    tb = i * tq + lax.broadcasted_iota(jnp.int32, (tq, LANES), 0)
    blk = lax.broadcasted_iota(jnp.int32, (tq, LANES), 1)
    qblk = lax.shift_right_logical(tb, SLC_SHIFT)
    forced = (blk == 0) | (blk == qblk) | (blk == qblk - 1)
    score = jnp.where(forced, FORCE_SCORE, jnp.where(blk <= qblk, imp, NEG_INF))
    score = jnp.where(blk < n_slc, score, BELOW_NEG_INF)
    st = score.T[0:n_slc]
    n_tiles = n_slc // 8
    tiles = [st[8 * j:8 * j + 8] for j in range(n_tiles)]
    sub8 = lax.broadcasted_iota(jnp.int32, (8, tq), 0)
    cnt = [jnp.zeros((8, tq), jnp.float32) for _ in range(n_tiles)]
    for m in range(n_slc):
        row = st[m:m + 1, :]
        own, off = divmod(m, 8)
        for j in range(n_tiles):
            if j < own:
                inc = jnp.where(row > tiles[j], 1.0, 0.0)
            elif j > own:
                inc = jnp.where(row >= tiles[j], 1.0, 0.0)
            else:
                later = jnp.where(sub8 > off, 1.0, 0.0)
                inc = (jnp.where(row >= tiles[j], later, 0.0)
                       + jnp.where(row > tiles[j], 1.0 - later, 0.0))
            cnt[j] = cnt[j] + inc
    keep = float(min(N_SELECT, n_slc))
    sel_t = jnp.concatenate([jnp.where(c < keep, 1.0, 0.0) for c in cnt]
                            + [jnp.zeros((LANES - n_slc, tq), jnp.float32)] * (n_slc < LANES), axis=0)
    sel_ref[0, 0] = sel_t.T.astype(sel_ref.dtype)


def cmp_attention(z3, kc, vc, q_gain, gate_logits, tq=512):
    b, t, _ = z3.shape
    tq = min(tq, t)
    n_chunk = t // CMP_STRIDE
    n_slc = t // SLC_BLOCK
    assert n_slc <= LANES
    gw = NSA_REP * HEAD_DIM
    grid_spec = pltpu.PrefetchScalarGridSpec(
        num_scalar_prefetch=1,
        grid=(b, NSA_KV_HEADS, t // tq),
        in_specs=[pl.BlockSpec((1, tq, gw), lambda bi, g, i, s: (bi, i, Q_OFF // gw + g)),
                  pl.BlockSpec((1, 1, n_chunk, HEAD_DIM), lambda bi, g, i, s: (bi, g, 0, 0)),
                  pl.BlockSpec((1, 1, n_chunk, HEAD_DIM), lambda bi, g, i, s: (bi, g, 0, 0)),
                  pl.BlockSpec((1, HEAD_DIM), lambda bi, g, i, s: (0, 0)),
                  pl.BlockSpec((n_chunk, LANES), lambda bi, g, i, s: (0, 0)),
                  pl.BlockSpec((1, tq, LANES), lambda bi, g, i, s: (bi, i, 0)),
                  pl.BlockSpec((None, None, LANES, gw), lambda bi, g, i, s: (0, g, 0, 0))],
        out_specs=[pl.BlockSpec((1, tq, gw), lambda bi, g, i, s: (bi, i, g)),
                   pl.BlockSpec((1, 1, tq, LANES), lambda bi, g, i, s: (bi, g, i, 0))],
    )
    return pl.pallas_call(
        functools.partial(_cmp_kernel, tq=tq, n_chunk=n_chunk, n_slc=n_slc),
        grid_spec=grid_spec,
        out_shape=[jax.ShapeDtypeStruct((b, t, NSA_WIDTH), jnp.float32),
                   jax.ShapeDtypeStruct((b, NSA_KV_HEADS, t, LANES), jnp.bfloat16)],
        compiler_params=_cparams(("parallel", "parallel", "parallel")),
        name="cmp_attention",
    )(jnp.asarray(_alibi_slopes()), z3, kc, vc, q_gain.reshape(1, HEAD_DIM),
      jnp.asarray(_overlap_matrix(n_chunk, n_slc), dtype=jnp.bfloat16), gate_logits,
      jnp.asarray(_gate_expand_matrix(), dtype=jnp.bfloat16))


STEP_FIRST, STEP_LAST, STEP_MASKED = 1, 2, 4


def _flash_steps(mode, t, tq, tk):
    qi, kv, fl = [], [], []
    for i in range(t // tq):
        lo, hi = i * tq, i * tq + tq - 1
        first_key = 0 if mode == "slc" else max(lo - WINDOW + 1, 0)
        tiles = list(range(first_key // tk, hi // tk + 1))
        for n, j in enumerate(tiles):
            k_lo, k_hi = j * tk, j * tk + tk - 1
            causal_ok = k_hi <= lo
            window_ok = mode == "slc" or hi - k_lo < WINDOW
            flag = (STEP_FIRST if n == 0 else 0) | (STEP_LAST if n == len(tiles) - 1 else 0)
            flag |= 0 if (causal_ok and window_ok) else STEP_MASKED
            qi.append(i), kv.append(j), fl.append(flag)
    return [np.asarray(a, np.int32) for a in (qi, kv, fl)]


def _split3(x):
    a = x.astype(jnp.bfloat16).astype(jnp.float32)
    b = (x - a).astype(jnp.bfloat16).astype(jnp.float32)
    c = (x - a - b).astype(jnp.bfloat16).astype(jnp.float32)
    return a, b, c


def _flash_kernel(slopes_ref, qi_ref, kv_ref, fl_ref, *refs, mode, tq, tk):
    if mode == "slc":
        q_ref, k_ref, v_ref, qg_ref, gl_ref, ex_ref, prev_ref, sel_ref, o_ref, q_sc, m_sc, acc_sc = refs
    else:
        q_ref, k_ref, v_ref, qg_ref, gl_ref, ex_ref, prev_ref, o_ref, ssq_ref, q_sc, m_sc, acc_sc = refs
    g = pl.program_id(1)
    s = pl.program_id(2)
    i = qi_ref[s]
    kv_idx = kv_ref[s]
    flags = fl_ref[s]

    @pl.when(jnp.bitwise_and(flags, STEP_FIRST) != 0)
    def _():
        q = q_ref[0]
        lane = lax.broadcasted_iota(jnp.int32, (tq, HEAD_DIM), 1)
        if mode == "slc":
            sel_m1 = sel_ref[0, 0].astype(jnp.float32) - 1.0
        else:
            sel_m1 = jnp.zeros((tq, HEAD_DIM), jnp.float32)
        t0 = (i * tq).astype(jnp.float32)
        for r in range(NSA_REP):
            qn = _rms(q[:, r * HEAD_DIM:(r + 1) * HEAD_DIM], qg_ref[...]) * (HEAD_DIM ** -0.5 * LOG2E)
            slope2 = slopes_ref[g * NSA_REP + r] * LOG2E
            a, b, c = _split3(jnp.where(lane < ONE_LANE, slope2, -slope2 * t0))
            k3 = jnp.where(lane < ONE_LANE, lax.shift_right_logical(lane - POS_LANE, 1), lane - ONE_LANE)
            term = jnp.where(k3 == 0, a, jnp.where(k3 == 1, b, c))
            feat = jnp.where(lane < POS_LANE, sel_m1, jnp.where(lane < FEAT_END, term, 0.0))
            q_sc[r, :, 0:HEAD_DIM] = qn.astype(q_sc.dtype)
            q_sc[r, :, HEAD_DIM:AUG_DIM] = feat.astype(q_sc.dtype)
        m_sc[...] = jnp.full(m_sc.shape, M_INIT, jnp.float32)
        acc_sc[...] = jnp.zeros(acc_sc.shape, jnp.float32)

    def step(masked):
        k = k_ref[0]
        v = v_ref[0]
        if masked:
            t = i * tq + lax.broadcasted_iota(jnp.int32, (tq, tk), 0)
            dist = t - (kv_idx * tk + lax.broadcasted_iota(jnp.int32, (tq, tk), 1))
            ok = dist >= 0 if mode == "slc" else jnp.logical_and(dist >= 0, dist < WINDOW)
        for r in range(NSA_REP):
            s2 = lax.dot_general(q_sc[r], k, (((1,), (1,)), ((), ())),
                                 preferred_element_type=jnp.float32)
            if masked:
                s2 = jnp.where(ok, s2, MASKED)
            m_prev = m_sc[r]
            m_new = jnp.maximum(m_prev, jnp.max(s2, axis=-1, keepdims=True))
            p = jnp.exp2(s2 - jnp.tile(m_new, (1, tk // HEAD_DIM))).astype(jnp.bfloat16)
            alpha = jnp.exp2(m_prev - m_new)
            acc_sc[r] = jnp.tile(alpha, (1, AUG_DIM // HEAD_DIM)) * acc_sc[r] + jnp.dot(
                p, v, preferred_element_type=jnp.float32)
            m_sc[r] = m_new

    is_masked = jnp.bitwise_and(flags, STEP_MASKED) != 0
    pl.when(jnp.logical_not(is_masked))(functools.partial(step, False))
    pl.when(is_masked)(functools.partial(step, True))

    @pl.when(jnp.bitwise_and(flags, STEP_LAST) != 0)
    def _():
        gate = _branch_gate(gl_ref, ex_ref)
        prev = prev_ref[0]
        ssq = jnp.zeros((tq, LANES), jnp.float32)
        for r in range(NSA_REP):
            acc = acc_sc[r]
            cols = slice(r * HEAD_DIM, (r + 1) * HEAD_DIM)
            tot = prev[:, cols] + gate[:, cols] * (acc[:, 0:HEAD_DIM] / acc[:, HEAD_DIM:AUG_DIM])
            o_ref[0, :, cols] = tot.astype(o_ref.dtype)
            ssq = ssq + _lane_sum_sq(tot)
        if mode == "win":
            ssq_ref[0] = ssq


def flash_attention(mode, z3, kv3, q_gain, gate_logits, prev, sel=None, tq=None, tk=None):
    b, t, _ = z3.shape
    if mode == "slc":
        tq, tk = tq or 512, tk or 512
        k_col, v_col, branch = 0, NSA_KV_HEADS, 1
    else:
        tq, tk = tq or 512, tk or 512
        k_col, v_col, branch = 2 * NSA_KV_HEADS, 3 * NSA_KV_HEADS, 2
    tq, tk = min(tq, t), min(tk, t)
    gw = NSA_REP * HEAD_DIM
    qi, kv, fl = _flash_steps(mode, t, tq, tk)

    q_tile = lambda col: (lambda bi, g, s, sl, qi, kv, fl: (bi, qi[s], col + g))
    in_specs = [pl.BlockSpec((1, tq, gw), q_tile(Q_OFF // gw)),
                pl.BlockSpec((1, tk, AUG_DIM), lambda bi, g, s, sl, qi, kv, fl: (bi, kv[s], k_col + g)),
                pl.BlockSpec((1, tk, AUG_DIM), lambda bi, g, s, sl, qi, kv, fl: (bi, kv[s], v_col + g)),
                pl.BlockSpec((1, HEAD_DIM), lambda bi, g, s, sl, qi, kv, fl: (0, 0)),
                pl.BlockSpec((1, tq, LANES), lambda bi, g, s, sl, qi, kv, fl: (bi, qi[s], 0)),
                pl.BlockSpec((None, None, LANES, gw), lambda bi, g, s, sl, qi, kv, fl: (branch, g, 0, 0)),
                pl.BlockSpec((1, tq, gw), q_tile(0))]
    args = [jnp.asarray(_alibi_slopes()), jnp.asarray(qi), jnp.asarray(kv), jnp.asarray(fl),
            z3, kv3, kv3, q_gain.reshape(1, HEAD_DIM), gate_logits,
            jnp.asarray(_gate_expand_matrix(), dtype=jnp.bfloat16), prev]
    out_specs = pl.BlockSpec((1, tq, gw), q_tile(0))
    if mode == "slc":
        in_specs.append(pl.BlockSpec((1, 1, tq, LANES), lambda bi, g, s, sl, qi, kv, fl: (bi, g, qi[s], 0)))
        args.append(sel)
        out_shape = jax.ShapeDtypeStruct((b, t, NSA_WIDTH), jnp.float32)
    else:
        out_specs = [out_specs, pl.BlockSpec((1, tq, LANES), q_tile(0))]
        out_shape = [jax.ShapeDtypeStruct((b, t, NSA_WIDTH), jnp.bfloat16),
                     jax.ShapeDtypeStruct((b, t, NSA_KV_HEADS * LANES), jnp.float32)]
    grid_spec = pltpu.PrefetchScalarGridSpec(
        num_scalar_prefetch=4,
        grid=(b, NSA_KV_HEADS, len(qi)),
        in_specs=in_specs,
        out_specs=out_specs,
        scratch_shapes=[pltpu.VMEM((NSA_REP, tq, AUG_DIM), jnp.bfloat16),
                        pltpu.VMEM((NSA_REP, tq, HEAD_DIM), jnp.float32),
                        pltpu.VMEM((NSA_REP, tq, AUG_DIM), jnp.float32)],
    )
    return pl.pallas_call(
        functools.partial(_flash_kernel, mode=mode, tq=tq, tk=tk),
        grid_spec=grid_spec,
        out_shape=out_shape,
        compiler_params=_cparams(("parallel", "parallel", "arbitrary")),
        name="flash_" + mode,
    )(*args)


def kernel(x, p, mix_norm, w_in, conv_w, q_gain, k_gain, cmp_pos, cmp_w1, cmp_w2, sgu_gain, w_sp,
           b_sp, mix_out_norm, w_out, mlp_norm, w_mlp_in, w_mlp_out, ple_norm, w_ple_proj,
           w_ple_gate):
    b, t, d = x.shape
    m = b * t
    bf = jnp.bfloat16
    depth = p.shape[0]
    p_rows = p.reshape(depth, m, PLE_DIM)
    w_in_t = jnp.swapaxes(w_in, 1, 2)
    ones = jnp.ones((D_FF,), jnp.float32)
    h2 = x.reshape(m, d)
    hb, ssq = stats_cast(h2)
    for i in range(depth):
        g_in = mix_norm[i]
        z_a = matmul(hb, cast_weight_t(w_in_t, i, g_in, 0, Z_A_COLS), ssq=ssq).reshape(b, t, Z_A_COLS)
        z_g = matmul(hb, cast_weight_t(w_in_t, i, g_in, Z_G_OFF, Z_G_COLS), ssq=ssq).reshape(b, t, Z_G_COLS)
        w_gate = cast_weight_t(w_in_t, i, g_in, Z_A_COLS, LANES, keep=N_GATES)
        gate_logits = matmul(hb, w_gate, ssq=ssq, tn=LANES).reshape(b, t, LANES)

        on = mix_out_norm[i]
        mixed_a = conv_mixer(z_a, conv_w[i], on[:CONV_WIDTH])
        kc, vc = compress_kv(z_a, cmp_pos[i], cmp_w1[i], cmp_w2[i], k_gain[i])
        o_b, sel = cmp_attention(z_a, kc, vc, q_gain[i], gate_logits)
        kv3 = kv_prep(z_a, k_gain[i])
        o_b = flash_attention("slc", z_a, kv3, q_gain[i], gate_logits, o_b, sel=sel)
        o_b, ssq_b = flash_attention("win", z_a, kv3, q_gain[i], gate_logits, o_b)
        mixed_c = gmlp_mixer(z_g, sgu_gain[i], w_sp[i], b_sp[i], on[CONV_WIDTH + NSA_WIDTH:])
        g_out = jnp.concatenate([ones[:CONV_WIDTH], on[CONV_WIDTH:CONV_WIDTH + NSA_WIDTH],
                                 ones[:GMLP_WIDTH]])
        mixed = (mixed_a.reshape(m, CONV_WIDTH), o_b.reshape(m, NSA_WIDTH),
                 mixed_c.reshape(m, GMLP_WIDTH))
        h2, hb, ssq = matmul(mixed, cast_weight(w_out, i, g_out), mode="residual", res=h2,
                             ssq=(None, ssq_b.reshape(m, NSA_KV_HEADS * LANES), None),
                             stats=True, tn=512)

        hid = matmul(hb, cast_weight(w_mlp_in, i, mlp_norm[i]), mode="relu2", out_dtype=bf, ssq=ssq)
        h2 = matmul(hid, cast_weight(w_mlp_out, i, ones), mode="residual", res=h2)
        hb, ssq = stats_cast(h2)

        out = matmul(hb, cast_weight(w_ple_gate, i, ple_norm[i]), mode="ple", res=h2, p=p_rows,
                     wp=w_ple_proj, layer=i, ssq=ssq, stats=i + 1 < depth, tn=512)
        h2, hb, ssq = out if i + 1 < depth else (out, None, None)
    return h2.reshape(b, t, d)
```

```python
import functools

import jax
import jax.numpy as jnp
import numpy as np
from jax import lax
from jax.experimental import pallas as pl
from jax.experimental.pallas import tpu as pltpu

D_MODEL = 4096
HEAD_DIM = 128
CONV_WIDTH = D_MODEL // 4
CONV_K = 3
NSA_HEADS = (D_MODEL // 2) // HEAD_DIM
NSA_WIDTH = NSA_HEADS * HEAD_DIM
NSA_KV_HEADS = 4
NSA_REP = NSA_HEADS // NSA_KV_HEADS
NSA_BRANCHES = 3
KV_WIDTH = NSA_KV_HEADS * HEAD_DIM
CMP_LEN = 32
CMP_STRIDE = 16
SLC_BLOCK = 64
SLC_SHIFT = 6
N_SELECT = 16
WINDOW = 512
GMLP_WIDTH = D_MODEL // 4
GMLP_GROUPS = GMLP_WIDTH // HEAD_DIM
GMLP_CHUNK = 128
D_FF = 4 * D_MODEL
PLE_DIM = 256
N_GATES = NSA_HEADS * NSA_BRANCHES
EPS = 1e-6
NEG_INF = -1e30
FORCE_SCORE = 1e9
BELOW_NEG_INF = -3e38

Z_A_COLS = 3 * CONV_WIDTH + NSA_WIDTH + 2 * NSA_BRANCHES * KV_WIDTH
Z_G_COLS = 2 * GMLP_WIDTH
Z_G_OFF = Z_A_COLS + N_GATES
Q_OFF = 3 * CONV_WIDTH
KCMP_OFF = Q_OFF + NSA_WIDTH
VCMP_OFF = KCMP_OFF + KV_WIDTH
KV4_OFF = VCMP_OFF + KV_WIDTH

LANES = 128
VMEM_LIMIT_BYTES = 60 * 1024 * 1024


def _cparams(dims):
    return pltpu.CompilerParams(dimension_semantics=dims, vmem_limit_bytes=VMEM_LIMIT_BYTES)


def _rms(x, gain):
    return x * lax.rsqrt(jnp.mean(x * x, axis=-1, keepdims=True) + EPS) * gain


def _gelu(x):
    return jax.nn.gelu(x)


def _row_scale(ssq, width, n_groups=1):
    tot = ssq[:, 0:LANES]
    for gi in range(1, n_groups):
        tot = tot + ssq[:, gi * LANES:(gi + 1) * LANES]
    return lax.rsqrt(tot * (1.0 / width) + EPS)


def _lane_sum_sq(x):
    return jnp.broadcast_to(jnp.sum(x * x, axis=-1, keepdims=True), (x.shape[0], LANES))


def _stats_kernel(x_ref, hb_ref, ssq_ref):
    x = x_ref[...]
    hb_ref[...] = x.astype(hb_ref.dtype)
    ssq_ref[...] = _lane_sum_sq(x)


def stats_cast(x, tm=512):
    m, d = x.shape
    return pl.pallas_call(
        _stats_kernel,
        grid=(m // tm,),
        in_specs=[pl.BlockSpec((tm, d), lambda i: (i, 0))],
        out_specs=[pl.BlockSpec((tm, d), lambda i: (i, 0)),
                   pl.BlockSpec((tm, LANES), lambda i: (i, 0))],
        out_shape=[jax.ShapeDtypeStruct((m, d), jnp.bfloat16),
                   jax.ShapeDtypeStruct((m, LANES), jnp.float32)],
        compiler_params=_cparams(("parallel",)),
        name="stats_cast",
    )(x)


def _cast_kernel(w_ref, g_ref, o_ref):
    o_ref[...] = (w_ref[...] * g_ref[...]).astype(o_ref.dtype)


def _cast_t_kernel(a_ref, b_ref, g_ref, o_ref, *, shift, keep):
    x = a_ref[...]
    if shift:
        x = jnp.concatenate([x[shift:, :], b_ref[:shift, :]], axis=0)
    if keep < x.shape[0]:
        row = lax.broadcasted_iota(jnp.int32, x.shape, 0)
        x = jnp.where(row < keep, x, 0.0)
    o_ref[...] = (x * g_ref[...]).T.astype(o_ref.dtype)


def cast_weight(w, layer, gain, tk=1024, tn=2048):
    _, kdim, n = w.shape
    tk, tn = min(tk, kdim), min(tn, n)
    assert kdim % tk == 0 and n % tn == 0
    return pl.pallas_call(
        _cast_kernel,
        grid=(kdim // tk, n // tn),
        in_specs=[pl.BlockSpec((None, tk, tn), lambda i, j: (layer, i, j)),
                  pl.BlockSpec((tk, 1), lambda i, j: (i, 0))],
        out_specs=pl.BlockSpec((tk, tn), lambda i, j: (i, j)),
        out_shape=jax.ShapeDtypeStruct((kdim, n), jnp.bfloat16),
        compiler_params=_cparams(("parallel", "parallel")),
        name="cast_weight",
    )(w, gain.reshape(kdim, 1))


def cast_weight_t(w_t, layer, gain, row0, nrows, keep=None, tr=512, tc=2048):
    _, n, kdim = w_t.shape
    tr, tc = min(tr, nrows), min(tc, kdim)
    shift = row0 % tr
    base = row0 // tr
    assert nrows % tr == 0 and kdim % tc == 0 and shift % 8 == 0
    block = lambda off: pl.BlockSpec((None, tr, tc), lambda i, j: (layer, base + j + off, i))
    return pl.pallas_call(
        functools.partial(_cast_t_kernel, shift=shift, keep=keep or nrows),
        grid=(kdim // tc, nrows // tr),
        in_specs=[block(0), block(1 if shift else 0), pl.BlockSpec((1, tc), lambda i, j: (0, i))],
        out_specs=pl.BlockSpec((tc, tr), lambda i, j: (i, j)),
        out_shape=jax.ShapeDtypeStruct((kdim, nrows), jnp.bfloat16),
        compiler_params=_cparams(("parallel", "parallel")),
        name="cast_weight_t",
    )(w_t, w_t, gain.reshape(1, kdim))


def _mm_kernel(*refs, mode, nk, n_x, scale_groups, stats):
    x_refs, refs = refs[:n_x], refs[n_x:]
    refs = list(refs)
    take = lambda cond: refs.pop(0) if cond else None
    w_ref = refs.pop(0)
    res_ref = take(mode in ("residual", "ple"))
    p_ref, wp_ref = take(mode == "ple"), take(mode == "ple")
    ssq_refs = [take(g > 0) for g in scale_groups]
    o_ref = refs.pop(0)
    hb_ref, ssq_out_ref = take(stats), take(stats)
    tn = o_ref.shape[1]
    j = pl.program_id(1)
    k = pl.program_id(2)

    def scaled(d, n):
        if ssq_refs[n] is None:
            return d
        s = _row_scale(ssq_refs[n][...], x_refs[n].shape[1] * (nk if len(x_refs) == 1 else 1),
                       scale_groups[n])
        return d * jnp.tile(s, (1, tn // LANES))

    def finish(new):
        o_ref[...] = new
        if stats:
            hb_ref[...] = new.astype(hb_ref.dtype)
            ssq = _lane_sum_sq(new)

            @pl.when(j == 0)
            def _():
                ssq_out_ref[...] = ssq

            @pl.when(j > 0)
            def _():
                ssq_out_ref[...] += ssq

    def epilogue(acc):
        if mode == "plain":
            o_ref[...] = acc.astype(o_ref.dtype)
        elif mode == "relu2":
            r = jnp.maximum(acc, 0.0)
            o_ref[...] = (r * r).astype(o_ref.dtype)
        elif mode == "residual":
            finish(res_ref[...] + acc)
        else:
            proj = jnp.dot(p_ref[...].astype(jnp.bfloat16), wp_ref[...].astype(jnp.bfloat16),
                           preferred_element_type=jnp.float32)
            finish(res_ref[...] + jax.nn.sigmoid(acc) * proj)

    def product():
        part, off = None, 0
        for n, x_ref in enumerate(x_refs):
            kx = x_ref.shape[1]
            d = scaled(jnp.dot(x_ref[...], w_ref[off:off + kx, :], preferred_element_type=jnp.float32), n)
            part = d if part is None else part + d
            off += kx
        return part

    if nk == 1:
        epilogue(product())
        return
    if mode == "residual":
        @pl.when(k == 0)
        def _():
            o_ref[...] = res_ref[...] + product()

        @pl.when(jnp.logical_and(k > 0, k < nk - 1))
        def _():
            o_ref[...] += product()

        @pl.when(k == nk - 1)
        def _():
            finish(o_ref[...] + product())
        return
    acc_ref, = refs

    @pl.when(k == 0)
    def _():
        acc_ref[...] = product()

    @pl.when(jnp.logical_and(k > 0, k < nk - 1))
    def _():
        acc_ref[...] += product()

    @pl.when(k == nk - 1)
    def _():
        epilogue(acc_ref[...] + product())


def matmul(x, w, mode="plain", out_dtype=jnp.float32, res=None, p=None, wp=None, layer=0,
           ssq=None, stats=False, tm=1024, tn=1024, tk=4096):
    xs = x if isinstance(x, (tuple, list)) else (x,)
    ssqs = ssq if isinstance(ssq, (tuple, list)) else (ssq,) * len(xs)
    m = xs[0].shape[0]
    kdim, n = w.shape
    assert sum(xi.shape[1] for xi in xs) == kdim and len(ssqs) == len(xs)
    tm, tn, tk = min(tm, m), min(tn, n), min(tk, kdim)
    nk = kdim // tk
    assert m % tm == 0 and n % tn == 0 and kdim % tk == 0 and (nk == 1 or len(xs) == 1)
    assert not stats or mode in ("residual", "ple")
    in_specs = [pl.BlockSpec((tm, xi.shape[1] if len(xs) > 1 else tk), lambda i, j, k: (i, k))
                for xi in xs]
    in_specs.append(pl.BlockSpec((tk, tn), lambda i, j, k: (k, j)))
    args = [*xs, w]
    if mode in ("residual", "ple"):
        in_specs.append(pl.BlockSpec((tm, tn), lambda i, j, k: (i, j)))
        args.append(res)
    if mode == "ple":
        pd = p.shape[2]
        in_specs += [pl.BlockSpec((None, tm, pd), lambda i, j, k: (layer, i, 0)),
                     pl.BlockSpec((None, pd, tn), lambda i, j, k: (layer, 0, j))]
        args += [p, wp]
    scale_groups = tuple(0 if s is None else s.shape[1] // LANES for s in ssqs)
    for s in ssqs:
        if s is not None:
            in_specs.append(pl.BlockSpec((tm, s.shape[1]), lambda i, j, k: (i, 0)))
            args.append(s)
    tile = pl.BlockSpec((tm, tn), lambda i, j, k: (i, j))
    out_specs, out_shape = tile, jax.ShapeDtypeStruct((m, n), out_dtype)
    if stats:
        out_specs = [tile, tile, pl.BlockSpec((tm, LANES), lambda i, j, k: (i, 0))]
        out_shape = [out_shape, jax.ShapeDtypeStruct((m, n), jnp.bfloat16),
                     jax.ShapeDtypeStruct((m, LANES), jnp.float32)]
    scratch = [pltpu.VMEM((tm, tn), jnp.float32)] if nk > 1 and mode != "residual" else []
    return pl.pallas_call(
        functools.partial(_mm_kernel, mode=mode, nk=nk, n_x=len(xs), scale_groups=scale_groups,
                          stats=stats),
        grid=(m // tm, n // tn, nk),
        in_specs=in_specs,
        out_specs=out_specs,
        out_shape=out_shape,
        scratch_shapes=scratch,
        compiler_params=_cparams(("parallel", "arbitrary" if stats else "parallel", "arbitrary")),
        name="matmul_" + mode,
    )(*args)


HALO_ROWS = 8


def _conv_kernel(xa_ref, gb_ref, gc_ref, xap_ref, gcp_ref, cw_ref, on_ref, o_ref):
    i = pl.program_id(1)
    inner = gc_ref[0] * xa_ref[0]
    prev = gcp_ref[0] * xap_ref[0]
    prev = jnp.where(i > 0, prev, 0.0)
    tt = inner.shape[0]
    row = lax.broadcasted_iota(jnp.int32, inner.shape, 0)
    m1 = jnp.where(row == 0, prev[HALO_ROWS - 1:HALO_ROWS], pltpu.roll(inner, 1, 0))
    m2 = pltpu.roll(inner, 2, 0)
    m2 = jnp.where(row == 0, prev[HALO_ROWS - 2:HALO_ROWS - 1], m2)
    m2 = jnp.where(row == 1, prev[HALO_ROWS - 1:HALO_ROWS], m2)
    cw = cw_ref[...]
    conv = cw[0:1] * m2 + cw[1:2] * m1 + cw[2:3] * inner
    o_ref[0] = _rms(gb_ref[0] * conv, on_ref[...]).astype(o_ref.dtype)


def conv_mixer(z3, conv_w, on_gain, tt=512):
    b, t, _ = z3.shape
    c = CONV_WIDTH
    hb = tt // HALO_ROWS
    cur = lambda col: pl.BlockSpec((1, tt, c), lambda bi, i: (bi, i, col))
    halo = lambda col: pl.BlockSpec((1, HALO_ROWS, c),
                                    lambda bi, i: (bi, jnp.maximum(i * hb - 1, 0), col))
    return pl.pallas_call(
        _conv_kernel,
        grid=(b, t // tt),
        in_specs=[cur(0), cur(1), cur(2), halo(0), halo(2),
                  pl.BlockSpec((HALO_ROWS, c), lambda bi, i: (0, 0)),
                  pl.BlockSpec((1, c), lambda bi, i: (0, 0))],
        out_specs=pl.BlockSpec((1, tt, c), lambda bi, i: (bi, i, 0)),
        out_shape=jax.ShapeDtypeStruct((b, t, c), jnp.bfloat16),
        compiler_params=_cparams(("parallel", "parallel")),
        name="conv_mixer",
    )(z3, z3, z3, z3, z3,
      jnp.pad(conv_w, ((0, HALO_ROWS - CONV_K), (0, 0))), on_gain.reshape(1, c))


def _gmlp_kernel(gu_ref, gv_ref, sg_ref, wsp_ref, bexp_ref, on_ref, o_ref, *, n_chunks):
    u = _gelu(gu_ref[0])
    v = _rms(_gelu(gv_ref[0]), sg_ref[...]).astype(jnp.bfloat16)
    ck = GMLP_CHUNK
    r_i = lax.broadcasted_iota(jnp.int32, (ck, ck), 0)
    c_i = lax.broadcasted_iota(jnp.int32, (ck, ck), 1)
    bexp = bexp_ref[...]
    rows = []
    for c in range(n_chunks):
        cols = []
        for g in range(GMLP_GROUPS):
            w = jnp.where(r_i >= c_i, wsp_ref[g], 0.0).astype(jnp.bfloat16)
            vg = v[c * ck:(c + 1) * ck, g * HEAD_DIM:(g + 1) * HEAD_DIM]
            cols.append(jnp.dot(w, vg, preferred_element_type=jnp.float32))
        rows.append(jnp.concatenate(cols, axis=1) + bexp)
    spatial = jnp.concatenate(rows, axis=0) if n_chunks > 1 else rows[0]
    o_ref[0] = _rms(u * spatial, on_ref[...]).astype(o_ref.dtype)


def gmlp_mixer(z3, sgu_gain, w_sp, b_sp, on_gain, tt=256):
    b, t, _ = z3.shape
    c = GMLP_WIDTH
    bexp = jnp.repeat(b_sp.T, HEAD_DIM, axis=1)
    const2 = lambda shape: pl.BlockSpec(shape, lambda bi, i: (0,) * len(shape))
    return pl.pallas_call(
        functools.partial(_gmlp_kernel, n_chunks=tt // GMLP_CHUNK),
        grid=(b, t // tt),
        in_specs=[pl.BlockSpec((1, tt, c), lambda bi, i: (bi, i, 0)),
                  pl.BlockSpec((1, tt, c), lambda bi, i: (bi, i, 1)),
                  const2((1, c)), const2((GMLP_GROUPS, GMLP_CHUNK, GMLP_CHUNK)),
                  const2((GMLP_CHUNK, c)), const2((1, c))],
        out_specs=pl.BlockSpec((1, tt, c), lambda bi, i: (bi, i, 0)),
        out_shape=jax.ShapeDtypeStruct((b, t, c), jnp.bfloat16),
        compiler_params=_cparams(("parallel", "parallel")),
        name="gmlp_mixer",
    )(z3, z3, sgu_gain.reshape(1, c), w_sp, bexp, on_gain.reshape(1, c))


ATT_TILE = 512
KV_TILE_SHIFT = 9
AUG_DIM = 2 * HEAD_DIM
POS_LANE = 64
ONE_LANE = 70
FEAT_END = 73
MASK_BIG = 2.0 ** 100
MASKED = -MASK_BIG
M_INIT = NEG_INF
LOG2E = 1.4426950408889634


def _kvprep_kernel(z_ref, kg_ref, o_ref, *, tt):
    i = pl.program_id(1)
    z = z_ref[0]
    pos = i * tt + lax.broadcasted_iota(jnp.int32, (tt, HEAD_DIM), 0)
    lane = lax.broadcasted_iota(jnp.int32, (tt, HEAD_DIM), 1)
    blk = lax.shift_right_logical(pos, SLC_SHIFT)
    hi = lax.shift_left(blk, SLC_SHIFT).astype(jnp.float32)
    lo = jnp.bitwise_and(pos, SLC_BLOCK - 1).astype(jnp.float32)
    posf = jnp.where(jnp.bitwise_and(lane, 1) == 0, hi, lo)
    tail = jnp.where(lane < ONE_LANE, posf, jnp.where(lane < FEAT_END, 1.0, 0.0))
    onehot = jnp.where(lane == blk, MASK_BIG, 0.0)
    kfeat = (jnp.where(lane < POS_LANE, onehot, tail).astype(o_ref.dtype),
             jnp.where(lane < POS_LANE, 0.0, tail).astype(o_ref.dtype))
    vfeat = jnp.ones((tt, HEAD_DIM), o_ref.dtype)
    for part in range(4):
        for g in range(NSA_KV_HEADS):
            src = part * KV_WIDTH + g * HEAD_DIM
            dst = (part * NSA_KV_HEADS + g) * AUG_DIM
            x = z[:, src:src + HEAD_DIM]
            if part % 2 == 0:
                x = _rms(x, kg_ref[1 + part // 2:2 + part // 2, :])
            o_ref[0, :, dst:dst + HEAD_DIM] = x.astype(o_ref.dtype)
            o_ref[0, :, dst + HEAD_DIM:dst + AUG_DIM] = kfeat[part // 2] if part % 2 == 0 else vfeat


def kv_prep(z3, k_gain, tt=256):
    b, t, _ = z3.shape
    w = 4 * KV_WIDTH
    wo = 4 * NSA_KV_HEADS * AUG_DIM
    kg = jnp.pad(k_gain, ((0, 8 - NSA_BRANCHES), (0, 0)))
    return pl.pallas_call(
        functools.partial(_kvprep_kernel, tt=tt),
        grid=(b, t // tt),
        in_specs=[pl.BlockSpec((1, tt, w), lambda bi, i: (bi, i, KV4_OFF // w)),
                  pl.BlockSpec((8, HEAD_DIM), lambda bi, i: (0, 0))],
        out_specs=pl.BlockSpec((1, tt, wo), lambda bi, i: (bi, i, 0)),
        out_shape=jax.ShapeDtypeStruct((b, t, wo), jnp.bfloat16),
        compiler_params=_cparams(("parallel", "parallel")),
        name="kv_prep",
    )(z3, kg)


def _compress_kernel(k_ref, v_ref, w1_ref, w2_ref, pe_ref, kg_ref, kc_ref, vc_ref, *, n_chunk):
    def one(src_ref, which):
        ya = jnp.zeros((n_chunk, HEAD_DIM), jnp.float32)
        yb = jnp.zeros((n_chunk, HEAD_DIM), jnp.float32)
        bias = jnp.zeros((8, HEAD_DIM), jnp.float32)
        for l in range(CMP_STRIDE):
            xl = src_ref[0, pl.ds(l, n_chunk, stride=CMP_STRIDE), :].astype(jnp.bfloat16)
            ya += jnp.dot(xl, w1_ref[which, l], preferred_element_type=jnp.float32)
            yb += jnp.dot(xl, w1_ref[which, CMP_STRIDE + l], preferred_element_type=jnp.float32)
        for l in range(CMP_LEN):
            pe_l = jnp.broadcast_to(pe_ref[which, l:l + 1, :], (8, HEAD_DIM)).astype(jnp.bfloat16)
            bias += jnp.dot(pe_l, w1_ref[which, l], preferred_element_type=jnp.float32)
        hid = _gelu(ya + pltpu.roll(yb, n_chunk - 1, 0) + bias[0:1])
        return jnp.dot(hid.astype(jnp.bfloat16), w2_ref[which], preferred_element_type=jnp.float32)

    kc_ref[0, 0] = _rms(one(k_ref, 0), kg_ref[0:1, :]).astype(kc_ref.dtype)
    vc_ref[0, 0] = one(v_ref, 1).astype(vc_ref.dtype)


def compress_kv(z3, cmp_pos, cmp_w1, cmp_w2, k_gain):
    b, t, _ = z3.shape
    n_chunk = t // CMP_STRIDE
    kg = jnp.pad(k_gain, ((0, 8 - NSA_BRANCHES), (0, 0)))
    const = lambda shape: pl.BlockSpec(shape, lambda bi, g: (0,) * len(shape))
    out = jax.ShapeDtypeStruct((b, NSA_KV_HEADS, n_chunk, HEAD_DIM), jnp.bfloat16)
    return pl.pallas_call(
        functools.partial(_compress_kernel, n_chunk=n_chunk),
        grid=(b, NSA_KV_HEADS),
        in_specs=[pl.BlockSpec((1, t, HEAD_DIM), lambda bi, g: (bi, 0, KCMP_OFF // HEAD_DIM + g)),
                  pl.BlockSpec((1, t, HEAD_DIM), lambda bi, g: (bi, 0, VCMP_OFF // HEAD_DIM + g)),
                  const((2, CMP_LEN, HEAD_DIM, HEAD_DIM)), const((2, HEAD_DIM, HEAD_DIM)),
                  const((2, CMP_LEN, HEAD_DIM)), const((8, HEAD_DIM))],
        out_specs=[pl.BlockSpec((1, 1, n_chunk, HEAD_DIM), lambda bi, g: (bi, g, 0, 0))] * 2,
        out_shape=[out, out],
        compiler_params=_cparams(("parallel", "parallel")),
        name="compress_kv",
    )(z3, z3, cmp_w1.astype(jnp.bfloat16), cmp_w2.astype(jnp.bfloat16), cmp_pos, kg)


def _alibi_slopes():
    h = np.arange(1, NSA_HEADS + 1, dtype=np.float32)
    return np.power(np.float32(2.0), -8.0 * h / NSA_HEADS).astype(np.float32)


def _overlap_matrix(n_chunk, n_slc):
    n_cmp = n_chunk - CMP_LEN // CMP_STRIDE + 1
    jc = np.arange(n_cmp)[:, None] * CMP_STRIDE
    bs = np.arange(n_slc)[None, :] * SLC_BLOCK
    ov = np.clip(np.minimum(jc + CMP_LEN, bs + SLC_BLOCK) - np.maximum(jc, bs), 0, None)
    out = np.zeros((n_chunk, LANES), np.float32)
    out[:n_cmp, :n_slc] = ov.astype(np.float32) / CMP_LEN
    return out


def _q_heads(q_ref, qg_ref):
    q = q_ref[0]
    scale = HEAD_DIM ** -0.5
    return [(_rms(q[:, r * HEAD_DIM:(r + 1) * HEAD_DIM], qg_ref[...]) * scale).astype(jnp.bfloat16)
            for r in range(NSA_REP)]


def _gate_expand_matrix():
    m = np.zeros((NSA_BRANCHES, NSA_KV_HEADS, LANES, NSA_REP * HEAD_DIM), np.float32)
    for br in range(NSA_BRANCHES):
        for g in range(NSA_KV_HEADS):
            for r in range(NSA_REP):
                m[br, g, (g * NSA_REP + r) * NSA_BRANCHES + br, r * HEAD_DIM:(r + 1) * HEAD_DIM] = 1.0
    return m


def _branch_gate(gl_ref, ex_ref):
    sig = jax.nn.sigmoid(gl_ref[0])
    hi = sig.astype(jnp.bfloat16)
    lo = (sig - hi.astype(jnp.float32)).astype(jnp.bfloat16)
    ex = ex_ref[...]
    return (jnp.dot(hi, ex, preferred_element_type=jnp.float32)
            + jnp.dot(lo, ex, preferred_element_type=jnp.float32))


def _cmp_kernel(slopes_ref, q_ref, kc_ref, vc_ref, qg_ref, ov_ref, gl_ref, ex_ref, o_ref, sel_ref,
                used_ref, *, tq, n_chunk, n_slc):
    g = pl.program_id(1)
    i = pl.program_id(2)
    gate = _branch_gate(gl_ref, ex_ref)
    n_cmp = n_chunk - CMP_LEN // CMP_STRIDE + 1
    qh = _q_heads(q_ref, qg_ref)
    kc = kc_ref[0, 0]
    vc = vc_ref[0, 0]
    t = i * tq + lax.broadcasted_iota(jnp.int32, (tq, n_chunk), 0)
    n = lax.broadcasted_iota(jnp.int32, (tq, n_chunk), 1)
    dist = t - (n * CMP_STRIDE + CMP_LEN - 1)
    mask = jnp.logical_and(dist >= 0, n < n_cmp)
    distf = dist.astype(jnp.float32)
    psum = jnp.zeros((tq, n_chunk), jnp.float32)
    for r in range(NSA_REP):
        s = lax.dot_general(qh[r], kc, (((1,), (1,)), ((), ())),
                            preferred_element_type=jnp.float32)
        s = jnp.where(mask, s - slopes_ref[g * NSA_REP + r] * distf, NEG_INF)
        mx = jnp.max(s, axis=-1, keepdims=True)
        e = jnp.where(mask, jnp.exp(s - mx), 0.0)
        den = jnp.sum(e, axis=-1, keepdims=True)
        p = e / jnp.where(den > 0.0, den, 1.0)
        psum += p
        o_ref[0, :, r * HEAD_DIM:(r + 1) * HEAD_DIM] = gate[:, r * HEAD_DIM:(r + 1) * HEAD_DIM] * jnp.dot(
            p.astype(jnp.bfloat16), vc, preferred_element_type=jnp.float32)

    ov = ov_ref[...]
    imp = jnp.zeros((tq, LANES), jnp.float32)
    rem = psum
    for _ in range(3):
        part = rem.astype(jnp.bfloat16)
        imp += jnp.dot(part, ov, preferred_element_type=jnp.float32)
        rem = rem - part.astype(jnp.float32)

    tb = i * tq + lax.broadcasted_iota(jnp.int32, (tq, LANES), 0)
    blk = lax.broadcasted_iota(jnp.int32, (tq, LANES), 1)
    qblk = lax.shift_right_logical(tb, SLC_SHIFT)
    forced = (blk == 0) | (blk == qblk) | (blk == qblk - 1)
    score = jnp.where(forced, FORCE_SCORE, jnp.where(blk <= qblk, imp, NEG_INF))
    score = jnp.where(blk < n_slc, score, BELOW_NEG_INF)
    st = score.T[0:n_slc]
    n_tiles = n_slc // 8
    tiles = [st[8 * j:8 * j + 8] for j in range(n_tiles)]
    sub8 = lax.broadcasted_iota(jnp.int32, (8, tq), 0)
    cnt = [jnp.zeros((8, tq), jnp.float32) for _ in range(n_tiles)]
    for m in range(n_slc):
        row = st[m:m + 1, :]
        own, off = divmod(m, 8)
        for j in range(n_tiles):
            if j < own:
                inc = jnp.where(row > tiles[j], 1.0, 0.0)
            elif j > own:
                inc = jnp.where(row >= tiles[j], 1.0, 0.0)
            else:
                later = jnp.where(sub8 > off, 1.0, 0.0)
                inc = (jnp.where(row >= tiles[j], later, 0.0)
                       + jnp.where(row > tiles[j], 1.0 - later, 0.0))
            cnt[j] = cnt[j] + inc
    keep = float(min(N_SELECT, n_slc))
    sel_t = jnp.concatenate([jnp.where(c < keep, 1.0, 0.0) for c in cnt]
                            + [jnp.zeros((LANES - n_slc, tq), jnp.float32)] * (n_slc < LANES), axis=0)
    sel = sel_t.T
    sel_ref[0, 0] = sel.astype(sel_ref.dtype)
    picked = jnp.broadcast_to(jnp.max(sel, axis=0, keepdims=True), (8, LANES)).astype(jnp.bfloat16)
    blk_i = lax.broadcasted_iota(jnp.int32, (LANES, LANES), 0)
    tile_i = lax.broadcasted_iota(jnp.int32, (LANES, LANES), 1)
    group = jnp.where(lax.shift_right_logical(blk_i, KV_TILE_SHIFT - SLC_SHIFT) == tile_i, 1.0, 0.0)
    used = jnp.dot(picked, group.astype(jnp.bfloat16), preferred_element_type=jnp.float32)
    used_ref[0, 0] = jnp.where(used > 0.5, 1, 0).astype(jnp.int32)


def cmp_attention(z3, kc, vc, q_gain, gate_logits):
    b, t, _ = z3.shape
    tq = min(ATT_TILE, t)
    n_chunk = t // CMP_STRIDE
    n_slc = t // SLC_BLOCK
    assert n_slc <= LANES
    gw = NSA_REP * HEAD_DIM
    grid_spec = pltpu.PrefetchScalarGridSpec(
        num_scalar_prefetch=1,
        grid=(b, NSA_KV_HEADS, t // tq),
        in_specs=[pl.BlockSpec((1, tq, gw), lambda bi, g, i, s: (bi, i, Q_OFF // gw + g)),
                  pl.BlockSpec((1, 1, n_chunk, HEAD_DIM), lambda bi, g, i, s: (bi, g, 0, 0)),
                  pl.BlockSpec((1, 1, n_chunk, HEAD_DIM), lambda bi, g, i, s: (bi, g, 0, 0)),
                  pl.BlockSpec((1, HEAD_DIM), lambda bi, g, i, s: (0, 0)),
                  pl.BlockSpec((n_chunk, LANES), lambda bi, g, i, s: (0, 0)),
                  pl.BlockSpec((1, tq, LANES), lambda bi, g, i, s: (bi, i, 0)),
                  pl.BlockSpec((None, None, LANES, gw), lambda bi, g, i, s: (0, g, 0, 0))],
        out_specs=[pl.BlockSpec((1, tq, gw), lambda bi, g, i, s: (bi, i, g)),
                   pl.BlockSpec((1, 1, tq, LANES), lambda bi, g, i, s: (bi, g, i, 0)),
                   pl.BlockSpec((1, 1, 8, LANES), lambda bi, g, i, s: (bi, g, i, 0))],
    )
    return pl.pallas_call(
        functools.partial(_cmp_kernel, tq=tq, n_chunk=n_chunk, n_slc=n_slc),
        grid_spec=grid_spec,
        out_shape=[jax.ShapeDtypeStruct((b, t, NSA_WIDTH), jnp.float32),
                   jax.ShapeDtypeStruct((b, NSA_KV_HEADS, t, LANES), jnp.bfloat16),
                   jax.ShapeDtypeStruct((b, NSA_KV_HEADS, t // tq * 8, LANES), jnp.int32)],
        compiler_params=_cparams(("parallel", "parallel", "parallel")),
        name="cmp_attention",
    )(jnp.asarray(_alibi_slopes()), z3, kc, vc, q_gain.reshape(1, HEAD_DIM),
      jnp.asarray(_overlap_matrix(n_chunk, n_slc), dtype=jnp.bfloat16), gate_logits,
      jnp.asarray(_gate_expand_matrix(), dtype=jnp.bfloat16))


STEP_FIRST, STEP_LAST, STEP_MASKED = 1, 2, 4


def _flash_steps(mode, t, tq, tk):
    qi, kv, fl = [], [], []
    for i in range(t // tq):
        lo, hi = i * tq, i * tq + tq - 1
        first_key = 0 if mode == "slc" else max(lo - WINDOW + 1, 0)
        tiles = list(range(first_key // tk, hi // tk + 1))
        for n, j in enumerate(tiles):
            k_lo, k_hi = j * tk, j * tk + tk - 1
            causal_ok = k_hi <= lo
            window_ok = mode == "slc" or hi - k_lo < WINDOW
            flag = (STEP_FIRST if n == 0 else 0) | (STEP_LAST if n == len(tiles) - 1 else 0)
            flag |= 0 if (causal_ok and window_ok) else STEP_MASKED
            qi.append(i), kv.append(j), fl.append(flag)
    return [np.asarray(a, np.int32) for a in (qi, kv, fl)]


def _split3(x):
    a = x.astype(jnp.bfloat16).astype(jnp.float32)
    b = (x - a).astype(jnp.bfloat16).astype(jnp.float32)
    c = (x - a - b).astype(jnp.bfloat16).astype(jnp.float32)
    return a, b, c


def _flash_kernel(slopes_ref, qi_ref, kv_ref, fl_ref, used_ref, *refs, mode, tq, tk):
    if mode == "slc":
        q_ref, k_ref, v_ref, qg_ref, gl_ref, ex_ref, prev_ref, sel_ref, o_ref, q_sc, m_sc, acc_sc = refs
    else:
        q_ref, k_ref, v_ref, qg_ref, gl_ref, ex_ref, prev_ref, o_ref, ssq_ref, q_sc, m_sc, acc_sc = refs
    g = pl.program_id(1)
    s = pl.program_id(2)
    i = qi_ref[s]
    kv_idx = kv_ref[s]
    flags = fl_ref[s]

    @pl.when(jnp.bitwise_and(flags, STEP_FIRST) != 0)
    def _():
        q = q_ref[0]
        lane = lax.broadcasted_iota(jnp.int32, (tq, HEAD_DIM), 1)
        if mode == "slc":
            sel_m1 = sel_ref[0, 0].astype(jnp.float32) - 1.0
        else:
            sel_m1 = jnp.zeros((tq, HEAD_DIM), jnp.float32)
        t0 = (i * tq).astype(jnp.float32)
        for r in range(NSA_REP):
            qn = _rms(q[:, r * HEAD_DIM:(r + 1) * HEAD_DIM], qg_ref[...]) * (HEAD_DIM ** -0.5 * LOG2E)
            slope2 = slopes_ref[g * NSA_REP + r] * LOG2E
            a, b, c = _split3(jnp.where(lane < ONE_LANE, slope2, -slope2 * t0))
            k3 = jnp.where(lane < ONE_LANE, lax.shift_right_logical(lane - POS_LANE, 1), lane - ONE_LANE)
            term = jnp.where(k3 == 0, a, jnp.where(k3 == 1, b, c))
            feat = jnp.where(lane < POS_LANE, sel_m1, jnp.where(lane < FEAT_END, term, 0.0))
            q_sc[r, :, 0:HEAD_DIM] = qn.astype(q_sc.dtype)
            q_sc[r, :, HEAD_DIM:AUG_DIM] = feat.astype(q_sc.dtype)
        m_sc[...] = jnp.full(m_sc.shape, M_INIT, jnp.float32)
        acc_sc[...] = jnp.zeros(acc_sc.shape, jnp.float32)

    def step(masked):
        k = k_ref[0]
        v = v_ref[0]
        if masked:
            t = i * tq + lax.broadcasted_iota(jnp.int32, (tq, tk), 0)
            dist = t - (kv_idx * tk + lax.broadcasted_iota(jnp.int32, (tq, tk), 1))
            ok = dist >= 0 if mode == "slc" else jnp.logical_and(dist >= 0, dist < WINDOW)
        for r in range(NSA_REP):
            s2 = lax.dot_general(q_sc[r], k, (((1,), (1,)), ((), ())),
                                 preferred_element_type=jnp.float32)
            if masked:
                s2 = jnp.where(ok, s2, MASKED)
            m_prev = m_sc[r]
            m_new = jnp.maximum(m_prev, jnp.max(s2, axis=-1, keepdims=True))
            p = jnp.exp2(s2 - jnp.tile(m_new, (1, tk // HEAD_DIM))).astype(jnp.bfloat16)
            alpha = jnp.exp2(m_prev - m_new)
            acc_sc[r] = jnp.tile(alpha, (1, AUG_DIM // HEAD_DIM)) * acc_sc[r] + jnp.dot(
                p, v, preferred_element_type=jnp.float32)
            m_sc[r] = m_new

    is_masked = jnp.bitwise_and(flags, STEP_MASKED) != 0
    if mode == "slc":
        bg = pl.program_id(0) * NSA_KV_HEADS + g
        live = used_ref[(bg * pl.num_programs(2) + s)] != 0
    else:
        live = True
    pl.when(jnp.logical_and(live, jnp.logical_not(is_masked)))(functools.partial(step, False))
    pl.when(jnp.logical_and(live, is_masked))(functools.partial(step, True))

    @pl.when(jnp.bitwise_and(flags, STEP_LAST) != 0)
    def _():
        gate = _branch_gate(gl_ref, ex_ref)
        prev = prev_ref[0]
        ssq = jnp.zeros((tq, LANES), jnp.float32)
        for r in range(NSA_REP):
            acc = acc_sc[r]
            cols = slice(r * HEAD_DIM, (r + 1) * HEAD_DIM)
            tot = prev[:, cols] + gate[:, cols] * (acc[:, 0:HEAD_DIM] / acc[:, HEAD_DIM:AUG_DIM])
            o_ref[0, :, cols] = tot.astype(o_ref.dtype)
            ssq = ssq + _lane_sum_sq(tot)
        if mode == "win":
            ssq_ref[0] = ssq


def flash_attention(mode, z3, kv3, q_gain, gate_logits, prev, sel=None, used=None):
    b, t, _ = z3.shape
    if mode == "slc":
        k_col, v_col, branch = 0, NSA_KV_HEADS, 1
    else:
        k_col, v_col, branch = 2 * NSA_KV_HEADS, 3 * NSA_KV_HEADS, 2
    tq = tk = min(ATT_TILE, t)
    gw = NSA_REP * HEAD_DIM
    qi, kv, fl = _flash_steps(mode, t, tq, tk)
    if mode == "slc":
        used = used[:, :, ::8, :][:, :, qi, kv].reshape(-1)
    else:
        used = jnp.ones((1,), jnp.int32)

    q_tile = lambda col: (lambda bi, g, s, sl, qi, *_: (bi, qi[s], col + g))
    kv_tile = lambda col: (lambda bi, g, s, sl, qi, kv, *_: (bi, kv[s], col + g))
    in_specs = [pl.BlockSpec((1, tq, gw), q_tile(Q_OFF // gw)),
                pl.BlockSpec((1, tk, AUG_DIM), kv_tile(k_col)),
                pl.BlockSpec((1, tk, AUG_DIM), kv_tile(v_col)),
                pl.BlockSpec((1, HEAD_DIM), lambda bi, g, s, *_: (0, 0)),
                pl.BlockSpec((1, tq, LANES), lambda bi, g, s, sl, qi, *_: (bi, qi[s], 0)),
                pl.BlockSpec((None, None, LANES, gw), lambda bi, g, s, *_: (branch, g, 0, 0)),
                pl.BlockSpec((1, tq, gw), q_tile(0))]
    args = [jnp.asarray(_alibi_slopes()), jnp.asarray(qi), jnp.asarray(kv), jnp.asarray(fl), used,
            z3, kv3, kv3, q_gain.reshape(1, HEAD_DIM), gate_logits,
            jnp.asarray(_gate_expand_matrix(), dtype=jnp.bfloat16), prev]
    out_specs = pl.BlockSpec((1, tq, gw), q_tile(0))
    if mode == "slc":
        in_specs.append(pl.BlockSpec((1, 1, tq, LANES), lambda bi, g, s, sl, qi, *_: (bi, g, qi[s], 0)))
        args.append(sel)
        out_shape = jax.ShapeDtypeStruct((b, t, NSA_WIDTH), jnp.float32)
    else:
        out_specs = [out_specs, pl.BlockSpec((1, tq, LANES), q_tile(0))]
        out_shape = [jax.ShapeDtypeStruct((b, t, NSA_WIDTH), jnp.bfloat16),
                     jax.ShapeDtypeStruct((b, t, NSA_KV_HEADS * LANES), jnp.float32)]
    grid_spec = pltpu.PrefetchScalarGridSpec(
        num_scalar_prefetch=5,
        grid=(b, NSA_KV_HEADS, len(qi)),
        in_specs=in_specs,
        out_specs=out_specs,
        scratch_shapes=[pltpu.VMEM((NSA_REP, tq, AUG_DIM), jnp.bfloat16),
                        pltpu.VMEM((NSA_REP, tq, HEAD_DIM), jnp.float32),
                        pltpu.VMEM((NSA_REP, tq, AUG_DIM), jnp.float32)],
    )
    return pl.pallas_call(
        functools.partial(_flash_kernel, mode=mode, tq=tq, tk=tk),
        grid_spec=grid_spec,
        out_shape=out_shape,
        compiler_params=_cparams(("parallel", "parallel", "arbitrary")),
        name="flash_" + mode,
    )(*args)


def kernel(x, p, mix_norm, w_in, conv_w, q_gain, k_gain, cmp_pos, cmp_w1, cmp_w2, sgu_gain, w_sp,
           b_sp, mix_out_norm, w_out, mlp_norm, w_mlp_in, w_mlp_out, ple_norm, w_ple_proj,
           w_ple_gate):
    b, t, d = x.shape
    m = b * t
    bf = jnp.bfloat16
    depth = p.shape[0]
    p_rows = p.reshape(depth, m, PLE_DIM)
    w_in_t = jnp.swapaxes(w_in, 1, 2)
    ones = jnp.ones((D_FF,), jnp.float32)
    h2 = x.reshape(m, d)
    hb, ssq = stats_cast(h2)
    for i in range(depth):
        g_in = mix_norm[i]
        z_a = matmul(hb, cast_weight_t(w_in_t, i, g_in, 0, Z_A_COLS), ssq=ssq).reshape(b, t, Z_A_COLS)
        z_g = matmul(hb, cast_weight_t(w_in_t, i, g_in, Z_G_OFF, Z_G_COLS), ssq=ssq).reshape(b, t, Z_G_COLS)
        w_gate = cast_weight_t(w_in_t, i, g_in, Z_A_COLS, LANES, keep=N_GATES)
        gate_logits = matmul(hb, w_gate, ssq=ssq, tn=LANES).reshape(b, t, LANES)

        on = mix_out_norm[i]
        mixed_a = conv_mixer(z_a, conv_w[i], on[:CONV_WIDTH])
        kc, vc = compress_kv(z_a, cmp_pos[i], cmp_w1[i], cmp_w2[i], k_gain[i])
        o_b, sel, used = cmp_attention(z_a, kc, vc, q_gain[i], gate_logits)
        kv3 = kv_prep(z_a, k_gain[i])
        o_b = flash_attention("slc", z_a, kv3, q_gain[i], gate_logits, o_b, sel=sel, used=used)
        o_b, ssq_b = flash_attention("win", z_a, kv3, q_gain[i], gate_logits, o_b)
        mixed_c = gmlp_mixer(z_g, sgu_gain[i], w_sp[i], b_sp[i], on[CONV_WIDTH + NSA_WIDTH:])
        g_out = jnp.concatenate([ones[:CONV_WIDTH], on[CONV_WIDTH:CONV_WIDTH + NSA_WIDTH],
                                 ones[:GMLP_WIDTH]])
        mixed = (mixed_a.reshape(m, CONV_WIDTH), o_b.reshape(m, NSA_WIDTH),
                 mixed_c.reshape(m, GMLP_WIDTH))
        h2, hb, ssq = matmul(mixed, cast_weight(w_out, i, g_out), mode="residual", res=h2,
                             ssq=(None, ssq_b.reshape(m, NSA_KV_HEADS * LANES), None),
                             stats=True, tn=512)

        hid = matmul(hb, cast_weight(w_mlp_in, i, mlp_norm[i]), mode="relu2", out_dtype=bf, ssq=ssq)
        h2 = matmul(hid, cast_weight(w_mlp_out, i, ones), mode="residual", res=h2)
        hb, ssq = stats_cast(h2)

        out = matmul(hb, cast_weight(w_ple_gate, i, ple_norm[i]), mode="ple", res=h2, p=p_rows,
                     wp=w_ple_proj, layer=i, ssq=ssq, stats=i + 1 < depth, tn=512)
        h2, hb, ssq = out if i + 1 < depth else (out, None, None)
    return h2.reshape(b, t, d)
```

```python
import functools

import jax
import jax.numpy as jnp
import numpy as np
from jax import lax
from jax.experimental import pallas as pl
from jax.experimental.pallas import tpu as pltpu

D_MODEL = 4096
HEAD_DIM = 128
CONV_WIDTH = D_MODEL // 4
CONV_K = 3
NSA_HEADS = (D_MODEL // 2) // HEAD_DIM
NSA_WIDTH = NSA_HEADS * HEAD_DIM
NSA_KV_HEADS = 4
NSA_REP = NSA_HEADS // NSA_KV_HEADS
NSA_BRANCHES = 3
KV_WIDTH = NSA_KV_HEADS * HEAD_DIM
CMP_LEN = 32
CMP_STRIDE = 16
SLC_BLOCK = 64
SLC_SHIFT = 6
N_SELECT = 16
WINDOW = 512
GMLP_WIDTH = D_MODEL // 4
GMLP_GROUPS = GMLP_WIDTH // HEAD_DIM
GMLP_CHUNK = 128
D_FF = 4 * D_MODEL
PLE_DIM = 256
N_GATES = NSA_HEADS * NSA_BRANCHES
EPS = 1e-6
NEG_INF = -1e30
FORCE_SCORE = 1e9
BELOW_NEG_INF = -3e38

Z_A_COLS = 3 * CONV_WIDTH + NSA_WIDTH + 2 * NSA_BRANCHES * KV_WIDTH
Z_G_COLS = 2 * GMLP_WIDTH
Z_G_OFF = Z_A_COLS + N_GATES
Q_OFF = 3 * CONV_WIDTH
KCMP_OFF = Q_OFF + NSA_WIDTH
VCMP_OFF = KCMP_OFF + KV_WIDTH
KV4_OFF = VCMP_OFF + KV_WIDTH

LANES = 128
VMEM_LIMIT_BYTES = 60 * 1024 * 1024


def _cparams(dims):
    return pltpu.CompilerParams(dimension_semantics=dims, vmem_limit_bytes=VMEM_LIMIT_BYTES)


def _rms(x, gain):
    return x * lax.rsqrt(jnp.mean(x * x, axis=-1, keepdims=True) + EPS) * gain


def _gelu(x):
    return jax.nn.gelu(x)


def _row_scale(ssq, width, n_groups=1):
    tot = ssq[:, 0:LANES]
    for gi in range(1, n_groups):
        tot = tot + ssq[:, gi * LANES:(gi + 1) * LANES]
    return lax.rsqrt(tot * (1.0 / width) + EPS)


def _lane_sum_sq(x):
    return jnp.broadcast_to(jnp.sum(x * x, axis=-1, keepdims=True), (x.shape[0], LANES))


def _stats_kernel(x_ref, hb_ref, ssq_ref):
    x = x_ref[...]
    hb_ref[...] = x.astype(hb_ref.dtype)
    ssq_ref[...] = _lane_sum_sq(x)


def stats_cast(x, tm=512):
    m, d = x.shape
    return pl.pallas_call(
        _stats_kernel,
        grid=(m // tm,),
        in_specs=[pl.BlockSpec((tm, d), lambda i: (i, 0))],
        out_specs=[pl.BlockSpec((tm, d), lambda i: (i, 0)),
                   pl.BlockSpec((tm, LANES), lambda i: (i, 0))],
        out_shape=[jax.ShapeDtypeStruct((m, d), jnp.bfloat16),
                   jax.ShapeDtypeStruct((m, LANES), jnp.float32)],
        compiler_params=_cparams(("parallel",)),
        name="stats_cast",
    )(x)


def _cast_kernel(w_ref, g_ref, o_ref):
    o_ref[...] = (w_ref[...] * g_ref[...]).astype(o_ref.dtype)


def _cast_t_kernel(a_ref, b_ref, g_ref, o_ref, *, shift, keep, transpose):
    x = a_ref[...]
    if shift:
        x = jnp.concatenate([x[shift:, :], b_ref[:shift, :]], axis=0)
    if keep < x.shape[0]:
        row = lax.broadcasted_iota(jnp.int32, x.shape, 0)
        x = jnp.where(row < keep, x, 0.0)
    x = x * g_ref[...]
    o_ref[...] = (x.T if transpose else x).astype(o_ref.dtype)


def cast_weight(w, layer, gain, tk=1024, tn=2048):
    _, kdim, n = w.shape
    tk, tn = min(tk, kdim), min(tn, n)
    assert kdim % tk == 0 and n % tn == 0
    return pl.pallas_call(
        _cast_kernel,
        grid=(kdim // tk, n // tn),
        in_specs=[pl.BlockSpec((None, tk, tn), lambda i, j: (layer, i, j)),
                  pl.BlockSpec((tk, 1), lambda i, j: (i, 0))],
        out_specs=pl.BlockSpec((tk, tn), lambda i, j: (i, j)),
        out_shape=jax.ShapeDtypeStruct((kdim, n), jnp.bfloat16),
        compiler_params=_cparams(("parallel", "parallel")),
        name="cast_weight",
    )(w, gain.reshape(kdim, 1))


def cast_weight_t(w_t, layer, gain, row0, nrows, keep=None, transpose=True, tr=512, tc=2048):
    _, n, kdim = w_t.shape
    tr, tc = min(tr, nrows), min(tc, kdim)
    shift = row0 % tr
    base = row0 // tr
    assert nrows % tr == 0 and kdim % tc == 0 and shift % 8 == 0
    block = lambda off: pl.BlockSpec((None, tr, tc), lambda i, j: (layer, base + j + off, i))
    return pl.pallas_call(
        functools.partial(_cast_t_kernel, shift=shift, keep=keep or nrows, transpose=transpose),
        grid=(kdim // tc, nrows // tr),
        in_specs=[block(0), block(1 if shift else 0), pl.BlockSpec((1, tc), lambda i, j: (0, i))],
        out_specs=(pl.BlockSpec((tc, tr), lambda i, j: (i, j)) if transpose
                   else pl.BlockSpec((tr, tc), lambda i, j: (j, i))),
        out_shape=jax.ShapeDtypeStruct((kdim, nrows) if transpose else (nrows, kdim), jnp.bfloat16),
        compiler_params=_cparams(("parallel", "parallel")),
        name="cast_weight_t",
    )(w_t, w_t, gain.reshape(1, kdim))


def _mm_kernel(*refs, mode, nk, n_x, scale_groups, stats, w_rows_are_outputs):
    x_refs, refs = refs[:n_x], refs[n_x:]
    refs = list(refs)
    take = lambda cond: refs.pop(0) if cond else None
    w_ref = refs.pop(0)
    res_ref = take(mode in ("residual", "ple"))
    p_ref, wp_ref = take(mode == "ple"), take(mode == "ple")
    ssq_refs = [take(g > 0) for g in scale_groups]
    o_ref = refs.pop(0)
    hb_ref, ssq_out_ref = take(stats), take(stats)
    tn = o_ref.shape[1]
    j = pl.program_id(1)
    k = pl.program_id(2)

    def scaled(d, n):
        if ssq_refs[n] is None:
            return d
        s = _row_scale(ssq_refs[n][...], x_refs[n].shape[1] * (nk if len(x_refs) == 1 else 1),
                       scale_groups[n])
        return d * jnp.tile(s, (1, tn // LANES))

    def finish(new):
        o_ref[...] = new
        if stats:
            hb_ref[...] = new.astype(hb_ref.dtype)
            ssq = _lane_sum_sq(new)

            @pl.when(j == 0)
            def _():
                ssq_out_ref[...] = ssq

            @pl.when(j > 0)
            def _():
                ssq_out_ref[...] += ssq

    def epilogue(acc):
        if mode == "plain":
            o_ref[...] = acc.astype(o_ref.dtype)
        elif mode == "relu2":
            r = jnp.maximum(acc, 0.0)
            o_ref[...] = (r * r).astype(o_ref.dtype)
        elif mode == "residual":
            finish(res_ref[...] + acc)
        else:
            proj = jnp.dot(p_ref[...].astype(jnp.bfloat16), wp_ref[...].astype(jnp.bfloat16),
                           preferred_element_type=jnp.float32)
            finish(res_ref[...] + jax.nn.sigmoid(acc) * proj)

    def product():
        part, off = None, 0
        for n, x_ref in enumerate(x_refs):
            kx = x_ref.shape[1]
            if w_rows_are_outputs:
                d = lax.dot_general(x_ref[...], w_ref[:, off:off + kx], (((1,), (1,)), ((), ())),
                                    preferred_element_type=jnp.float32)
            else:
                d = jnp.dot(x_ref[...], w_ref[off:off + kx, :], preferred_element_type=jnp.float32)
            d = scaled(d, n)
            part = d if part is None else part + d
            off += kx
        return part

    if nk == 1:
        epilogue(product())
        return
    if mode == "residual":
        @pl.when(k == 0)
        def _():
            o_ref[...] = res_ref[...] + product()

        @pl.when(jnp.logical_and(k > 0, k < nk - 1))
        def _():
            o_ref[...] += product()

        @pl.when(k == nk - 1)
        def _():
            finish(o_ref[...] + product())
        return
    acc_ref, = refs

    @pl.when(k == 0)
    def _():
        acc_ref[...] = product()

    @pl.when(jnp.logical_and(k > 0, k < nk - 1))
    def _():
        acc_ref[...] += product()

    @pl.when(k == nk - 1)
    def _():
        epilogue(acc_ref[...] + product())


def matmul(x, w, mode="plain", out_dtype=jnp.float32, res=None, p=None, wp=None, layer=0,
           ssq=None, stats=False, w_rows_are_outputs=False, tm=1024, tn=1024, tk=4096):
    xs = x if isinstance(x, (tuple, list)) else (x,)
    ssqs = ssq if isinstance(ssq, (tuple, list)) else (ssq,) * len(xs)
    m = xs[0].shape[0]
    kdim, n = w.shape[::-1] if w_rows_are_outputs else w.shape
    assert sum(xi.shape[1] for xi in xs) == kdim and len(ssqs) == len(xs)
    tm, tn, tk = min(tm, m), min(tn, n), min(tk, kdim)
    nk = kdim // tk
    assert m % tm == 0 and n % tn == 0 and kdim % tk == 0 and (nk == 1 or len(xs) == 1)
    assert not stats or mode in ("residual", "ple")
    in_specs = [pl.BlockSpec((tm, xi.shape[1] if len(xs) > 1 else tk), lambda i, j, k: (i, k))
                for xi in xs]
    in_specs.append(pl.BlockSpec((tn, tk), lambda i, j, k: (j, k)) if w_rows_are_outputs
                    else pl.BlockSpec((tk, tn), lambda i, j, k: (k, j)))
    args = [*xs, w]
    if mode in ("residual", "ple"):
        in_specs.append(pl.BlockSpec((tm, tn), lambda i, j, k: (i, j)))
        args.append(res)
    if mode == "ple":
        pd = p.shape[2]
        in_specs += [pl.BlockSpec((None, tm, pd), lambda i, j, k: (layer, i, 0)),
                     pl.BlockSpec((None, pd, tn), lambda i, j, k: (layer, 0, j))]
        args += [p, wp]
    scale_groups = tuple(0 if s is None else s.shape[1] // LANES for s in ssqs)
    for s in ssqs:
        if s is not None:
            in_specs.append(pl.BlockSpec((tm, s.shape[1]), lambda i, j, k: (i, 0)))
            args.append(s)
    tile = pl.BlockSpec((tm, tn), lambda i, j, k: (i, j))
    out_specs, out_shape = tile, jax.ShapeDtypeStruct((m, n), out_dtype)
    if stats:
        out_specs = [tile, tile, pl.BlockSpec((tm, LANES), lambda i, j, k: (i, 0))]
        out_shape = [out_shape, jax.ShapeDtypeStruct((m, n), jnp.bfloat16),
                     jax.ShapeDtypeStruct((m, LANES), jnp.float32)]
    scratch = [pltpu.VMEM((tm, tn), jnp.float32)] if nk > 1 and mode != "residual" else []
    return pl.pallas_call(
        functools.partial(_mm_kernel, mode=mode, nk=nk, n_x=len(xs), scale_groups=scale_groups,
                          stats=stats, w_rows_are_outputs=w_rows_are_outputs),
        grid=(m // tm, n // tn, nk),
        in_specs=in_specs,
        out_specs=out_specs,
        out_shape=out_shape,
        scratch_shapes=scratch,
        compiler_params=_cparams(("parallel", "arbitrary" if stats else "parallel", "arbitrary")),
        name="matmul_" + mode,
    )(*args)


HALO_ROWS = 8


def _conv_kernel(xa_ref, gb_ref, gc_ref, xap_ref, gcp_ref, cw_ref, on_ref, o_ref):
    i = pl.program_id(1)
    inner = gc_ref[0] * xa_ref[0]
    prev = gcp_ref[0] * xap_ref[0]
    prev = jnp.where(i > 0, prev, 0.0)
    tt = inner.shape[0]
    row = lax.broadcasted_iota(jnp.int32, inner.shape, 0)
    m1 = jnp.where(row == 0, prev[HALO_ROWS - 1:HALO_ROWS], pltpu.roll(inner, 1, 0))
    m2 = pltpu.roll(inner, 2, 0)
    m2 = jnp.where(row == 0, prev[HALO_ROWS - 2:HALO_ROWS - 1], m2)
    m2 = jnp.where(row == 1, prev[HALO_ROWS - 1:HALO_ROWS], m2)
    cw = cw_ref[...]
    conv = cw[0:1] * m2 + cw[1:2] * m1 + cw[2:3] * inner
    o_ref[0] = _rms(gb_ref[0] * conv, on_ref[...]).astype(o_ref.dtype)


def conv_mixer(z3, conv_w, on_gain, tt=1024):
    b, t, _ = z3.shape
    c = CONV_WIDTH
    hb = tt // HALO_ROWS
    cur = lambda col: pl.BlockSpec((1, tt, c), lambda bi, i: (bi, i, col))
    halo = lambda col: pl.BlockSpec((1, HALO_ROWS, c),
                                    lambda bi, i: (bi, jnp.maximum(i * hb - 1, 0), col))
    return pl.pallas_call(
        _conv_kernel,
        grid=(b, t // tt),
        in_specs=[cur(0), cur(1), cur(2), halo(0), halo(2),
                  pl.BlockSpec((HALO_ROWS, c), lambda bi, i: (0, 0)),
                  pl.BlockSpec((1, c), lambda bi, i: (0, 0))],
        out_specs=pl.BlockSpec((1, tt, c), lambda bi, i: (bi, i, 0)),
        out_shape=jax.ShapeDtypeStruct((b, t, c), jnp.bfloat16),
        compiler_params=_cparams(("parallel", "parallel")),
        name="conv_mixer",
    )(z3, z3, z3, z3, z3,
      jnp.pad(conv_w, ((0, HALO_ROWS - CONV_K), (0, 0))), on_gain.reshape(1, c))


def _gmlp_kernel(gu_ref, gv_ref, sg_ref, wsp_ref, bexp_ref, on_ref, o_ref, *, n_chunks):
    u = _gelu(gu_ref[0])
    v = _rms(_gelu(gv_ref[0]), sg_ref[...]).astype(jnp.bfloat16)
    ck = GMLP_CHUNK
    r_i = lax.broadcasted_iota(jnp.int32, (ck, ck), 0)
    c_i = lax.broadcasted_iota(jnp.int32, (ck, ck), 1)
    bexp = bexp_ref[...]
    rows = []
    for c in range(n_chunks):
        cols = []
        for g in range(GMLP_GROUPS):
            w = jnp.where(r_i >= c_i, wsp_ref[g], 0.0).astype(jnp.bfloat16)
            vg = v[c * ck:(c + 1) * ck, g * HEAD_DIM:(g + 1) * HEAD_DIM]
            cols.append(jnp.dot(w, vg, preferred_element_type=jnp.float32))
        rows.append(jnp.concatenate(cols, axis=1) + bexp)
    spatial = jnp.concatenate(rows, axis=0) if n_chunks > 1 else rows[0]
    o_ref[0] = _rms(u * spatial, on_ref[...]).astype(o_ref.dtype)


def gmlp_mixer(z3, sgu_gain, w_sp, b_sp, on_gain, tt=512):
    b, t, _ = z3.shape
    c = GMLP_WIDTH
    bexp = jnp.repeat(b_sp.T, HEAD_DIM, axis=1)
    const2 = lambda shape: pl.BlockSpec(shape, lambda bi, i: (0,) * len(shape))
    return pl.pallas_call(
        functools.partial(_gmlp_kernel, n_chunks=tt // GMLP_CHUNK),
        grid=(b, t // tt),
        in_specs=[pl.BlockSpec((1, tt, c), lambda bi, i: (bi, i, 0)),
                  pl.BlockSpec((1, tt, c), lambda bi, i: (bi, i, 1)),
                  const2((1, c)), const2((GMLP_GROUPS, GMLP_CHUNK, GMLP_CHUNK)),
                  const2((GMLP_CHUNK, c)), const2((1, c))],
        out_specs=pl.BlockSpec((1, tt, c), lambda bi, i: (bi, i, 0)),
        out_shape=jax.ShapeDtypeStruct((b, t, c), jnp.bfloat16),
        compiler_params=_cparams(("parallel", "parallel")),
        name="gmlp_mixer",
    )(z3, z3, sgu_gain.reshape(1, c), w_sp, bexp, on_gain.reshape(1, c))


ATT_TILE = 512
KV_TILE_SHIFT = 9
AUG_DIM = 2 * HEAD_DIM
POS_LANE = 64
ONE_LANE = 70
FEAT_END = 73
MASK_BIG = 2.0 ** 100
MASKED = -MASK_BIG
M_INIT = NEG_INF
LOG2E = 1.4426950408889634


def _kvprep_kernel(z_ref, kg_ref, o_ref, *, tt):
    i = pl.program_id(1)
    z = z_ref[0]
    pos = i * tt + lax.broadcasted_iota(jnp.int32, (tt, HEAD_DIM), 0)
    lane = lax.broadcasted_iota(jnp.int32, (tt, HEAD_DIM), 1)
    blk = lax.shift_right_logical(pos, SLC_SHIFT)
    hi = lax.shift_left(blk, SLC_SHIFT).astype(jnp.float32)
    lo = jnp.bitwise_and(pos, SLC_BLOCK - 1).astype(jnp.float32)
    posf = jnp.where(jnp.bitwise_and(lane, 1) == 0, hi, lo)
    tail = jnp.where(lane < ONE_LANE, posf, jnp.where(lane < FEAT_END, 1.0, 0.0))
    onehot = jnp.where(lane == blk, MASK_BIG, 0.0)
    kfeat = (jnp.where(lane < POS_LANE, onehot, tail).astype(o_ref.dtype),
             jnp.where(lane < POS_LANE, 0.0, tail).astype(o_ref.dtype))
    vfeat = jnp.ones((tt, HEAD_DIM), o_ref.dtype)
    for part in range(4):
        for g in range(NSA_KV_HEADS):
            src = part * KV_WIDTH + g * HEAD_DIM
            dst = (part * NSA_KV_HEADS + g) * AUG_DIM
            x = z[:, src:src + HEAD_DIM]
            if part % 2 == 0:
                x = _rms(x, kg_ref[1 + part // 2:2 + part // 2, :])
            o_ref[0, :, dst:dst + HEAD_DIM] = x.astype(o_ref.dtype)
            o_ref[0, :, dst + HEAD_DIM:dst + AUG_DIM] = kfeat[part // 2] if part % 2 == 0 else vfeat


def kv_prep(z3, k_gain, tt=512):
    b, t, _ = z3.shape
    w = 4 * KV_WIDTH
    wo = 4 * NSA_KV_HEADS * AUG_DIM
    kg = jnp.pad(k_gain, ((0, 8 - NSA_BRANCHES), (0, 0)))
    return pl.pallas_call(
        functools.partial(_kvprep_kernel, tt=tt),
        grid=(b, t // tt),
        in_specs=[pl.BlockSpec((1, tt, w), lambda bi, i: (bi, i, KV4_OFF // w)),
                  pl.BlockSpec((8, HEAD_DIM), lambda bi, i: (0, 0))],
        out_specs=pl.BlockSpec((1, tt, wo), lambda bi, i: (bi, i, 0)),
        out_shape=jax.ShapeDtypeStruct((b, t, wo), jnp.bfloat16),
        compiler_params=_cparams(("parallel", "parallel")),
        name="kv_prep",
    )(z3, kg)


def _compress_kernel(k_ref, v_ref, w1_ref, w2_ref, pe_ref, kg_ref, kc_ref, vc_ref, *, n_chunk):
    def one(src_ref, which):
        ya = jnp.zeros((n_chunk, HEAD_DIM), jnp.float32)
        yb = jnp.zeros((n_chunk, HEAD_DIM), jnp.float32)
        bias = jnp.zeros((8, HEAD_DIM), jnp.float32)
        for l in range(CMP_STRIDE):
            xl = src_ref[0, pl.ds(l, n_chunk, stride=CMP_STRIDE), :].astype(jnp.bfloat16)
            ya += jnp.dot(xl, w1_ref[which, l], preferred_element_type=jnp.float32)
            yb += jnp.dot(xl, w1_ref[which, CMP_STRIDE + l], preferred_element_type=jnp.float32)
        for l in range(CMP_LEN):
            pe_l = jnp.broadcast_to(pe_ref[which, l:l + 1, :], (8, HEAD_DIM)).astype(jnp.bfloat16)
            bias += jnp.dot(pe_l, w1_ref[which, l], preferred_element_type=jnp.float32)
        hid = _gelu(ya + pltpu.roll(yb, n_chunk - 1, 0) + bias[0:1])
        return jnp.dot(hid.astype(jnp.bfloat16), w2_ref[which], preferred_element_type=jnp.float32)

    kc_ref[0, 0] = _rms(one(k_ref, 0), kg_ref[0:1, :]).astype(kc_ref.dtype)
    vc_ref[0, 0] = one(v_ref, 1).astype(vc_ref.dtype)


def compress_kv(z3, cmp_pos, cmp_w1, cmp_w2, k_gain):
    b, t, _ = z3.shape
    n_chunk = t // CMP_STRIDE
    kg = jnp.pad(k_gain, ((0, 8 - NSA_BRANCHES), (0, 0)))
    const = lambda shape: pl.BlockSpec(shape, lambda bi, g: (0,) * len(shape))
    out = jax.ShapeDtypeStruct((b, NSA_KV_HEADS, n_chunk, HEAD_DIM), jnp.bfloat16)
    return pl.pallas_call(
        functools.partial(_compress_kernel, n_chunk=n_chunk),
        grid=(b, NSA_KV_HEADS),
        in_specs=[pl.BlockSpec((1, t, HEAD_DIM), lambda bi, g: (bi, 0, KCMP_OFF // HEAD_DIM + g)),
                  pl.BlockSpec((1, t, HEAD_DIM), lambda bi, g: (bi, 0, VCMP_OFF // HEAD_DIM + g)),
                  const((2, CMP_LEN, HEAD_DIM, HEAD_DIM)), const((2, HEAD_DIM, HEAD_DIM)),
                  const((2, CMP_LEN, HEAD_DIM)), const((8, HEAD_DIM))],
        out_specs=[pl.BlockSpec((1, 1, n_chunk, HEAD_DIM), lambda bi, g: (bi, g, 0, 0))] * 2,
        out_shape=[out, out],
        compiler_params=_cparams(("parallel", "parallel")),
        name="compress_kv",
    )(z3, z3, cmp_w1.astype(jnp.bfloat16), cmp_w2.astype(jnp.bfloat16), cmp_pos, kg)


def _alibi_slopes():
    h = np.arange(1, NSA_HEADS + 1, dtype=np.float32)
    return np.power(np.float32(2.0), -8.0 * h / NSA_HEADS).astype(np.float32)


def _overlap_matrix(n_chunk, n_slc):
    n_cmp = n_chunk - CMP_LEN // CMP_STRIDE + 1
    jc = np.arange(n_cmp)[:, None] * CMP_STRIDE
    bs = np.arange(n_slc)[None, :] * SLC_BLOCK
    ov = np.clip(np.minimum(jc + CMP_LEN, bs + SLC_BLOCK) - np.maximum(jc, bs), 0, None)
    out = np.zeros((n_chunk, LANES), np.float32)
    out[:n_cmp, :n_slc] = ov.astype(np.float32) / CMP_LEN
    return out


def _q_heads(q_ref, qg_ref):
    q = q_ref[0]
    scale = HEAD_DIM ** -0.5
    return [(_rms(q[:, r * HEAD_DIM:(r + 1) * HEAD_DIM], qg_ref[...]) * scale).astype(jnp.bfloat16)
            for r in range(NSA_REP)]


def _gate_expand_matrix():
    m = np.zeros((NSA_BRANCHES, NSA_KV_HEADS, LANES, NSA_REP * HEAD_DIM), np.float32)
    for br in range(NSA_BRANCHES):
        for g in range(NSA_KV_HEADS):
            for r in range(NSA_REP):
                m[br, g, (g * NSA_REP + r) * NSA_BRANCHES + br, r * HEAD_DIM:(r + 1) * HEAD_DIM] = 1.0
    return m


def _branch_gate(gl_ref, ex_ref):
    sig = jax.nn.sigmoid(gl_ref[0])
    hi = sig.astype(jnp.bfloat16)
    lo = (sig - hi.astype(jnp.float32)).astype(jnp.bfloat16)
    ex = ex_ref[...]
    return (jnp.dot(hi, ex, preferred_element_type=jnp.float32)
            + jnp.dot(lo, ex, preferred_element_type=jnp.float32))


def _cmp_kernel(slopes_ref, q_ref, kc_ref, vc_ref, qg_ref, ov_ref, gl_ref, ex_ref, o_ref, sel_ref,
                used_ref, *, tq, n_chunk, n_slc):
    g = pl.program_id(1)
    i = pl.program_id(2)
    gate = _branch_gate(gl_ref, ex_ref)
    n_cmp = n_chunk - CMP_LEN // CMP_STRIDE + 1
    qh = _q_heads(q_ref, qg_ref)
    kc = kc_ref[0, 0]
    vc = vc_ref[0, 0]
    t = i * tq + lax.broadcasted_iota(jnp.int32, (tq, n_chunk), 0)
    n = lax.broadcasted_iota(jnp.int32, (tq, n_chunk), 1)
    dist = t - (n * CMP_STRIDE + CMP_LEN - 1)
    mask = jnp.logical_and(dist >= 0, n < n_cmp)
    distf = dist.astype(jnp.float32)
    psum = jnp.zeros((tq, n_chunk), jnp.float32)
    for r in range(NSA_REP):
        s = lax.dot_general(qh[r], kc, (((1,), (1,)), ((), ())),
                            preferred_element_type=jnp.float32)
        s = jnp.where(mask, s - slopes_ref[g * NSA_REP + r] * distf, NEG_INF)
        mx = jnp.max(s, axis=-1, keepdims=True)
        e = jnp.where(mask, jnp.exp(s - mx), 0.0)
        den = jnp.sum(e, axis=-1, keepdims=True)
        p = e / jnp.where(den > 0.0, den, 1.0)
        psum += p
        o_ref[0, :, r * HEAD_DIM:(r + 1) * HEAD_DIM] = gate[:, r * HEAD_DIM:(r + 1) * HEAD_DIM] * jnp.dot(
            p.astype(jnp.bfloat16), vc, preferred_element_type=jnp.float32)

    ov = ov_ref[...]
    imp = jnp.zeros((tq, LANES), jnp.float32)
    rem = psum
    for _ in range(3):
        part = rem.astype(jnp.bfloat16)
        imp += jnp.dot(part, ov, preferred_element_type=jnp.float32)
        rem = rem - part.astype(jnp.float32)

    tb = i * tq + lax.broadcasted_iota(jnp.int32, (tq, LANES), 0)
    blk = lax.broadcasted_iota(jnp.int32, (tq, LANES), 1)
    qblk = lax.shift_right_logical(tb, SLC_SHIFT)
    forced = (blk == 0) | (blk == qblk) | (blk == qblk - 1)
    score = jnp.where(forced, FORCE_SCORE, jnp.where(blk <= qblk, imp, NEG_INF))
    score = jnp.where(blk < n_slc, score, BELOW_NEG_INF)
    st = score.T[0:n_slc]
    n_tiles = n_slc // 8
    tiles = [st[8 * j:8 * j + 8] for j in range(n_tiles)]
    sub8 = lax.broadcasted_iota(jnp.int32, (8, tq), 0)
    cnt = [jnp.zeros((8, tq), jnp.float32) for _ in range(n_tiles)]
    for m in range(n_slc):
        row = st[m:m + 1, :]
        own, off = divmod(m, 8)
        for j in range(n_tiles):
            if j < own:
                inc = jnp.where(row > tiles[j], 1.0, 0.0)
            elif j > own:
                inc = jnp.where(row >= tiles[j], 1.0, 0.0)
            else:
                later = jnp.where(sub8 > off, 1.0, 0.0)
                inc = (jnp.where(row >= tiles[j], later, 0.0)
                       + jnp.where(row > tiles[j], 1.0 - later, 0.0))
            cnt[j] = cnt[j] + inc
    keep = float(min(N_SELECT, n_slc))
    sel_t = jnp.concatenate([jnp.where(c < keep, 1.0, 0.0) for c in cnt]
                            + [jnp.zeros((LANES - n_slc, tq), jnp.float32)] * (n_slc < LANES), axis=0)
    sel = sel_t.T
    sel_ref[0, 0] = sel.astype(sel_ref.dtype)
    picked = jnp.broadcast_to(jnp.max(sel, axis=0, keepdims=True), (8, LANES)).astype(jnp.bfloat16)
    blk_i = lax.broadcasted_iota(jnp.int32, (LANES, LANES), 0)
    tile_i = lax.broadcasted_iota(jnp.int32, (LANES, LANES), 1)
    group = jnp.where(lax.shift_right_logical(blk_i, KV_TILE_SHIFT - SLC_SHIFT) == tile_i, 1.0, 0.0)
    used = jnp.dot(picked, group.astype(jnp.bfloat16), preferred_element_type=jnp.float32)
    used_ref[0, 0] = jnp.where(used > 0.5, 1, 0).astype(jnp.int32)


def cmp_attention(z3, kc, vc, q_gain, gate_logits):
    b, t, _ = z3.shape
    tq = min(ATT_TILE, t)
    n_chunk = t // CMP_STRIDE
    n_slc = t // SLC_BLOCK
    assert n_slc <= LANES
    gw = NSA_REP * HEAD_DIM
    grid_spec = pltpu.PrefetchScalarGridSpec(
        num_scalar_prefetch=1,
        grid=(b, NSA_KV_HEADS, t // tq),
        in_specs=[pl.BlockSpec((1, tq, gw), lambda bi, g, i, s: (bi, i, Q_OFF // gw + g)),
                  pl.BlockSpec((1, 1, n_chunk, HEAD_DIM), lambda bi, g, i, s: (bi, g, 0, 0)),
                  pl.BlockSpec((1, 1, n_chunk, HEAD_DIM), lambda bi, g, i, s: (bi, g, 0, 0)),
                  pl.BlockSpec((1, HEAD_DIM), lambda bi, g, i, s: (0, 0)),
                  pl.BlockSpec((n_chunk, LANES), lambda bi, g, i, s: (0, 0)),
                  pl.BlockSpec((1, tq, LANES), lambda bi, g, i, s: (bi, i, 0)),
                  pl.BlockSpec((None, None, LANES, gw), lambda bi, g, i, s: (0, g, 0, 0))],
        out_specs=[pl.BlockSpec((1, tq, gw), lambda bi, g, i, s: (bi, i, g)),
                   pl.BlockSpec((1, 1, tq, LANES), lambda bi, g, i, s: (bi, g, i, 0)),
                   pl.BlockSpec((1, 1, 8, LANES), lambda bi, g, i, s: (bi, g, i, 0))],
    )
    return pl.pallas_call(
        functools.partial(_cmp_kernel, tq=tq, n_chunk=n_chunk, n_slc=n_slc),
        grid_spec=grid_spec,
        out_shape=[jax.ShapeDtypeStruct((b, t, NSA_WIDTH), jnp.float32),
                   jax.ShapeDtypeStruct((b, NSA_KV_HEADS, t, LANES), jnp.bfloat16),
                   jax.ShapeDtypeStruct((b, NSA_KV_HEADS, t // tq * 8, LANES), jnp.int32)],
        compiler_params=_cparams(("parallel", "parallel", "parallel")),
        name="cmp_attention",
    )(jnp.asarray(_alibi_slopes()), z3, kc, vc, q_gain.reshape(1, HEAD_DIM),
      jnp.asarray(_overlap_matrix(n_chunk, n_slc), dtype=jnp.bfloat16), gate_logits,
      jnp.asarray(_gate_expand_matrix(), dtype=jnp.bfloat16))


STEP_FIRST, STEP_LAST, STEP_MASKED = 1, 2, 4


def _flash_steps(mode, t, tq, tk):
    qi, kv, fl = [], [], []
    for i in range(t // tq):
        lo, hi = i * tq, i * tq + tq - 1
        first_key = 0 if mode == "slc" else max(lo - WINDOW + 1, 0)
        tiles = list(range(first_key // tk, hi // tk + 1))
        for n, j in enumerate(tiles):
            k_lo, k_hi = j * tk, j * tk + tk - 1
            causal_ok = k_hi <= lo
            window_ok = mode == "slc" or hi - k_lo < WINDOW
            flag = (STEP_FIRST if n == 0 else 0) | (STEP_LAST if n == len(tiles) - 1 else 0)
            flag |= 0 if (causal_ok and window_ok) else STEP_MASKED
            qi.append(i), kv.append(j), fl.append(flag)
    return [np.asarray(a, np.int32) for a in (qi, kv, fl)]


def _split3(x):
    a = x.astype(jnp.bfloat16).astype(jnp.float32)
    b = (x - a).astype(jnp.bfloat16).astype(jnp.float32)
    c = (x - a - b).astype(jnp.bfloat16).astype(jnp.float32)
    return a, b, c


def _flash_kernel(slopes_ref, qi_ref, kv_ref, fl_ref, used_ref, *refs, mode, tq, tk):
    if mode == "slc":
        q_ref, k_ref, v_ref, qg_ref, gl_ref, ex_ref, prev_ref, sel_ref, o_ref, q_sc, m_sc, acc_sc = refs
    else:
        q_ref, k_ref, v_ref, qg_ref, gl_ref, ex_ref, prev_ref, o_ref, ssq_ref, q_sc, m_sc, acc_sc = refs
    g = pl.program_id(1)
    s = pl.program_id(2)
    i = qi_ref[s]
    kv_idx = kv_ref[s]
    flags = fl_ref[s]

    @pl.when(jnp.bitwise_and(flags, STEP_FIRST) != 0)
    def _():
        q = q_ref[0]
        lane = lax.broadcasted_iota(jnp.int32, (tq, HEAD_DIM), 1)
        if mode == "slc":
            sel_m1 = sel_ref[0, 0].astype(jnp.float32) - 1.0
        else:
            sel_m1 = jnp.zeros((tq, HEAD_DIM), jnp.float32)
        t0 = (i * tq).astype(jnp.float32)
        for r in range(NSA_REP):
            qn = _rms(q[:, r * HEAD_DIM:(r + 1) * HEAD_DIM], qg_ref[...]) * (HEAD_DIM ** -0.5 * LOG2E)
            slope2 = slopes_ref[g * NSA_REP + r] * LOG2E
            a, b, c = _split3(jnp.where(lane < ONE_LANE, slope2, -slope2 * t0))
            k3 = jnp.where(lane < ONE_LANE, lax.shift_right_logical(lane - POS_LANE, 1), lane - ONE_LANE)
            term = jnp.where(k3 == 0, a, jnp.where(k3 == 1, b, c))
            feat = jnp.where(lane < POS_LANE, sel_m1, jnp.where(lane < FEAT_END, term, 0.0))
            q_sc[r, :, 0:HEAD_DIM] = qn.astype(q_sc.dtype)
            q_sc[r, :, HEAD_DIM:AUG_DIM] = feat.astype(q_sc.dtype)
        m_sc[...] = jnp.full(m_sc.shape, M_INIT, jnp.float32)
        acc_sc[...] = jnp.zeros(acc_sc.shape, jnp.float32)

    def step(masked):
        k = k_ref[0]
        v = v_ref[0]
        if masked:
            t = i * tq + lax.broadcasted_iota(jnp.int32, (tq, tk), 0)
            dist = t - (kv_idx * tk + lax.broadcasted_iota(jnp.int32, (tq, tk), 1))
            ok = dist >= 0 if mode == "slc" else jnp.logical_and(dist >= 0, dist < WINDOW)
        for r in range(NSA_REP):
            s2 = lax.dot_general(q_sc[r], k, (((1,), (1,)), ((), ())),
                                 preferred_element_type=jnp.float32)
            if masked:
                s2 = jnp.where(ok, s2, MASKED)
            m_prev = m_sc[r]
            m_new = jnp.maximum(m_prev, jnp.max(s2, axis=-1, keepdims=True))
            p = jnp.exp2(s2 - jnp.tile(m_new, (1, tk // HEAD_DIM))).astype(jnp.bfloat16)
            alpha = jnp.exp2(m_prev - m_new)
            acc_sc[r] = jnp.tile(alpha, (1, AUG_DIM // HEAD_DIM)) * acc_sc[r] + jnp.dot(
                p, v, preferred_element_type=jnp.float32)
            m_sc[r] = m_new

    is_masked = jnp.bitwise_and(flags, STEP_MASKED) != 0
    if mode == "slc":
        bg = pl.program_id(0) * NSA_KV_HEADS + g
        live = used_ref[(bg * pl.num_programs(2) + s)] != 0
    else:
        live = True
    pl.when(jnp.logical_and(live, jnp.logical_not(is_masked)))(functools.partial(step, False))
    pl.when(jnp.logical_and(live, is_masked))(functools.partial(step, True))

    @pl.when(jnp.bitwise_and(flags, STEP_LAST) != 0)
    def _():
        gate = _branch_gate(gl_ref, ex_ref)
        prev = prev_ref[0]
        ssq = jnp.zeros((tq, LANES), jnp.float32)
        for r in range(NSA_REP):
            acc = acc_sc[r]
            cols = slice(r * HEAD_DIM, (r + 1) * HEAD_DIM)
            tot = prev[:, cols] + gate[:, cols] * (acc[:, 0:HEAD_DIM] / acc[:, HEAD_DIM:AUG_DIM])
            o_ref[0, :, cols] = tot.astype(o_ref.dtype)
            ssq = ssq + _lane_sum_sq(tot)
        if mode == "win":
            ssq_ref[0] = ssq


def flash_attention(mode, z3, kv3, q_gain, gate_logits, prev, sel=None, used=None):
    b, t, _ = z3.shape
    if mode == "slc":
        k_col, v_col, branch = 0, NSA_KV_HEADS, 1
    else:
        k_col, v_col, branch = 2 * NSA_KV_HEADS, 3 * NSA_KV_HEADS, 2
    tq = tk = min(ATT_TILE, t)
    gw = NSA_REP * HEAD_DIM
    qi, kv, fl = _flash_steps(mode, t, tq, tk)
    if mode == "slc":
        used = used[:, :, ::8, :][:, :, qi, kv].reshape(-1)
    else:
        used = jnp.ones((1,), jnp.int32)

    q_tile = lambda col: (lambda bi, g, s, sl, qi, *_: (bi, qi[s], col + g))
    kv_tile = lambda col: (lambda bi, g, s, sl, qi, kv, *_: (bi, kv[s], col + g))
    in_specs = [pl.BlockSpec((1, tq, gw), q_tile(Q_OFF // gw)),
                pl.BlockSpec((1, tk, AUG_DIM), kv_tile(k_col)),
                pl.BlockSpec((1, tk, AUG_DIM), kv_tile(v_col)),
                pl.BlockSpec((1, HEAD_DIM), lambda bi, g, s, *_: (0, 0)),
                pl.BlockSpec((1, tq, LANES), lambda bi, g, s, sl, qi, *_: (bi, qi[s], 0)),
                pl.BlockSpec((None, None, LANES, gw), lambda bi, g, s, *_: (branch, g, 0, 0)),
                pl.BlockSpec((1, tq, gw), q_tile(0))]
    args = [jnp.asarray(_alibi_slopes()), jnp.asarray(qi), jnp.asarray(kv), jnp.asarray(fl), used,
            z3, kv3, kv3, q_gain.reshape(1, HEAD_DIM), gate_logits,
            jnp.asarray(_gate_expand_matrix(), dtype=jnp.bfloat16), prev]
    out_specs = pl.BlockSpec((1, tq, gw), q_tile(0))
    if mode == "slc":
        in_specs.append(pl.BlockSpec((1, 1, tq, LANES), lambda bi, g, s, sl, qi, *_: (bi, g, qi[s], 0)))
        args.append(sel)
        out_shape = jax.ShapeDtypeStruct((b, t, NSA_WIDTH), jnp.float32)
    else:
        out_specs = [out_specs, pl.BlockSpec((1, tq, LANES), q_tile(0))]
        out_shape = [jax.ShapeDtypeStruct((b, t, NSA_WIDTH), jnp.bfloat16),
                     jax.ShapeDtypeStruct((b, t, NSA_KV_HEADS * LANES), jnp.float32)]
    grid_spec = pltpu.PrefetchScalarGridSpec(
        num_scalar_prefetch=5,
        grid=(b, NSA_KV_HEADS, len(qi)),
        in_specs=in_specs,
        out_specs=out_specs,
        scratch_shapes=[pltpu.VMEM((NSA_REP, tq, AUG_DIM), jnp.bfloat16),
                        pltpu.VMEM((NSA_REP, tq, HEAD_DIM), jnp.float32),
                        pltpu.VMEM((NSA_REP, tq, AUG_DIM), jnp.float32)],
    )
    return pl.pallas_call(
        functools.partial(_flash_kernel, mode=mode, tq=tq, tk=tk),
        grid_spec=grid_spec,
        out_shape=out_shape,
        compiler_params=_cparams(("parallel", "parallel", "arbitrary")),
        name="flash_" + mode,
    )(*args)


def kernel(x, p, mix_norm, w_in, conv_w, q_gain, k_gain, cmp_pos, cmp_w1, cmp_w2, sgu_gain, w_sp,
           b_sp, mix_out_norm, w_out, mlp_norm, w_mlp_in, w_mlp_out, ple_norm, w_ple_proj,
           w_ple_gate):
    b, t, d = x.shape
    m = b * t
    bf = jnp.bfloat16
    depth = p.shape[0]
    p_rows = p.reshape(depth, m, PLE_DIM)
    w_in_t = jnp.swapaxes(w_in, 1, 2)
    ones = jnp.ones((D_FF,), jnp.float32)
    h2 = x.reshape(m, d)
    hb, ssq = stats_cast(h2)
    for i in range(depth):
        g_in = mix_norm[i]
        z_a = matmul(hb, cast_weight_t(w_in_t, i, g_in, 0, Z_A_COLS, transpose=False), ssq=ssq,
                     w_rows_are_outputs=True).reshape(b, t, Z_A_COLS)
        z_g = matmul(hb, cast_weight_t(w_in_t, i, g_in, Z_G_OFF, Z_G_COLS, transpose=False), ssq=ssq,
                     w_rows_are_outputs=True).reshape(b, t, Z_G_COLS)
        w_gate = cast_weight_t(w_in_t, i, g_in, Z_A_COLS, LANES, keep=N_GATES)
        gate_logits = matmul(hb, w_gate, ssq=ssq, tn=LANES).reshape(b, t, LANES)

        on = mix_out_norm[i]
        mixed_a = conv_mixer(z_a, conv_w[i], on[:CONV_WIDTH])
        kc, vc = compress_kv(z_a, cmp_pos[i], cmp_w1[i], cmp_w2[i], k_gain[i])
        o_b, sel, used = cmp_attention(z_a, kc, vc, q_gain[i], gate_logits)
        kv3 = kv_prep(z_a, k_gain[i])
        o_b = flash_attention("slc", z_a, kv3, q_gain[i], gate_logits, o_b, sel=sel, used=used)
        o_b, ssq_b = flash_attention("win", z_a, kv3, q_gain[i], gate_logits, o_b)
        mixed_c = gmlp_mixer(z_g, sgu_gain[i], w_sp[i], b_sp[i], on[CONV_WIDTH + NSA_WIDTH:])
        g_out = jnp.concatenate([ones[:CONV_WIDTH], on[CONV_WIDTH:CONV_WIDTH + NSA_WIDTH],
                                 ones[:GMLP_WIDTH]])
        mixed = (mixed_a.reshape(m, CONV_WIDTH), o_b.reshape(m, NSA_WIDTH),
                 mixed_c.reshape(m, GMLP_WIDTH))
        h2, hb, ssq = matmul(mixed, cast_weight(w_out, i, g_out), mode="residual", res=h2,
                             ssq=(None, ssq_b.reshape(m, NSA_KV_HEADS * LANES), None),
                             stats=True, tn=512)

        hid = matmul(hb, cast_weight(w_mlp_in, i, mlp_norm[i]), mode="relu2", out_dtype=bf, ssq=ssq)
        h2 = matmul(hid, cast_weight(w_mlp_out, i, ones), mode="residual", res=h2)
        hb, ssq = stats_cast(h2)

        out = matmul(hb, cast_weight(w_ple_gate, i, ple_norm[i]), mode="ple", res=h2, p=p_rows,
                     wp=w_ple_proj, layer=i, ssq=ssq, stats=i + 1 < depth, tn=512)
        h2, hb, ssq = out if i + 1 < depth else (out, None, None)
    return h2.reshape(b, t, d)
```

```python
import functools

import jax
import jax.numpy as jnp
import numpy as np
from jax import lax
from jax.experimental import pallas as pl
from jax.experimental.pallas import tpu as pltpu

D_MODEL = 4096
HEAD_DIM = 128
CONV_WIDTH = D_MODEL // 4
CONV_K = 3
NSA_HEADS = (D_MODEL // 2) // HEAD_DIM
NSA_WIDTH = NSA_HEADS * HEAD_DIM
NSA_KV_HEADS = 4
NSA_REP = NSA_HEADS // NSA_KV_HEADS
NSA_BRANCHES = 3
KV_WIDTH = NSA_KV_HEADS * HEAD_DIM
CMP_LEN = 32
CMP_STRIDE = 16
SLC_BLOCK = 64
SLC_SHIFT = 6
N_SELECT = 16
WINDOW = 512
GMLP_WIDTH = D_MODEL // 4
GMLP_GROUPS = GMLP_WIDTH // HEAD_DIM
GMLP_CHUNK = 128
D_FF = 4 * D_MODEL
PLE_DIM = 256
N_GATES = NSA_HEADS * NSA_BRANCHES
EPS = 1e-6
NEG_INF = -1e30
FORCE_SCORE = 1e9
BELOW_NEG_INF = -3e38

Z_A_COLS = 3 * CONV_WIDTH + NSA_WIDTH + 2 * NSA_BRANCHES * KV_WIDTH
Z_G_COLS = 2 * GMLP_WIDTH
Z_G_OFF = Z_A_COLS + N_GATES
Q_OFF = 3 * CONV_WIDTH
KCMP_OFF = Q_OFF + NSA_WIDTH
VCMP_OFF = KCMP_OFF + KV_WIDTH
KV4_OFF = VCMP_OFF + KV_WIDTH

LANES = 128
VMEM_LIMIT_BYTES = 60 * 1024 * 1024


def _cparams(dims):
    return pltpu.CompilerParams(dimension_semantics=dims, vmem_limit_bytes=VMEM_LIMIT_BYTES)


def _rms(x, gain):
    return x * lax.rsqrt(jnp.mean(x * x, axis=-1, keepdims=True) + EPS) * gain


def _gelu(x):
    return jax.nn.gelu(x)


def _row_scale(ssq, width, n_groups=1):
    tot = ssq[:, 0:LANES]
    for gi in range(1, n_groups):
        tot = tot + ssq[:, gi * LANES:(gi + 1) * LANES]
    return lax.rsqrt(tot * (1.0 / width) + EPS)


def _lane_sum_sq(x):
    return jnp.broadcast_to(jnp.sum(x * x, axis=-1, keepdims=True), (x.shape[0], LANES))


def _stats_kernel(x_ref, hb_ref, ssq_ref):
    x = x_ref[...]
    hb_ref[...] = x.astype(hb_ref.dtype)
    ssq_ref[...] = _lane_sum_sq(x)


def stats_cast(x, tm=512):
    m, d = x.shape
    return pl.pallas_call(
        _stats_kernel,
        grid=(m // tm,),
        in_specs=[pl.BlockSpec((tm, d), lambda i: (i, 0))],
        out_specs=[pl.BlockSpec((tm, d), lambda i: (i, 0)),
                   pl.BlockSpec((tm, LANES), lambda i: (i, 0))],
        out_shape=[jax.ShapeDtypeStruct((m, d), jnp.bfloat16),
                   jax.ShapeDtypeStruct((m, LANES), jnp.float32)],
        compiler_params=_cparams(("parallel",)),
        name="stats_cast",
    )(x)


def _cast_kernel(w_ref, g_ref, o_ref):
    o_ref[...] = (w_ref[...] * g_ref[...]).astype(o_ref.dtype)


def _cast_t_kernel(a_ref, *refs, shift, keep, transpose):
    g_ref, o_ref = refs[-2:]
    x = a_ref[...]
    if shift:
        x = jnp.concatenate([x[shift:, :], refs[0][:shift, :]], axis=0)
    if keep < x.shape[0]:
        row = lax.broadcasted_iota(jnp.int32, x.shape, 0)
        x = jnp.where(row < keep, x, 0.0)
    x = x * g_ref[...]
    o_ref[...] = (x.T if transpose else x).astype(o_ref.dtype)


def cast_weight(w, layer, gain, tk=1024, tn=2048):
    _, kdim, n = w.shape
    tk, tn = min(tk, kdim), min(tn, n)
    assert kdim % tk == 0 and n % tn == 0
    return pl.pallas_call(
        _cast_kernel,
        grid=(kdim // tk, n // tn),
        in_specs=[pl.BlockSpec((None, tk, tn), lambda i, j: (layer, i, j)),
                  pl.BlockSpec((tk, 1), lambda i, j: (i, 0))],
        out_specs=pl.BlockSpec((tk, tn), lambda i, j: (i, j)),
        out_shape=jax.ShapeDtypeStruct((kdim, n), jnp.bfloat16),
        compiler_params=_cparams(("parallel", "parallel")),
        name="cast_weight",
    )(w, gain.reshape(kdim, 1))


def cast_weight_t(w_t, layer, gain, row0, nrows, keep=None, transpose=True, tr=512, tc=2048):
    _, n, kdim = w_t.shape
    tr, tc = min(tr, nrows), min(tc, kdim)
    shift = row0 % tr
    base = row0 // tr
    assert nrows % tr == 0 and kdim % tc == 0 and shift % 8 == 0
    in_specs = [pl.BlockSpec((None, tr, tc), lambda i, j: (layer, base + j, i))]
    if shift:
        head = 8 * pl.cdiv(shift, 8)
        while tr % head:
            head += 8
        in_specs.append(pl.BlockSpec((None, head, tc), lambda i, j: (layer, (base + j + 1) * (tr // head), i)))
    return pl.pallas_call(
        functools.partial(_cast_t_kernel, shift=shift, keep=keep or nrows, transpose=transpose),
        grid=(kdim // tc, nrows // tr),
        in_specs=in_specs + [pl.BlockSpec((1, tc), lambda i, j: (0, i))],
        out_specs=(pl.BlockSpec((tc, tr), lambda i, j: (i, j)) if transpose
                   else pl.BlockSpec((tr, tc), lambda i, j: (j, i))),
        out_shape=jax.ShapeDtypeStruct((kdim, nrows) if transpose else (nrows, kdim), jnp.bfloat16),
        compiler_params=_cparams(("parallel", "parallel")),
        name="cast_weight_t",
    )(*([w_t] * (2 if shift else 1)), gain.reshape(1, kdim))


def _mm_kernel(*refs, mode, nk, n_x, scale_groups, stats, w_rows_are_outputs):
    x_refs, refs = refs[:n_x], refs[n_x:]
    refs = list(refs)
    take = lambda cond: refs.pop(0) if cond else None
    w_ref = refs.pop(0)
    res_ref = take(mode in ("residual", "ple"))
    p_ref, wp_ref = take(mode == "ple"), take(mode == "ple")
    ssq_refs = [take(g > 0) for g in scale_groups]
    o_ref = refs.pop(0)
    hb_ref, ssq_out_ref = take(stats), take(stats)
    tn = o_ref.shape[1]
    j = pl.program_id(1)
    k = pl.program_id(2)

    def scaled(d, n):
        if ssq_refs[n] is None:
            return d
        s = _row_scale(ssq_refs[n][...], x_refs[n].shape[1] * (nk if len(x_refs) == 1 else 1),
                       scale_groups[n])
        return d * jnp.tile(s, (1, tn // LANES))

    def finish(new):
        o_ref[...] = new
        if stats:
            hb_ref[...] = new.astype(hb_ref.dtype)
            ssq = _lane_sum_sq(new)

            @pl.when(j == 0)
            def _():
                ssq_out_ref[...] = ssq

            @pl.when(j > 0)
            def _():
                ssq_out_ref[...] += ssq

    def epilogue(acc):
        if mode == "plain":
            o_ref[...] = acc.astype(o_ref.dtype)
        elif mode == "relu2":
            r = jnp.maximum(acc, 0.0)
            o_ref[...] = (r * r).astype(o_ref.dtype)
        elif mode == "residual":
            finish(res_ref[...] + acc)
        else:
            proj = jnp.dot(p_ref[...].astype(jnp.bfloat16), wp_ref[...].astype(jnp.bfloat16),
                           preferred_element_type=jnp.float32)
            finish(res_ref[...] + jax.nn.sigmoid(acc) * proj)

    def product():
        part, off = None, 0
        for n, x_ref in enumerate(x_refs):
            kx = x_ref.shape[1]
            if w_rows_are_outputs:
                d = lax.dot_general(x_ref[...], w_ref[:, off:off + kx], (((1,), (1,)), ((), ())),
                                    preferred_element_type=jnp.float32)
            else:
                d = jnp.dot(x_ref[...], w_ref[off:off + kx, :], preferred_element_type=jnp.float32)
            d = scaled(d, n)
            part = d if part is None else part + d
            off += kx
        return part

    if nk == 1:
        epilogue(product())
        return
    if mode == "residual":
        @pl.when(k == 0)
        def _():
            o_ref[...] = res_ref[...] + product()

        @pl.when(jnp.logical_and(k > 0, k < nk - 1))
        def _():
            o_ref[...] += product()

        @pl.when(k == nk - 1)
        def _():
            finish(o_ref[...] + product())
        return
    acc_ref, = refs

    @pl.when(k == 0)
    def _():
        acc_ref[...] = product()

    @pl.when(jnp.logical_and(k > 0, k < nk - 1))
    def _():
        acc_ref[...] += product()

    @pl.when(k == nk - 1)
    def _():
        epilogue(acc_ref[...] + product())


def matmul(x, w, mode="plain", out_dtype=jnp.float32, res=None, p=None, wp=None, layer=0,
           ssq=None, stats=False, w_rows_are_outputs=False, tm=1024, tn=1024, tk=4096):
    xs = x if isinstance(x, (tuple, list)) else (x,)
    ssqs = ssq if isinstance(ssq, (tuple, list)) else (ssq,) * len(xs)
    m = xs[0].shape[0]
    kdim, n = w.shape[::-1] if w_rows_are_outputs else w.shape
    assert sum(xi.shape[1] for xi in xs) == kdim and len(ssqs) == len(xs)
    tm, tn, tk = min(tm, m), min(tn, n), min(tk, kdim)
    nk = kdim // tk
    assert m % tm == 0 and n % tn == 0 and kdim % tk == 0 and (nk == 1 or len(xs) == 1)
    assert not stats or mode in ("residual", "ple")
    in_specs = [pl.BlockSpec((tm, xi.shape[1] if len(xs) > 1 else tk), lambda i, j, k: (i, k))
                for xi in xs]
    in_specs.append(pl.BlockSpec((tn, tk), lambda i, j, k: (j, k)) if w_rows_are_outputs
                    else pl.BlockSpec((tk, tn), lambda i, j, k: (k, j)))
    args = [*xs, w]
    if mode in ("residual", "ple"):
        in_specs.append(pl.BlockSpec((tm, tn), lambda i, j, k: (i, j)))
        args.append(res)
    if mode == "ple":
        pd = p.shape[2]
        in_specs += [pl.BlockSpec((None, tm, pd), lambda i, j, k: (layer, i, 0)),
                     pl.BlockSpec((None, pd, tn), lambda i, j, k: (layer, 0, j))]
        args += [p, wp]
    scale_groups = tuple(0 if s is None else s.shape[1] // LANES for s in ssqs)
    for s in ssqs:
        if s is not None:
            in_specs.append(pl.BlockSpec((tm, s.shape[1]), lambda i, j, k: (i, 0)))
            args.append(s)
    tile = pl.BlockSpec((tm, tn), lambda i, j, k: (i, j))
    out_specs, out_shape = tile, jax.ShapeDtypeStruct((m, n), out_dtype)
    if stats:
        out_specs = [tile, tile, pl.BlockSpec((tm, LANES), lambda i, j, k: (i, 0))]
        out_shape = [out_shape, jax.ShapeDtypeStruct((m, n), jnp.bfloat16),
                     jax.ShapeDtypeStruct((m, LANES), jnp.float32)]
    scratch = [pltpu.VMEM((tm, tn), jnp.float32)] if nk > 1 and mode != "residual" else []
    return pl.pallas_call(
        functools.partial(_mm_kernel, mode=mode, nk=nk, n_x=len(xs), scale_groups=scale_groups,
                          stats=stats, w_rows_are_outputs=w_rows_are_outputs),
        grid=(m // tm, n // tn, nk),
        in_specs=in_specs,
        out_specs=out_specs,
        out_shape=out_shape,
        scratch_shapes=scratch,
        compiler_params=_cparams(("parallel", "arbitrary" if stats else "parallel", "arbitrary")),
        name="matmul_" + mode,
    )(*args)


HALO_ROWS = 8


def _conv_kernel(xa_ref, gb_ref, gc_ref, xap_ref, gcp_ref, cw_ref, on_ref, o_ref):
    i = pl.program_id(1)
    inner = gc_ref[0] * xa_ref[0]
    prev = gcp_ref[0] * xap_ref[0]
    prev = jnp.where(i > 0, prev, 0.0)
    tt = inner.shape[0]
    row = lax.broadcasted_iota(jnp.int32, inner.shape, 0)
    m1 = jnp.where(row == 0, prev[HALO_ROWS - 1:HALO_ROWS], pltpu.roll(inner, 1, 0))
    m2 = pltpu.roll(inner, 2, 0)
    m2 = jnp.where(row == 0, prev[HALO_ROWS - 2:HALO_ROWS - 1], m2)
    m2 = jnp.where(row == 1, prev[HALO_ROWS - 1:HALO_ROWS], m2)
    cw = cw_ref[...]
    conv = cw[0:1] * m2 + cw[1:2] * m1 + cw[2:3] * inner
    o_ref[0] = _rms(gb_ref[0] * conv, on_ref[...]).astype(o_ref.dtype)


def conv_mixer(z3, conv_w, on_gain, tt=1024):
    b, t, _ = z3.shape
    c = CONV_WIDTH
    hb = tt // HALO_ROWS
    cur = lambda col: pl.BlockSpec((1, tt, c), lambda bi, i: (bi, i, col))
    halo = lambda col: pl.BlockSpec((1, HALO_ROWS, c),
                                    lambda bi, i: (bi, jnp.maximum(i * hb - 1, 0), col))
    return pl.pallas_call(
        _conv_kernel,
        grid=(b, t // tt),
        in_specs=[cur(0), cur(1), cur(2), halo(0), halo(2),
                  pl.BlockSpec((HALO_ROWS, c), lambda bi, i: (0, 0)),
                  pl.BlockSpec((1, c), lambda bi, i: (0, 0))],
        out_specs=pl.BlockSpec((1, tt, c), lambda bi, i: (bi, i, 0)),
        out_shape=jax.ShapeDtypeStruct((b, t, c), jnp.bfloat16),
        compiler_params=_cparams(("parallel", "parallel")),
        name="conv_mixer",
    )(z3, z3, z3, z3, z3,
      jnp.pad(conv_w, ((0, HALO_ROWS - CONV_K), (0, 0))), on_gain.reshape(1, c))


def _gmlp_kernel(gu_ref, gv_ref, sg_ref, wsp_ref, bexp_ref, on_ref, o_ref, *, n_chunks):
    u = _gelu(gu_ref[0])
    v = _rms(_gelu(gv_ref[0]), sg_ref[...]).astype(jnp.bfloat16)
    ck = GMLP_CHUNK
    r_i = lax.broadcasted_iota(jnp.int32, (ck, ck), 0)
    c_i = lax.broadcasted_iota(jnp.int32, (ck, ck), 1)
    bexp = bexp_ref[...]
    rows = []
    for c in range(n_chunks):
        cols = []
        for g in range(GMLP_GROUPS):
            w = jnp.where(r_i >= c_i, wsp_ref[g], 0.0).astype(jnp.bfloat16)
            vg = v[c * ck:(c + 1) * ck, g * HEAD_DIM:(g + 1) * HEAD_DIM]
            cols.append(jnp.dot(w, vg, preferred_element_type=jnp.float32))
        rows.append(jnp.concatenate(cols, axis=1) + bexp)
    spatial = jnp.concatenate(rows, axis=0) if n_chunks > 1 else rows[0]
    o_ref[0] = _rms(u * spatial, on_ref[...]).astype(o_ref.dtype)


def gmlp_mixer(z3, sgu_gain, w_sp, b_sp, on_gain, tt=512):
    b, t, _ = z3.shape
    c = GMLP_WIDTH
    bexp = jnp.repeat(b_sp.T, HEAD_DIM, axis=1)
    const2 = lambda shape: pl.BlockSpec(shape, lambda bi, i: (0,) * len(shape))
    return pl.pallas_call(
        functools.partial(_gmlp_kernel, n_chunks=tt // GMLP_CHUNK),
        grid=(b, t // tt),
        in_specs=[pl.BlockSpec((1, tt, c), lambda bi, i: (bi, i, 0)),
                  pl.BlockSpec((1, tt, c), lambda bi, i: (bi, i, 1)),
                  const2((1, c)), const2((GMLP_GROUPS, GMLP_CHUNK, GMLP_CHUNK)),
                  const2((GMLP_CHUNK, c)), const2((1, c))],
        out_specs=pl.BlockSpec((1, tt, c), lambda bi, i: (bi, i, 0)),
        out_shape=jax.ShapeDtypeStruct((b, t, c), jnp.bfloat16),
        compiler_params=_cparams(("parallel", "parallel")),
        name="gmlp_mixer",
    )(z3, z3, sgu_gain.reshape(1, c), w_sp, bexp, on_gain.reshape(1, c))


ATT_TILE = 512
KV_TILE_SHIFT = 9
AUG_DIM = 2 * HEAD_DIM
POS_LANE = 64
ONE_LANE = 70
FEAT_END = 73
MASK_BIG = 2.0 ** 100
MASKED = -MASK_BIG
M_INIT = NEG_INF
LOG2E = 1.4426950408889634


def _kvprep_kernel(z_ref, kg_ref, o_ref, *, tt):
    i = pl.program_id(1)
    z = z_ref[0]
    pos = i * tt + lax.broadcasted_iota(jnp.int32, (tt, HEAD_DIM), 0)
    lane = lax.broadcasted_iota(jnp.int32, (tt, HEAD_DIM), 1)
    blk = lax.shift_right_logical(pos, SLC_SHIFT)
    hi = lax.shift_left(blk, SLC_SHIFT).astype(jnp.float32)
    lo = jnp.bitwise_and(pos, SLC_BLOCK - 1).astype(jnp.float32)
    posf = jnp.where(jnp.bitwise_and(lane, 1) == 0, hi, lo)
    tail = jnp.where(lane < ONE_LANE, posf, jnp.where(lane < FEAT_END, 1.0, 0.0))
    onehot = jnp.where(lane == blk, MASK_BIG, 0.0)
    kfeat = (jnp.where(lane < POS_LANE, onehot, tail).astype(o_ref.dtype),
             jnp.where(lane < POS_LANE, 0.0, tail).astype(o_ref.dtype))
    vfeat = jnp.ones((tt, HEAD_DIM), o_ref.dtype)
    for part in range(4):
        for g in range(NSA_KV_HEADS):
            src = part * KV_WIDTH + g * HEAD_DIM
            dst = (part * NSA_KV_HEADS + g) * AUG_DIM
            x = z[:, src:src + HEAD_DIM]
            if part % 2 == 0:
                x = _rms(x, kg_ref[1 + part // 2:2 + part // 2, :])
            o_ref[0, :, dst:dst + HEAD_DIM] = x.astype(o_ref.dtype)
            o_ref[0, :, dst + HEAD_DIM:dst + AUG_DIM] = kfeat[part // 2] if part % 2 == 0 else vfeat


def kv_prep(z3, k_gain, tt=512):
    b, t, _ = z3.shape
    w = 4 * KV_WIDTH
    wo = 4 * NSA_KV_HEADS * AUG_DIM
    kg = jnp.pad(k_gain, ((0, 8 - NSA_BRANCHES), (0, 0)))
    return pl.pallas_call(
        functools.partial(_kvprep_kernel, tt=tt),
        grid=(b, t // tt),
        in_specs=[pl.BlockSpec((1, tt, w), lambda bi, i: (bi, i, KV4_OFF // w)),
                  pl.BlockSpec((8, HEAD_DIM), lambda bi, i: (0, 0))],
        out_specs=pl.BlockSpec((1, tt, wo), lambda bi, i: (bi, i, 0)),
        out_shape=jax.ShapeDtypeStruct((b, t, wo), jnp.bfloat16),
        compiler_params=_cparams(("parallel", "parallel")),
        name="kv_prep",
    )(z3, kg)


def _compress_kernel(k_ref, v_ref, w1_ref, w2_ref, pe_ref, kg_ref, kc_ref, vc_ref, *, n_chunk):
    def one(src_ref, which):
        ya = jnp.zeros((n_chunk, HEAD_DIM), jnp.float32)
        yb = jnp.zeros((n_chunk, HEAD_DIM), jnp.float32)
        bias = jnp.zeros((8, HEAD_DIM), jnp.float32)
        for l in range(CMP_STRIDE):
            xl = src_ref[0, pl.ds(l, n_chunk, stride=CMP_STRIDE), :].astype(jnp.bfloat16)
            ya += jnp.dot(xl, w1_ref[which, l], preferred_element_type=jnp.float32)
            yb += jnp.dot(xl, w1_ref[which, CMP_STRIDE + l], preferred_element_type=jnp.float32)
        for l in range(CMP_LEN):
            pe_l = jnp.broadcast_to(pe_ref[which, l:l + 1, :], (8, HEAD_DIM)).astype(jnp.bfloat16)
            bias += jnp.dot(pe_l, w1_ref[which, l], preferred_element_type=jnp.float32)
        hid = _gelu(ya + pltpu.roll(yb, n_chunk - 1, 0) + bias[0:1])
        return jnp.dot(hid.astype(jnp.bfloat16), w2_ref[which], preferred_element_type=jnp.float32)

    kc_ref[0, 0] = _rms(one(k_ref, 0), kg_ref[0:1, :]).astype(kc_ref.dtype)
    vc_ref[0, 0] = one(v_ref, 1).astype(vc_ref.dtype)


def compress_kv(z3, cmp_pos, cmp_w1, cmp_w2, k_gain):
    b, t, _ = z3.shape
    n_chunk = t // CMP_STRIDE
    kg = jnp.pad(k_gain, ((0, 8 - NSA_BRANCHES), (0, 0)))
    const = lambda shape: pl.BlockSpec(shape, lambda bi, g: (0,) * len(shape))
    out = jax.ShapeDtypeStruct((b, NSA_KV_HEADS, n_chunk, HEAD_DIM), jnp.bfloat16)
    return pl.pallas_call(
        functools.partial(_compress_kernel, n_chunk=n_chunk),
        grid=(b, NSA_KV_HEADS),
        in_specs=[pl.BlockSpec((1, t, HEAD_DIM), lambda bi, g: (bi, 0, KCMP_OFF // HEAD_DIM + g)),
                  pl.BlockSpec((1, t, HEAD_DIM), lambda bi, g: (bi, 0, VCMP_OFF // HEAD_DIM + g)),
                  const((2, CMP_LEN, HEAD_DIM, HEAD_DIM)), const((2, HEAD_DIM, HEAD_DIM)),
                  const((2, CMP_LEN, HEAD_DIM)), const((8, HEAD_DIM))],
        out_specs=[pl.BlockSpec((1, 1, n_chunk, HEAD_DIM), lambda bi, g: (bi, g, 0, 0))] * 2,
        out_shape=[out, out],
        compiler_params=_cparams(("parallel", "parallel")),
        name="compress_kv",
    )(z3, z3, cmp_w1.astype(jnp.bfloat16), cmp_w2.astype(jnp.bfloat16), cmp_pos, kg)


def _alibi_slopes():
    h = np.arange(1, NSA_HEADS + 1, dtype=np.float32)
    return np.power(np.float32(2.0), -8.0 * h / NSA_HEADS).astype(np.float32)


def _overlap_matrix(n_chunk, n_slc):
    n_cmp = n_chunk - CMP_LEN // CMP_STRIDE + 1
    jc = np.arange(n_cmp)[:, None] * CMP_STRIDE
    bs = np.arange(n_slc)[None, :] * SLC_BLOCK
    ov = np.clip(np.minimum(jc + CMP_LEN, bs + SLC_BLOCK) - np.maximum(jc, bs), 0, None)
    out = np.zeros((n_chunk, LANES), np.float32)
    out[:n_cmp, :n_slc] = ov.astype(np.float32) / CMP_LEN
    return out


def _q_heads(q_ref, qg_ref):
    q = q_ref[0]
    scale = HEAD_DIM ** -0.5
    return [(_rms(q[:, r * HEAD_DIM:(r + 1) * HEAD_DIM], qg_ref[...]) * scale).astype(jnp.bfloat16)
            for r in range(NSA_REP)]


def _gate_expand_matrix():
    m = np.zeros((NSA_BRANCHES, NSA_KV_HEADS, LANES, NSA_REP * HEAD_DIM), np.float32)
    for br in range(NSA_BRANCHES):
        for g in range(NSA_KV_HEADS):
            for r in range(NSA_REP):
                m[br, g, (g * NSA_REP + r) * NSA_BRANCHES + br, r * HEAD_DIM:(r + 1) * HEAD_DIM] = 1.0
    return m


def _branch_gate(gl_ref, ex_ref):
    sig = jax.nn.sigmoid(gl_ref[0])
    hi = sig.astype(jnp.bfloat16)
    lo = (sig - hi.astype(jnp.float32)).astype(jnp.bfloat16)
    ex = ex_ref[...]
    return (jnp.dot(hi, ex, preferred_element_type=jnp.float32)
            + jnp.dot(lo, ex, preferred_element_type=jnp.float32))


def _cmp_kernel(slopes_ref, q_ref, kc_ref, vc_ref, qg_ref, ov_ref, gl_ref, ex_ref, o_ref, sel_ref,
                used_ref, *, tq, n_chunk, n_slc):
    g = pl.program_id(1)
    i = pl.program_id(2)
    gate = _branch_gate(gl_ref, ex_ref)
    n_cmp = n_chunk - CMP_LEN // CMP_STRIDE + 1
    qh = _q_heads(q_ref, qg_ref)
    kc = kc_ref[0, 0]
    vc = vc_ref[0, 0]
    t = i * tq + lax.broadcasted_iota(jnp.int32, (tq, n_chunk), 0)
    n = lax.broadcasted_iota(jnp.int32, (tq, n_chunk), 1)
    dist = t - (n * CMP_STRIDE + CMP_LEN - 1)
    mask = jnp.logical_and(dist >= 0, n < n_cmp)
    distf = dist.astype(jnp.float32)
    psum = jnp.zeros((tq, n_chunk), jnp.float32)
    for r in range(NSA_REP):
        s = lax.dot_general(qh[r], kc, (((1,), (1,)), ((), ())),
                            preferred_element_type=jnp.float32)
        s = jnp.where(mask, s - slopes_ref[g * NSA_REP + r] * distf, NEG_INF)
        mx = jnp.max(s, axis=-1, keepdims=True)
        e = jnp.where(mask, jnp.exp(s - mx), 0.0)
        den = jnp.sum(e, axis=-1, keepdims=True)
        p = e / jnp.where(den > 0.0, den, 1.0)
        psum += p
        o_ref[0, :, r * HEAD_DIM:(r + 1) * HEAD_DIM] = gate[:, r * HEAD_DIM:(r + 1) * HEAD_DIM] * jnp.dot(
            p.astype(jnp.bfloat16), vc, preferred_element_type=jnp.float32)

    ov = ov_ref[...]
    imp = jnp.zeros((tq, LANES), jnp.float32)
    rem = psum
    for _ in range(3):
        part = rem.astype(jnp.bfloat16)
        imp += jnp.dot(part, ov, preferred_element_type=jnp.float32)
        rem = rem - part.astype(jnp.float32)

    tb = i * tq + lax.broadcasted_iota(jnp.int32, (tq, LANES), 0)
    blk = lax.broadcasted_iota(jnp.int32, (tq, LANES), 1)
    qblk = lax.shift_right_logical(tb, SLC_SHIFT)
    forced = (blk == 0) | (blk == qblk) | (blk == qblk - 1)
    score = jnp.where(forced, FORCE_SCORE, jnp.where(blk <= qblk, imp, NEG_INF))
    score = jnp.where(blk < n_slc, score, BELOW_NEG_INF)
    st = score.T[0:n_slc]
    n_tiles = n_slc // 8
    tiles = [st[8 * j:8 * j + 8] for j in range(n_tiles)]
    sub8 = lax.broadcasted_iota(jnp.int32, (8, tq), 0)
    cnt = [jnp.zeros((8, tq), jnp.float32) for _ in range(n_tiles)]
    for m in range(n_slc):
        row = st[m:m + 1, :]
        own, off = divmod(m, 8)
        for j in range(n_tiles):
            if j < own:
                inc = jnp.where(row > tiles[j], 1.0, 0.0)
            elif j > own:
                inc = jnp.where(row >= tiles[j], 1.0, 0.0)
            else:
                later = jnp.where(sub8 > off, 1.0, 0.0)
                inc = (jnp.where(row >= tiles[j], later, 0.0)
                       + jnp.where(row > tiles[j], 1.0 - later, 0.0))
            cnt[j] = cnt[j] + inc
    keep = float(min(N_SELECT, n_slc))
    sel_t = jnp.concatenate([jnp.where(c < keep, 1.0, 0.0) for c in cnt]
                            + [jnp.zeros((LANES - n_slc, tq), jnp.float32)] * (n_slc < LANES), axis=0)
    sel = sel_t.T
    sel_ref[0, 0] = sel.astype(sel_ref.dtype)
    picked = jnp.broadcast_to(jnp.max(sel, axis=0, keepdims=True), (8, LANES)).astype(jnp.bfloat16)
    blk_i = lax.broadcasted_iota(jnp.int32, (LANES, LANES), 0)
    tile_i = lax.broadcasted_iota(jnp.int32, (LANES, LANES), 1)
    group = jnp.where(lax.shift_right_logical(blk_i, KV_TILE_SHIFT - SLC_SHIFT) == tile_i, 1.0, 0.0)
    used = jnp.dot(picked, group.astype(jnp.bfloat16), preferred_element_type=jnp.float32)
    used_ref[0, 0] = jnp.where(used > 0.5, 1, 0).astype(jnp.int32)


def cmp_attention(z3, kc, vc, q_gain, gate_logits):
    b, t, _ = z3.shape
    tq = min(ATT_TILE, t)
    n_chunk = t // CMP_STRIDE
    n_slc = t // SLC_BLOCK
    assert n_slc <= LANES
    gw = NSA_REP * HEAD_DIM
    grid_spec = pltpu.PrefetchScalarGridSpec(
        num_scalar_prefetch=1,
        grid=(b, NSA_KV_HEADS, t // tq),
        in_specs=[pl.BlockSpec((1, tq, gw), lambda bi, g, i, s: (bi, i, Q_OFF // gw + g)),
                  pl.BlockSpec((1, 1, n_chunk, HEAD_DIM), lambda bi, g, i, s: (bi, g, 0, 0)),
                  pl.BlockSpec((1, 1, n_chunk, HEAD_DIM), lambda bi, g, i, s: (bi, g, 0, 0)),
                  pl.BlockSpec((1, HEAD_DIM), lambda bi, g, i, s: (0, 0)),
                  pl.BlockSpec((n_chunk, LANES), lambda bi, g, i, s: (0, 0)),
                  pl.BlockSpec((1, tq, LANES), lambda bi, g, i, s: (bi, i, 0)),
                  pl.BlockSpec((None, None, LANES, gw), lambda bi, g, i, s: (0, g, 0, 0))],
        out_specs=[pl.BlockSpec((1, tq, gw), lambda bi, g, i, s: (bi, i, g)),
                   pl.BlockSpec((1, 1, tq, LANES), lambda bi, g, i, s: (bi, g, i, 0)),
                   pl.BlockSpec((1, 1, 8, LANES), lambda bi, g, i, s: (bi, g, i, 0))],
    )
    return pl.pallas_call(
        functools.partial(_cmp_kernel, tq=tq, n_chunk=n_chunk, n_slc=n_slc),
        grid_spec=grid_spec,
        out_shape=[jax.ShapeDtypeStruct((b, t, NSA_WIDTH), jnp.float32),
                   jax.ShapeDtypeStruct((b, NSA_KV_HEADS, t, LANES), jnp.bfloat16),
                   jax.ShapeDtypeStruct((b, NSA_KV_HEADS, t // tq * 8, LANES), jnp.int32)],
        compiler_params=_cparams(("parallel", "parallel", "parallel")),
        name="cmp_attention",
    )(jnp.asarray(_alibi_slopes()), z3, kc, vc, q_gain.reshape(1, HEAD_DIM),
      jnp.asarray(_overlap_matrix(n_chunk, n_slc), dtype=jnp.bfloat16), gate_logits,
      jnp.asarray(_gate_expand_matrix(), dtype=jnp.bfloat16))


STEP_FIRST, STEP_LAST, STEP_MASKED = 1, 2, 4


def _flash_steps(mode, t, tq, tk):
    qi, kv, fl = [], [], []
    for i in range(t // tq):
        lo, hi = i * tq, i * tq + tq - 1
        first_key = 0 if mode == "slc" else max(lo - WINDOW + 1, 0)
        tiles = list(range(first_key // tk, hi // tk + 1))
        for n, j in enumerate(tiles):
            k_lo, k_hi = j * tk, j * tk + tk - 1
            causal_ok = k_hi <= lo
            window_ok = mode == "slc" or hi - k_lo < WINDOW
            flag = (STEP_FIRST if n == 0 else 0) | (STEP_LAST if n == len(tiles) - 1 else 0)
            flag |= 0 if (causal_ok and window_ok) else STEP_MASKED
            qi.append(i), kv.append(j), fl.append(flag)
    return [np.asarray(a, np.int32) for a in (qi, kv, fl)]


def _split3(x):
    a = x.astype(jnp.bfloat16).astype(jnp.float32)
    b = (x - a).astype(jnp.bfloat16).astype(jnp.float32)
    c = (x - a - b).astype(jnp.bfloat16).astype(jnp.float32)
    return a, b, c


def _flash_kernel(slopes_ref, qi_ref, kv_ref, fl_ref, used_ref, *refs, mode, tq, tk):
    if mode == "slc":
        q_ref, k_ref, v_ref, qg_ref, gl_ref, ex_ref, prev_ref, sel_ref, o_ref, q_sc, m_sc, acc_sc = refs
    else:
        q_ref, k_ref, v_ref, qg_ref, gl_ref, ex_ref, prev_ref, o_ref, ssq_ref, q_sc, m_sc, acc_sc = refs
    g = pl.program_id(1)
    s = pl.program_id(2)
    i = qi_ref[s]
    kv_idx = kv_ref[s]
    flags = fl_ref[s]

    @pl.when(jnp.bitwise_and(flags, STEP_FIRST) != 0)
    def _():
        q = q_ref[0]
        lane = lax.broadcasted_iota(jnp.int32, (tq, HEAD_DIM), 1)
        if mode == "slc":
            sel_m1 = sel_ref[0, 0].astype(jnp.float32) - 1.0
        else:
            sel_m1 = jnp.zeros((tq, HEAD_DIM), jnp.float32)
        t0 = (i * tq).astype(jnp.float32)
        for r in range(NSA_REP):
            qn = _rms(q[:, r * HEAD_DIM:(r + 1) * HEAD_DIM], qg_ref[...]) * (HEAD_DIM ** -0.5 * LOG2E)
            slope2 = slopes_ref[g * NSA_REP + r] * LOG2E
            a, b, c = _split3(jnp.where(lane < ONE_LANE, slope2, -slope2 * t0))
            k3 = jnp.where(lane < ONE_LANE, lax.shift_right_logical(lane - POS_LANE, 1), lane - ONE_LANE)
            term = jnp.where(k3 == 0, a, jnp.where(k3 == 1, b, c))
            feat = jnp.where(lane < POS_LANE, sel_m1, jnp.where(lane < FEAT_END, term, 0.0))
            q_sc[r, :, 0:HEAD_DIM] = qn.astype(q_sc.dtype)
            q_sc[r, :, HEAD_DIM:AUG_DIM] = feat.astype(q_sc.dtype)
        m_sc[...] = jnp.full(m_sc.shape, M_INIT, jnp.float32)
        acc_sc[...] = jnp.zeros(acc_sc.shape, jnp.float32)

    def step(masked):
        k = k_ref[0]
        v = v_ref[0]
        if masked:
            t = i * tq + lax.broadcasted_iota(jnp.int32, (tq, tk), 0)
            dist = t - (kv_idx * tk + lax.broadcasted_iota(jnp.int32, (tq, tk), 1))
            ok = dist >= 0 if mode == "slc" else jnp.logical_and(dist >= 0, dist < WINDOW)
        for r in range(NSA_REP):
            s2 = lax.dot_general(q_sc[r], k, (((1,), (1,)), ((), ())),
                                 preferred_element_type=jnp.float32)
            if masked:
                s2 = jnp.where(ok, s2, MASKED)
            m_prev = m_sc[r]
            m_new = jnp.maximum(m_prev, jnp.max(s2, axis=-1, keepdims=True))
            p = jnp.exp2(s2 - jnp.tile(m_new, (1, tk // HEAD_DIM))).astype(jnp.bfloat16)
            alpha = jnp.exp2(m_prev - m_new)
            acc_sc[r] = jnp.tile(alpha, (1, AUG_DIM // HEAD_DIM)) * acc_sc[r] + jnp.dot(
                p, v, preferred_element_type=jnp.float32)
            m_sc[r] = m_new

    is_masked = jnp.bitwise_and(flags, STEP_MASKED) != 0
    if mode == "slc":
        bg = pl.program_id(0) * NSA_KV_HEADS + g
        live = used_ref[(bg * pl.num_programs(2) + s)] != 0
    else:
        live = True
    pl.when(jnp.logical_and(live, jnp.logical_not(is_masked)))(functools.partial(step, False))
    pl.when(jnp.logical_and(live, is_masked))(functools.partial(step, True))

    @pl.when(jnp.bitwise_and(flags, STEP_LAST) != 0)
    def _():
        gate = _branch_gate(gl_ref, ex_ref)
        prev = prev_ref[0]
        ssq = jnp.zeros((tq, LANES), jnp.float32)
        for r in range(NSA_REP):
            acc = acc_sc[r]
            cols = slice(r * HEAD_DIM, (r + 1) * HEAD_DIM)
            tot = prev[:, cols] + gate[:, cols] * (acc[:, 0:HEAD_DIM] / acc[:, HEAD_DIM:AUG_DIM])
            o_ref[0, :, cols] = tot.astype(o_ref.dtype)
            ssq = ssq + _lane_sum_sq(tot)
        if mode == "win":
            ssq_ref[0] = ssq


def flash_attention(mode, z3, kv3, q_gain, gate_logits, prev, sel=None, used=None):
    b, t, _ = z3.shape
    if mode == "slc":
        k_col, v_col, branch = 0, NSA_KV_HEADS, 1
    else:
        k_col, v_col, branch = 2 * NSA_KV_HEADS, 3 * NSA_KV_HEADS, 2
    tq = tk = min(ATT_TILE, t)
    gw = NSA_REP * HEAD_DIM
    qi, kv, fl = _flash_steps(mode, t, tq, tk)
    if mode == "slc":
        used = used[:, :, ::8, :][:, :, qi, kv].reshape(-1)
    else:
        used = jnp.ones((1,), jnp.int32)

    q_tile = lambda col: (lambda bi, g, s, sl, qi, *_: (bi, qi[s], col + g))
    kv_tile = lambda col: (lambda bi, g, s, sl, qi, kv, *_: (bi, kv[s], col + g))
    in_specs = [pl.BlockSpec((1, tq, gw), q_tile(Q_OFF // gw)),
                pl.BlockSpec((1, tk, AUG_DIM), kv_tile(k_col)),
                pl.BlockSpec((1, tk, AUG_DIM), kv_tile(v_col)),
                pl.BlockSpec((1, HEAD_DIM), lambda bi, g, s, *_: (0, 0)),
                pl.BlockSpec((1, tq, LANES), lambda bi, g, s, sl, qi, *_: (bi, qi[s], 0)),
                pl.BlockSpec((None, None, LANES, gw), lambda bi, g, s, *_: (branch, g, 0, 0)),
                pl.BlockSpec((1, tq, gw), q_tile(0))]
    args = [jnp.asarray(_alibi_slopes()), jnp.asarray(qi), jnp.asarray(kv), jnp.asarray(fl), used,
            z3, kv3, kv3, q_gain.reshape(1, HEAD_DIM), gate_logits,
            jnp.asarray(_gate_expand_matrix(), dtype=jnp.bfloat16), prev]
    out_specs = pl.BlockSpec((1, tq, gw), q_tile(0))
    if mode == "slc":
        in_specs.append(pl.BlockSpec((1, 1, tq, LANES), lambda bi, g, s, sl, qi, *_: (bi, g, qi[s], 0)))
        args.append(sel)
        out_shape = jax.ShapeDtypeStruct((b, t, NSA_WIDTH), jnp.float32)
    else:
        out_specs = [out_specs, pl.BlockSpec((1, tq, LANES), q_tile(0))]
        out_shape = [jax.ShapeDtypeStruct((b, t, NSA_WIDTH), jnp.bfloat16),
                     jax.ShapeDtypeStruct((b, t, NSA_KV_HEADS * LANES), jnp.float32)]
    grid_spec = pltpu.PrefetchScalarGridSpec(
        num_scalar_prefetch=5,
        grid=(b, NSA_KV_HEADS, len(qi)),
        in_specs=in_specs,
        out_specs=out_specs,
        scratch_shapes=[pltpu.VMEM((NSA_REP, tq, AUG_DIM), jnp.bfloat16),
                        pltpu.VMEM((NSA_REP, tq, HEAD_DIM), jnp.float32),
                        pltpu.VMEM((NSA_REP, tq, AUG_DIM), jnp.float32)],
    )
    return pl.pallas_call(
        functools.partial(_flash_kernel, mode=mode, tq=tq, tk=tk),
        grid_spec=grid_spec,
        out_shape=out_shape,
        compiler_params=_cparams(("parallel", "parallel", "arbitrary")),
        name="flash_" + mode,
    )(*args)


def kernel(x, p, mix_norm, w_in, conv_w, q_gain, k_gain, cmp_pos, cmp_w1, cmp_w2, sgu_gain, w_sp,
           b_sp, mix_out_norm, w_out, mlp_norm, w_mlp_in, w_mlp_out, ple_norm, w_ple_proj,
           w_ple_gate):
    b, t, d = x.shape
    m = b * t
    bf = jnp.bfloat16
    depth = p.shape[0]
    p_rows = p.reshape(depth, m, PLE_DIM)
    w_in_t = jnp.swapaxes(w_in, 1, 2)
    ones = jnp.ones((D_FF,), jnp.float32)
    h2 = x.reshape(m, d)
    hb, ssq = stats_cast(h2)
    for i in range(depth):
        g_in = mix_norm[i]
        z_a = matmul(hb, cast_weight_t(w_in_t, i, g_in, 0, Z_A_COLS, transpose=False), ssq=ssq,
                     w_rows_are_outputs=True).reshape(b, t, Z_A_COLS)
        z_g = matmul(hb, cast_weight_t(w_in_t, i, g_in, Z_G_OFF, Z_G_COLS, transpose=False), ssq=ssq,
                     w_rows_are_outputs=True).reshape(b, t, Z_G_COLS)
        w_gate = cast_weight_t(w_in_t, i, g_in, Z_A_COLS, LANES, keep=N_GATES)
        gate_logits = matmul(hb, w_gate, ssq=ssq, tn=LANES).reshape(b, t, LANES)

        on = mix_out_norm[i]
        mixed_a = conv_mixer(z_a, conv_w[i], on[:CONV_WIDTH])
        kc, vc = compress_kv(z_a, cmp_pos[i], cmp_w1[i], cmp_w2[i], k_gain[i])
        o_b, sel, used = cmp_attention(z_a, kc, vc, q_gain[i], gate_logits)
        kv3 = kv_prep(z_a, k_gain[i])
        o_b = flash_attention("slc", z_a, kv3, q_gain[i], gate_logits, o_b, sel=sel, used=used)
        o_b, ssq_b = flash_attention("win", z_a, kv3, q_gain[i], gate_logits, o_b)
        mixed_c = gmlp_mixer(z_g, sgu_gain[i], w_sp[i], b_sp[i], on[CONV_WIDTH + NSA_WIDTH:])
        g_out = jnp.concatenate([ones[:CONV_WIDTH], on[CONV_WIDTH:CONV_WIDTH + NSA_WIDTH],
                                 ones[:GMLP_WIDTH]])
        mixed = (mixed_a.reshape(m, CONV_WIDTH), o_b.reshape(m, NSA_WIDTH),
                 mixed_c.reshape(m, GMLP_WIDTH))
        h2, hb, ssq = matmul(mixed, cast_weight(w_out, i, g_out), mode="residual", res=h2,
                             ssq=(None, ssq_b.reshape(m, NSA_KV_HEADS * LANES), None),
                             stats=True, tn=512)

        hid = matmul(hb, cast_weight(w_mlp_in, i, mlp_norm[i]), mode="relu2", out_dtype=bf, ssq=ssq)
        h2 = matmul(hid, cast_weight(w_mlp_out, i, ones), mode="residual", res=h2)
        hb, ssq = stats_cast(h2)

        out = matmul(hb, cast_weight(w_ple_gate, i, ple_norm[i]), mode="ple", res=h2, p=p_rows,
                     wp=w_ple_proj, layer=i, ssq=ssq, stats=i + 1 < depth, tn=512)
        h2, hb, ssq = out if i + 1 < depth else (out, None, None)
    return h2.reshape(b, t, d)
```

```python
import functools

import jax
import jax.numpy as jnp
import numpy as np
from jax import lax
from jax.experimental import pallas as pl
from jax.experimental.pallas import tpu as pltpu

D_MODEL = 4096
HEAD_DIM = 128
CONV_WIDTH = D_MODEL // 4
CONV_K = 3
NSA_HEADS = (D_MODEL // 2) // HEAD_DIM
NSA_WIDTH = NSA_HEADS * HEAD_DIM
NSA_KV_HEADS = 4
NSA_REP = NSA_HEADS // NSA_KV_HEADS
NSA_BRANCHES = 3
KV_WIDTH = NSA_KV_HEADS * HEAD_DIM
CMP_LEN = 32
CMP_STRIDE = 16
SLC_BLOCK = 64
SLC_SHIFT = 6
N_SELECT = 16
WINDOW = 512
GMLP_WIDTH = D_MODEL // 4
GMLP_GROUPS = GMLP_WIDTH // HEAD_DIM
GMLP_CHUNK = 128
D_FF = 4 * D_MODEL
PLE_DIM = 256
N_GATES = NSA_HEADS * NSA_BRANCHES
EPS = 1e-6
NEG_INF = -1e30
FORCE_SCORE = 1e9
BELOW_NEG_INF = -3e38

Z_A_COLS = 3 * CONV_WIDTH + NSA_WIDTH + 2 * NSA_BRANCHES * KV_WIDTH
Z_G_COLS = 2 * GMLP_WIDTH
Z_G_OFF = Z_A_COLS + N_GATES
Q_OFF = 3 * CONV_WIDTH
KCMP_OFF = Q_OFF + NSA_WIDTH
VCMP_OFF = KCMP_OFF + KV_WIDTH
KV4_OFF = VCMP_OFF + KV_WIDTH

LANES = 128
VMEM_LIMIT_BYTES = 60 * 1024 * 1024


def _cparams(dims):
    return pltpu.CompilerParams(dimension_semantics=dims, vmem_limit_bytes=VMEM_LIMIT_BYTES)


def _rms(x, gain):
    return x * lax.rsqrt(jnp.mean(x * x, axis=-1, keepdims=True) + EPS) * gain


def _gelu(x):
    return jax.nn.gelu(x)


def _row_scale(ssq, width, n_groups=1):
    tot = ssq[:, 0:LANES]
    for gi in range(1, n_groups):
        tot = tot + ssq[:, gi * LANES:(gi + 1) * LANES]
    return lax.rsqrt(tot * (1.0 / width) + EPS)


def _lane_sum_sq(x):
    return jnp.broadcast_to(jnp.sum(x * x, axis=-1, keepdims=True), (x.shape[0], LANES))


def _stats_kernel(x_ref, hb_ref, ssq_ref):
    x = x_ref[...]
    hb_ref[...] = x.astype(hb_ref.dtype)
    ssq_ref[...] = _lane_sum_sq(x)


def stats_cast(x, tm=512):
    m, d = x.shape
    return pl.pallas_call(
        _stats_kernel,
        grid=(m // tm,),
        in_specs=[pl.BlockSpec((tm, d), lambda i: (i, 0))],
        out_specs=[pl.BlockSpec((tm, d), lambda i: (i, 0)),
                   pl.BlockSpec((tm, LANES), lambda i: (i, 0))],
        out_shape=[jax.ShapeDtypeStruct((m, d), jnp.bfloat16),
                   jax.ShapeDtypeStruct((m, LANES), jnp.float32)],
        compiler_params=_cparams(("parallel",)),
        name="stats_cast",
    )(x)


def _cast_kernel(w_ref, g_ref, o_ref):
    o_ref[...] = (w_ref[...] * g_ref[...]).astype(o_ref.dtype)


def _cast_t_kernel(a_ref, *refs, shift, keep, transpose):
    g_ref, o_ref = refs[-2:]
    x = a_ref[...]
    if shift:
        x = jnp.concatenate([x[shift:, :], refs[0][:shift, :]], axis=0)
    if keep < x.shape[0]:
        row = lax.broadcasted_iota(jnp.int32, x.shape, 0)
        x = jnp.where(row < keep, x, 0.0)
    x = x * g_ref[...]
    o_ref[...] = (x.T if transpose else x).astype(o_ref.dtype)


def cast_weight(w, layer, gain, tk=1024, tn=2048):
    _, kdim, n = w.shape
    tk, tn = min(tk, kdim), min(tn, n)
    assert kdim % tk == 0 and n % tn == 0
    return pl.pallas_call(
        _cast_kernel,
        grid=(kdim // tk, n // tn),
        in_specs=[pl.BlockSpec((None, tk, tn), lambda i, j: (layer, i, j)),
                  pl.BlockSpec((tk, 1), lambda i, j: (i, 0))],
        out_specs=pl.BlockSpec((tk, tn), lambda i, j: (i, j)),
        out_shape=jax.ShapeDtypeStruct((kdim, n), jnp.bfloat16),
        compiler_params=_cparams(("parallel", "parallel")),
        name="cast_weight",
    )(w, gain.reshape(kdim, 1))


def cast_weight_t(w_t, layer, gain, row0, nrows, keep=None, transpose=True, tr=512, tc=2048):
    _, n, kdim = w_t.shape
    tr, tc = min(tr, nrows), min(tc, kdim)
    shift = row0 % tr
    base = row0 // tr
    assert nrows % tr == 0 and kdim % tc == 0 and shift % 8 == 0
    in_specs = [pl.BlockSpec((None, tr, tc), lambda i, j: (layer, base + j, i))]
    if shift:
        head = 8 * pl.cdiv(shift, 8)
        while tr % head:
            head += 8
        in_specs.append(pl.BlockSpec((None, head, tc), lambda i, j: (layer, (base + j + 1) * (tr // head), i)))
    return pl.pallas_call(
        functools.partial(_cast_t_kernel, shift=shift, keep=keep or nrows, transpose=transpose),
        grid=(kdim // tc, nrows // tr),
        in_specs=in_specs + [pl.BlockSpec((1, tc), lambda i, j: (0, i))],
        out_specs=(pl.BlockSpec((tc, tr), lambda i, j: (i, j)) if transpose
                   else pl.BlockSpec((tr, tc), lambda i, j: (j, i))),
        out_shape=jax.ShapeDtypeStruct((kdim, nrows) if transpose else (nrows, kdim), jnp.bfloat16),
        compiler_params=_cparams(("parallel", "parallel")),
        name="cast_weight_t",
    )(*([w_t] * (2 if shift else 1)), gain.reshape(1, kdim))


def _mm_kernel(*refs, mode, nk, n_x, scale_groups, stats, w_rows_are_outputs):
    x_refs, refs = refs[:n_x], refs[n_x:]
    refs = list(refs)
    take = lambda cond: refs.pop(0) if cond else None
    w_ref = refs.pop(0)
    res_ref = take(mode in ("residual", "ple"))
    p_ref, wp_ref = take(mode == "ple"), take(mode == "ple")
    ssq_refs = [take(g > 0) for g in scale_groups]
    o_ref = refs.pop(0)
    hb_ref, ssq_out_ref = take(stats), take(stats)
    tn = o_ref.shape[1]
    j = pl.program_id(1)
    k = pl.program_id(2)

    def scaled(d, n):
        if ssq_refs[n] is None:
            return d
        s = _row_scale(ssq_refs[n][...], x_refs[n].shape[1] * (nk if len(x_refs) == 1 else 1),
                       scale_groups[n])
        return d * jnp.tile(s, (1, tn // LANES))

    def finish(new):
        o_ref[...] = new
        if stats:
            hb_ref[...] = new.astype(hb_ref.dtype)
            ssq = _lane_sum_sq(new)

            @pl.when(j == 0)
            def _():
                ssq_out_ref[...] = ssq

            @pl.when(j > 0)
            def _():
                ssq_out_ref[...] += ssq

    def epilogue(acc):
        if mode == "plain":
            o_ref[...] = acc.astype(o_ref.dtype)
        elif mode == "relu2":
            r = jnp.maximum(acc, 0.0)
            o_ref[...] = (r * r).astype(o_ref.dtype)
        elif mode == "residual":
            finish(res_ref[...] + acc)
        else:
            proj = jnp.dot(p_ref[...].astype(jnp.bfloat16), wp_ref[...].astype(jnp.bfloat16),
                           preferred_element_type=jnp.float32)
            finish(res_ref[...] + jax.nn.sigmoid(acc) * proj)

    def product():
        part, off = None, 0
        for n, x_ref in enumerate(x_refs):
            kx = x_ref.shape[1]
            if w_rows_are_outputs:
                d = lax.dot_general(x_ref[...], w_ref[:, off:off + kx], (((1,), (1,)), ((), ())),
                                    preferred_element_type=jnp.float32)
            else:
                d = jnp.dot(x_ref[...], w_ref[off:off + kx, :], preferred_element_type=jnp.float32)
            d = scaled(d, n)
            part = d if part is None else part + d
            off += kx
        return part

    if nk == 1:
        epilogue(product())
        return
    if mode == "residual":
        @pl.when(k == 0)
        def _():
            o_ref[...] = res_ref[...] + product()

        @pl.when(jnp.logical_and(k > 0, k < nk - 1))
        def _():
            o_ref[...] += product()

        @pl.when(k == nk - 1)
        def _():
            finish(o_ref[...] + product())
        return
    acc_ref, = refs

    @pl.when(k == 0)
    def _():
        acc_ref[...] = product()

    @pl.when(jnp.logical_and(k > 0, k < nk - 1))
    def _():
        acc_ref[...] += product()

    @pl.when(k == nk - 1)
    def _():
        epilogue(acc_ref[...] + product())


def matmul(x, w, mode="plain", out_dtype=jnp.float32, res=None, p=None, wp=None, layer=0,
           ssq=None, stats=False, w_rows_are_outputs=False, tm=1024, tn=1024, tk=4096):
    xs = x if isinstance(x, (tuple, list)) else (x,)
    ssqs = ssq if isinstance(ssq, (tuple, list)) else (ssq,) * len(xs)
    m = xs[0].shape[0]
    kdim, n = w.shape[::-1] if w_rows_are_outputs else w.shape
    assert sum(xi.shape[1] for xi in xs) == kdim and len(ssqs) == len(xs)
    tm, tn, tk = min(tm, m), min(tn, n), min(tk, kdim)
    nk = kdim // tk
    assert m % tm == 0 and n % tn == 0 and kdim % tk == 0 and (nk == 1 or len(xs) == 1)
    assert not stats or mode in ("residual", "ple")
    in_specs = [pl.BlockSpec((tm, xi.shape[1] if len(xs) > 1 else tk), lambda i, j, k: (i, k))
                for xi in xs]
    in_specs.append(pl.BlockSpec((tn, tk), lambda i, j, k: (j, k)) if w_rows_are_outputs
                    else pl.BlockSpec((tk, tn), lambda i, j, k: (k, j)))
    args = [*xs, w]
    if mode in ("residual", "ple"):
        in_specs.append(pl.BlockSpec((tm, tn), lambda i, j, k: (i, j)))
        args.append(res)
    if mode == "ple":
        pd = p.shape[2]
        in_specs += [pl.BlockSpec((None, tm, pd), lambda i, j, k: (layer, i, 0)),
                     pl.BlockSpec((None, pd, tn), lambda i, j, k: (layer, 0, j))]
        args += [p, wp]
    scale_groups = tuple(0 if s is None else s.shape[1] // LANES for s in ssqs)
    for s in ssqs:
        if s is not None:
            in_specs.append(pl.BlockSpec((tm, s.shape[1]), lambda i, j, k: (i, 0)))
            args.append(s)
    tile = pl.BlockSpec((tm, tn), lambda i, j, k: (i, j))
    out_specs, out_shape = tile, jax.ShapeDtypeStruct((m, n), out_dtype)
    if stats:
        out_specs = [tile, tile, pl.BlockSpec((tm, LANES), lambda i, j, k: (i, 0))]
        out_shape = [out_shape, jax.ShapeDtypeStruct((m, n), jnp.bfloat16),
                     jax.ShapeDtypeStruct((m, LANES), jnp.float32)]
    scratch = [pltpu.VMEM((tm, tn), jnp.float32)] if nk > 1 and mode != "residual" else []
    return pl.pallas_call(
        functools.partial(_mm_kernel, mode=mode, nk=nk, n_x=len(xs), scale_groups=scale_groups,
                          stats=stats, w_rows_are_outputs=w_rows_are_outputs),
        grid=(m // tm, n // tn, nk),
        in_specs=in_specs,
        out_specs=out_specs,
        out_shape=out_shape,
        scratch_shapes=scratch,
        compiler_params=_cparams(("parallel", "arbitrary" if stats else "parallel", "arbitrary")),
        name="matmul_" + mode,
    )(*args)


HALO_ROWS = 8


def _conv_kernel(xa_ref, gb_ref, gc_ref, xap_ref, gcp_ref, cw_ref, on_ref, o_ref):
    i = pl.program_id(1)
    inner = gc_ref[0] * xa_ref[0]
    prev = gcp_ref[0] * xap_ref[0]
    prev = jnp.where(i > 0, prev, 0.0)
    tt = inner.shape[0]
    row = lax.broadcasted_iota(jnp.int32, inner.shape, 0)
    m1 = jnp.where(row == 0, prev[HALO_ROWS - 1:HALO_ROWS], pltpu.roll(inner, 1, 0))
    m2 = pltpu.roll(inner, 2, 0)
    m2 = jnp.where(row == 0, prev[HALO_ROWS - 2:HALO_ROWS - 1], m2)
    m2 = jnp.where(row == 1, prev[HALO_ROWS - 1:HALO_ROWS], m2)
    cw = cw_ref[...]
    conv = cw[0:1] * m2 + cw[1:2] * m1 + cw[2:3] * inner
    o_ref[0] = _rms(gb_ref[0] * conv, on_ref[...]).astype(o_ref.dtype)


def conv_mixer(z3, conv_w, on_gain, tt=1024):
    b, t, _ = z3.shape
    c = CONV_WIDTH
    hb = tt // HALO_ROWS
    cur = lambda col: pl.BlockSpec((1, tt, c), lambda bi, i: (bi, i, col))
    halo = lambda col: pl.BlockSpec((1, HALO_ROWS, c),
                                    lambda bi, i: (bi, jnp.maximum(i * hb - 1, 0), col))
    return pl.pallas_call(
        _conv_kernel,
        grid=(b, t // tt),
        in_specs=[cur(0), cur(1), cur(2), halo(0), halo(2),
                  pl.BlockSpec((HALO_ROWS, c), lambda bi, i: (0, 0)),
                  pl.BlockSpec((1, c), lambda bi, i: (0, 0))],
        out_specs=pl.BlockSpec((1, tt, c), lambda bi, i: (bi, i, 0)),
        out_shape=jax.ShapeDtypeStruct((b, t, c), jnp.bfloat16),
        compiler_params=_cparams(("parallel", "parallel")),
        name="conv_mixer",
    )(z3, z3, z3, z3, z3,
      jnp.pad(conv_w, ((0, HALO_ROWS - CONV_K), (0, 0))), on_gain.reshape(1, c))


def _gmlp_kernel(gu_ref, gv_ref, sg_ref, wsp_ref, bexp_ref, on_ref, o_ref, *, n_chunks):
    u = _gelu(gu_ref[0])
    v = _rms(_gelu(gv_ref[0]), sg_ref[...]).astype(jnp.bfloat16)
    ck = GMLP_CHUNK
    r_i = lax.broadcasted_iota(jnp.int32, (ck, ck), 0)
    c_i = lax.broadcasted_iota(jnp.int32, (ck, ck), 1)
    bexp = bexp_ref[...]
    rows = []
    for c in range(n_chunks):
        cols = []
        for g in range(GMLP_GROUPS):
            w = jnp.where(r_i >= c_i, wsp_ref[g], 0.0).astype(jnp.bfloat16)
            vg = v[c * ck:(c + 1) * ck, g * HEAD_DIM:(g + 1) * HEAD_DIM]
            cols.append(jnp.dot(w, vg, preferred_element_type=jnp.float32))
        rows.append(jnp.concatenate(cols, axis=1) + bexp)
    spatial = jnp.concatenate(rows, axis=0) if n_chunks > 1 else rows[0]
    o_ref[0] = _rms(u * spatial, on_ref[...]).astype(o_ref.dtype)


def gmlp_mixer(z3, sgu_gain, w_sp, b_sp, on_gain, tt=512):
    b, t, _ = z3.shape
    c = GMLP_WIDTH
    bexp = jnp.repeat(b_sp.T, HEAD_DIM, axis=1)
    const2 = lambda shape: pl.BlockSpec(shape, lambda bi, i: (0,) * len(shape))
    return pl.pallas_call(
        functools.partial(_gmlp_kernel, n_chunks=tt // GMLP_CHUNK),
        grid=(b, t // tt),
        in_specs=[pl.BlockSpec((1, tt, c), lambda bi, i: (bi, i, 0)),
                  pl.BlockSpec((1, tt, c), lambda bi, i: (bi, i, 1)),
                  const2((1, c)), const2((GMLP_GROUPS, GMLP_CHUNK, GMLP_CHUNK)),
                  const2((GMLP_CHUNK, c)), const2((1, c))],
        out_specs=pl.BlockSpec((1, tt, c), lambda bi, i: (bi, i, 0)),
        out_shape=jax.ShapeDtypeStruct((b, t, c), jnp.bfloat16),
        compiler_params=_cparams(("parallel", "parallel")),
        name="gmlp_mixer",
    )(z3, z3, sgu_gain.reshape(1, c), w_sp, bexp, on_gain.reshape(1, c))


ATT_TILE = 512
KV_TILE_SHIFT = 9
AUG_DIM = 2 * HEAD_DIM
POS_LANE = 64
ONE_LANE = 70
FEAT_END = 73
MASK_BIG = 2.0 ** 100
MASKED = -MASK_BIG
M_INIT = NEG_INF
LOG2E = 1.4426950408889634


def _kvprep_kernel(z_ref, kg_ref, o_ref, *, tt):
    i = pl.program_id(1)
    z = z_ref[0]
    pos = i * tt + lax.broadcasted_iota(jnp.int32, (tt, HEAD_DIM), 0)
    lane = lax.broadcasted_iota(jnp.int32, (tt, HEAD_DIM), 1)
    blk = lax.shift_right_logical(pos, SLC_SHIFT)
    hi = lax.shift_left(blk, SLC_SHIFT).astype(jnp.float32)
    lo = jnp.bitwise_and(pos, SLC_BLOCK - 1).astype(jnp.float32)
    posf = jnp.where(jnp.bitwise_and(lane, 1) == 0, hi, lo)
    tail = jnp.where(lane < ONE_LANE, posf, jnp.where(lane < FEAT_END, 1.0, 0.0))
    onehot = jnp.where(lane == blk, MASK_BIG, 0.0)
    kfeat = (jnp.where(lane < POS_LANE, onehot, tail).astype(o_ref.dtype),
             jnp.where(lane < POS_LANE, 0.0, tail).astype(o_ref.dtype))
    vfeat = jnp.ones((tt, HEAD_DIM), o_ref.dtype)
    for part in range(4):
        for g in range(NSA_KV_HEADS):
            src = part * KV_WIDTH + g * HEAD_DIM
            dst = (part * NSA_KV_HEADS + g) * AUG_DIM
            x = z[:, src:src + HEAD_DIM]
            if part % 2 == 0:
                x = _rms(x, kg_ref[1 + part // 2:2 + part // 2, :])
            o_ref[0, :, dst:dst + HEAD_DIM] = x.astype(o_ref.dtype)
            o_ref[0, :, dst + HEAD_DIM:dst + AUG_DIM] = kfeat[part // 2] if part % 2 == 0 else vfeat


def kv_prep(z3, k_gain, tt=512):
    b, t, _ = z3.shape
    w = 4 * KV_WIDTH
    wo = 4 * NSA_KV_HEADS * AUG_DIM
    kg = jnp.pad(k_gain, ((0, 8 - NSA_BRANCHES), (0, 0)))
    return pl.pallas_call(
        functools.partial(_kvprep_kernel, tt=tt),
        grid=(b, t // tt),
        in_specs=[pl.BlockSpec((1, tt, w), lambda bi, i: (bi, i, KV4_OFF // w)),
                  pl.BlockSpec((8, HEAD_DIM), lambda bi, i: (0, 0))],
        out_specs=pl.BlockSpec((1, tt, wo), lambda bi, i: (bi, i, 0)),
        out_shape=jax.ShapeDtypeStruct((b, t, wo), jnp.bfloat16),
        compiler_params=_cparams(("parallel", "parallel")),
        name="kv_prep",
    )(z3, kg)


def _compress_kernel(k_ref, v_ref, w1_ref, w2_ref, pe_ref, kg_ref, kc_ref, vc_ref, *, n_chunk):
    def one(src_ref, which):
        ya = jnp.zeros((n_chunk, HEAD_DIM), jnp.float32)
        yb = jnp.zeros((n_chunk, HEAD_DIM), jnp.float32)
        bias = jnp.zeros((8, HEAD_DIM), jnp.float32)
        for l in range(CMP_STRIDE):
            xl = src_ref[0, pl.ds(l, n_chunk, stride=CMP_STRIDE), :].astype(jnp.bfloat16)
            ya += jnp.dot(xl, w1_ref[which, l], preferred_element_type=jnp.float32)
            yb += jnp.dot(xl, w1_ref[which, CMP_STRIDE + l], preferred_element_type=jnp.float32)
        for l in range(CMP_LEN):
            pe_l = jnp.broadcast_to(pe_ref[which, l:l + 1, :], (8, HEAD_DIM)).astype(jnp.bfloat16)
            bias += jnp.dot(pe_l, w1_ref[which, l], preferred_element_type=jnp.float32)
        hid = _gelu(ya + pltpu.roll(yb, n_chunk - 1, 0) + bias[0:1])
        return jnp.dot(hid.astype(jnp.bfloat16), w2_ref[which], preferred_element_type=jnp.float32)

    kc_ref[0, 0] = _rms(one(k_ref, 0), kg_ref[0:1, :]).astype(kc_ref.dtype)
    vc_ref[0, 0] = one(v_ref, 1).astype(vc_ref.dtype)


def compress_kv(z3, cmp_pos, cmp_w1, cmp_w2, k_gain):
    b, t, _ = z3.shape
    n_chunk = t // CMP_STRIDE
    kg = jnp.pad(k_gain, ((0, 8 - NSA_BRANCHES), (0, 0)))
    const = lambda shape: pl.BlockSpec(shape, lambda bi, g: (0,) * len(shape))
    out = jax.ShapeDtypeStruct((b, NSA_KV_HEADS, n_chunk, HEAD_DIM), jnp.bfloat16)
    return pl.pallas_call(
        functools.partial(_compress_kernel, n_chunk=n_chunk),
        grid=(b, NSA_KV_HEADS),
        in_specs=[pl.BlockSpec((1, t, HEAD_DIM), lambda bi, g: (bi, 0, KCMP_OFF // HEAD_DIM + g)),
                  pl.BlockSpec((1, t, HEAD_DIM), lambda bi, g: (bi, 0, VCMP_OFF // HEAD_DIM + g)),
                  const((2, CMP_LEN, HEAD_DIM, HEAD_DIM)), const((2, HEAD_DIM, HEAD_DIM)),
                  const((2, CMP_LEN, HEAD_DIM)), const((8, HEAD_DIM))],
        out_specs=[pl.BlockSpec((1, 1, n_chunk, HEAD_DIM), lambda bi, g: (bi, g, 0, 0))] * 2,
        out_shape=[out, out],
        compiler_params=_cparams(("parallel", "parallel")),
        name="compress_kv",
    )(z3, z3, cmp_w1.astype(jnp.bfloat16), cmp_w2.astype(jnp.bfloat16), cmp_pos, kg)


def _alibi_slopes():
    h = np.arange(1, NSA_HEADS + 1, dtype=np.float32)
    return np.power(np.float32(2.0), -8.0 * h / NSA_HEADS).astype(np.float32)


def _overlap_matrix(n_chunk, n_slc):
    n_cmp = n_chunk - CMP_LEN // CMP_STRIDE + 1
    jc = np.arange(n_cmp)[:, None] * CMP_STRIDE
    bs = np.arange(n_slc)[None, :] * SLC_BLOCK
    ov = np.clip(np.minimum(jc + CMP_LEN, bs + SLC_BLOCK) - np.maximum(jc, bs), 0, None)
    out = np.zeros((n_chunk, LANES), np.float32)
    out[:n_cmp, :n_slc] = ov.astype(np.float32) / CMP_LEN
    return out


def _q_heads(q_ref, qg_ref):
    q = q_ref[0]
    scale = HEAD_DIM ** -0.5
    return [(_rms(q[:, r * HEAD_DIM:(r + 1) * HEAD_DIM], qg_ref[...]) * scale).astype(jnp.bfloat16)
            for r in range(NSA_REP)]


def _gate_expand_matrix():
    m = np.zeros((NSA_BRANCHES, NSA_KV_HEADS, LANES, NSA_REP * HEAD_DIM), np.float32)
    for br in range(NSA_BRANCHES):
        for g in range(NSA_KV_HEADS):
            for r in range(NSA_REP):
                m[br, g, (g * NSA_REP + r) * NSA_BRANCHES + br, r * HEAD_DIM:(r + 1) * HEAD_DIM] = 1.0
    return m


def _branch_gate(gl_ref, ex_ref):
    sig = jax.nn.sigmoid(gl_ref[0])
    hi = sig.astype(jnp.bfloat16)
    lo = (sig - hi.astype(jnp.float32)).astype(jnp.bfloat16)
    ex = ex_ref[...]
    return (jnp.dot(hi, ex, preferred_element_type=jnp.float32)
            + jnp.dot(lo, ex, preferred_element_type=jnp.float32))


def _cmp_kernel(slopes_ref, q_ref, kc_ref, vc_ref, qg_ref, ov_ref, gl_ref, ex_ref, o_ref, sel_ref,
                used_ref, *, tq, n_chunk, n_slc):
    g = pl.program_id(1)
    i = pl.program_id(2)
    gate = _branch_gate(gl_ref, ex_ref)
    n_cmp = n_chunk - CMP_LEN // CMP_STRIDE + 1
    qh = _q_heads(q_ref, qg_ref)
    kc = kc_ref[0, 0]
    vc = vc_ref[0, 0]
    t = i * tq + lax.broadcasted_iota(jnp.int32, (tq, n_chunk), 0)
    n = lax.broadcasted_iota(jnp.int32, (tq, n_chunk), 1)
    dist = t - (n * CMP_STRIDE + CMP_LEN - 1)
    mask = jnp.logical_and(dist >= 0, n < n_cmp)
    distf = dist.astype(jnp.float32)
    psum = jnp.zeros((tq, n_chunk), jnp.float32)
    for r in range(NSA_REP):
        s = lax.dot_general(qh[r], kc, (((1,), (1,)), ((), ())),
                            preferred_element_type=jnp.float32)
        s = jnp.where(mask, s - slopes_ref[g * NSA_REP + r] * distf, NEG_INF)
        mx = jnp.max(s, axis=-1, keepdims=True)
        e = jnp.where(mask, jnp.exp(s - mx), 0.0)
        den = jnp.sum(e, axis=-1, keepdims=True)
        p = e / jnp.where(den > 0.0, den, 1.0)
        psum += p
        o_ref[0, :, r * HEAD_DIM:(r + 1) * HEAD_DIM] = gate[:, r * HEAD_DIM:(r + 1) * HEAD_DIM] * jnp.dot(
            p.astype(jnp.bfloat16), vc, preferred_element_type=jnp.float32)

    ov = ov_ref[...]
    imp = jnp.zeros((tq, LANES), jnp.float32)
    rem = psum
    for _ in range(3):
        part = rem.astype(jnp.bfloat16)
        imp += jnp.dot(part, ov, preferred_element_type=jnp.float32)
        rem = rem - part.astype(jnp.float32)

    tb = i * tq + lax.broadcasted_iota(jnp.int32, (tq, LANES), 0)
    blk = lax.broadcasted_iota(jnp.int32, (tq, LANES), 1)
    qblk = lax.shift_right_logical(tb, SLC_SHIFT)
    forced = (blk == 0) | (blk == qblk) | (blk == qblk - 1)
    score = jnp.where(forced, FORCE_SCORE, jnp.where(blk <= qblk, imp, NEG_INF))
    score = jnp.where(blk < n_slc, score, BELOW_NEG_INF)
    st = score.T[0:n_slc]
    n_tiles = n_slc // 8
    tiles = [st[8 * j:8 * j + 8] for j in range(n_tiles)]
    sub8 = lax.broadcasted_iota(jnp.int32, (8, tq), 0)
    cnt = [jnp.zeros((8, tq), jnp.float32) for _ in range(n_tiles)]
    for m in range(n_slc):
        row = st[m:m + 1, :]
        own, off = divmod(m, 8)
        for j in range(n_tiles):
            if j < own:
                inc = jnp.where(row > tiles[j], 1.0, 0.0)
            elif j > own:
                inc = jnp.where(row >= tiles[j], 1.0, 0.0)
            else:
                later = jnp.where(sub8 > off, 1.0, 0.0)
                inc = (jnp.where(row >= tiles[j], later, 0.0)
                       + jnp.where(row > tiles[j], 1.0 - later, 0.0))
            cnt[j] = cnt[j] + inc
    keep = float(min(N_SELECT, n_slc))
    sel_t = jnp.concatenate([jnp.where(c < keep, 1.0, 0.0) for c in cnt]
                            + [jnp.zeros((LANES - n_slc, tq), jnp.float32)] * (n_slc < LANES), axis=0)
    sel = sel_t.T
    sel_ref[0, 0] = sel.astype(sel_ref.dtype)
    picked = jnp.broadcast_to(jnp.max(sel, axis=0, keepdims=True), (8, LANES)).astype(jnp.bfloat16)
    blk_i = lax.broadcasted_iota(jnp.int32, (LANES, LANES), 0)
    tile_i = lax.broadcasted_iota(jnp.int32, (LANES, LANES), 1)
    group = jnp.where(lax.shift_right_logical(blk_i, KV_TILE_SHIFT - SLC_SHIFT) == tile_i, 1.0, 0.0)
    used = jnp.dot(picked, group.astype(jnp.bfloat16), preferred_element_type=jnp.float32)
    used_ref[0, 0] = jnp.where(used > 0.5, 1, 0).astype(jnp.int32)


def cmp_attention(z3, kc, vc, q_gain, gate_logits):
    b, t, _ = z3.shape
    tq = min(ATT_TILE, t)
    n_chunk = t // CMP_STRIDE
    n_slc = t // SLC_BLOCK
    assert n_slc <= LANES
    gw = NSA_REP * HEAD_DIM
    grid_spec = pltpu.PrefetchScalarGridSpec(
        num_scalar_prefetch=1,
        grid=(b, NSA_KV_HEADS, t // tq),
        in_specs=[pl.BlockSpec((1, tq, gw), lambda bi, g, i, s: (bi, i, Q_OFF // gw + g)),
                  pl.BlockSpec((1, 1, n_chunk, HEAD_DIM), lambda bi, g, i, s: (bi, g, 0, 0)),
                  pl.BlockSpec((1, 1, n_chunk, HEAD_DIM), lambda bi, g, i, s: (bi, g, 0, 0)),
                  pl.BlockSpec((1, HEAD_DIM), lambda bi, g, i, s: (0, 0)),
                  pl.BlockSpec((n_chunk, LANES), lambda bi, g, i, s: (0, 0)),
                  pl.BlockSpec((1, tq, LANES), lambda bi, g, i, s: (bi, i, 0)),
                  pl.BlockSpec((None, None, LANES, gw), lambda bi, g, i, s: (0, g, 0, 0))],
        out_specs=[pl.BlockSpec((1, tq, gw), lambda bi, g, i, s: (bi, i, g)),
                   pl.BlockSpec((1, 1, tq, LANES), lambda bi, g, i, s: (bi, g, i, 0)),
                   pl.BlockSpec((1, 1, 8, LANES), lambda bi, g, i, s: (bi, g, i, 0))],
    )
    return pl.pallas_call(
        functools.partial(_cmp_kernel, tq=tq, n_chunk=n_chunk, n_slc=n_slc),
        grid_spec=grid_spec,
        out_shape=[jax.ShapeDtypeStruct((b, t, NSA_WIDTH), jnp.float32),
                   jax.ShapeDtypeStruct((b, NSA_KV_HEADS, t, LANES), jnp.bfloat16),
                   jax.ShapeDtypeStruct((b, NSA_KV_HEADS, t // tq * 8, LANES), jnp.int32)],
        compiler_params=_cparams(("parallel", "parallel", "parallel")),
        name="cmp_attention",
    )(jnp.asarray(_alibi_slopes()), z3, kc, vc, q_gain.reshape(1, HEAD_DIM),
      jnp.asarray(_overlap_matrix(n_chunk, n_slc), dtype=jnp.bfloat16), gate_logits,
      jnp.asarray(_gate_expand_matrix(), dtype=jnp.bfloat16))


STEP_FIRST, STEP_LAST, STEP_MASKED = 1, 2, 4


def _flash_steps(t, tq, tk):
    qi, kv, fl = [], [], []
    for i in range(t // tq):
        lo, hi = i * tq, i * tq + tq - 1
        n_tiles = hi // tk + 1
        for j in range(n_tiles):
            flag = (STEP_FIRST if j == 0 else 0) | (STEP_LAST if j == n_tiles - 1 else 0)
            flag |= 0 if j * tk + tk - 1 <= lo else STEP_MASKED
            qi.append(i), kv.append(j), fl.append(flag)
    return [np.asarray(a, np.int32) for a in (qi, kv, fl)]


def _split3(x):
    a = x.astype(jnp.bfloat16).astype(jnp.float32)
    b = (x - a).astype(jnp.bfloat16).astype(jnp.float32)
    c = (x - a - b).astype(jnp.bfloat16).astype(jnp.float32)
    return a, b, c


def _q_aug(q_ref, qg_ref, slopes_ref, g, t_start, sel_m1):
    q = q_ref[0]
    tq = q.shape[0]
    lane = lax.broadcasted_iota(jnp.int32, (tq, HEAD_DIM), 1)
    if sel_m1 is None:
        sel_m1 = jnp.zeros((tq, HEAD_DIM), jnp.float32)
    t0 = t_start.astype(jnp.float32)
    out = []
    for r in range(NSA_REP):
        qn = _rms(q[:, r * HEAD_DIM:(r + 1) * HEAD_DIM], qg_ref[...]) * (HEAD_DIM ** -0.5 * LOG2E)
        slope2 = slopes_ref[g * NSA_REP + r] * LOG2E
        a, b, c = _split3(jnp.where(lane < ONE_LANE, slope2, -slope2 * t0))
        k3 = jnp.where(lane < ONE_LANE, lax.shift_right_logical(lane - POS_LANE, 1), lane - ONE_LANE)
        term = jnp.where(k3 == 0, a, jnp.where(k3 == 1, b, c))
        feat = jnp.where(lane < POS_LANE, sel_m1, jnp.where(lane < FEAT_END, term, 0.0))
        out.append((qn.astype(jnp.bfloat16), feat.astype(jnp.bfloat16)))
    return out


def _slc_kernel(slopes_ref, qi_ref, kv_ref, fl_ref, used_ref, q_ref, k_ref, v_ref, qg_ref, gl_ref,
                ex_ref, prev_ref, sel_ref, o_ref, q_sc, m_sc, acc_sc, *, tq, tk):
    g = pl.program_id(1)
    s = pl.program_id(2)
    i = qi_ref[s]
    kv_idx = kv_ref[s]
    flags = fl_ref[s]

    @pl.when(jnp.bitwise_and(flags, STEP_FIRST) != 0)
    def _():
        sel_m1 = sel_ref[0, 0].astype(jnp.float32) - 1.0
        for r, (qn, feat) in enumerate(_q_aug(q_ref, qg_ref, slopes_ref, g, i * tq, sel_m1)):
            q_sc[r, :, 0:HEAD_DIM] = qn
            q_sc[r, :, HEAD_DIM:AUG_DIM] = feat
        m_sc[...] = jnp.full(m_sc.shape, M_INIT, jnp.float32)
        acc_sc[...] = jnp.zeros(acc_sc.shape, jnp.float32)

    def step(masked):
        k = k_ref[0]
        v = v_ref[0]
        if masked:
            t = i * tq + lax.broadcasted_iota(jnp.int32, (tq, tk), 0)
            ok = t >= kv_idx * tk + lax.broadcasted_iota(jnp.int32, (tq, tk), 1)
        for r in range(NSA_REP):
            s2 = lax.dot_general(q_sc[r], k, (((1,), (1,)), ((), ())),
                                 preferred_element_type=jnp.float32)
            if masked:
                s2 = jnp.where(ok, s2, MASKED)
            m_prev = m_sc[r]
            m_new = jnp.maximum(m_prev, jnp.max(s2, axis=-1, keepdims=True))
            p = jnp.exp2(s2 - jnp.tile(m_new, (1, tk // HEAD_DIM))).astype(jnp.bfloat16)
            alpha = jnp.exp2(m_prev - m_new)
            acc_sc[r] = jnp.tile(alpha, (1, AUG_DIM // HEAD_DIM)) * acc_sc[r] + jnp.dot(
                p, v, preferred_element_type=jnp.float32)
            m_sc[r] = m_new

    is_masked = jnp.bitwise_and(flags, STEP_MASKED) != 0
    bg = pl.program_id(0) * NSA_KV_HEADS + g
    live = used_ref[(bg * pl.num_programs(2) + s)] != 0
    pl.when(jnp.logical_and(live, jnp.logical_not(is_masked)))(functools.partial(step, False))
    pl.when(jnp.logical_and(live, is_masked))(functools.partial(step, True))

    @pl.when(jnp.bitwise_and(flags, STEP_LAST) != 0)
    def _():
        gate = _branch_gate(gl_ref, ex_ref)
        prev = prev_ref[0]
        for r in range(NSA_REP):
            acc = acc_sc[r]
            cols = slice(r * HEAD_DIM, (r + 1) * HEAD_DIM)
            o_ref[0, :, cols] = prev[:, cols] + gate[:, cols] * (
                acc[:, 0:HEAD_DIM] / acc[:, HEAD_DIM:AUG_DIM])


def selected_attention(z3, kv3, q_gain, gate_logits, prev, sel, used):
    b, t, _ = z3.shape
    tq = tk = min(ATT_TILE, t)
    gw = NSA_REP * HEAD_DIM
    qi, kv, fl = _flash_steps(t, tq, tk)
    used = used[:, :, ::8, :][:, :, qi, kv].reshape(-1)

    q_tile = lambda col: (lambda bi, g, s, sl, qi, *_: (bi, qi[s], col + g))
    kv_tile = lambda col: (lambda bi, g, s, sl, qi, kv, *_: (bi, kv[s], col + g))
    grid_spec = pltpu.PrefetchScalarGridSpec(
        num_scalar_prefetch=5,
        grid=(b, NSA_KV_HEADS, len(qi)),
        in_specs=[pl.BlockSpec((1, tq, gw), q_tile(Q_OFF // gw)),
                  pl.BlockSpec((1, tk, AUG_DIM), kv_tile(0)),
                  pl.BlockSpec((1, tk, AUG_DIM), kv_tile(NSA_KV_HEADS)),
                  pl.BlockSpec((1, HEAD_DIM), lambda bi, g, s, *_: (0, 0)),
                  pl.BlockSpec((1, tq, LANES), lambda bi, g, s, sl, qi, *_: (bi, qi[s], 0)),
                  pl.BlockSpec((None, None, LANES, gw), lambda bi, g, s, *_: (1, g, 0, 0)),
                  pl.BlockSpec((1, tq, gw), q_tile(0)),
                  pl.BlockSpec((1, 1, tq, LANES), lambda bi, g, s, sl, qi, *_: (bi, g, qi[s], 0))],
        out_specs=pl.BlockSpec((1, tq, gw), q_tile(0)),
        scratch_shapes=[pltpu.VMEM((NSA_REP, tq, AUG_DIM), jnp.bfloat16),
                        pltpu.VMEM((NSA_REP, tq, HEAD_DIM), jnp.float32),
                        pltpu.VMEM((NSA_REP, tq, AUG_DIM), jnp.float32)],
    )
    return pl.pallas_call(
        functools.partial(_slc_kernel, tq=tq, tk=tk),
        grid_spec=grid_spec,
        out_shape=jax.ShapeDtypeStruct((b, t, NSA_WIDTH), jnp.float32),
        compiler_params=_cparams(("parallel", "parallel", "arbitrary")),
        name="selected_attention",
    )(jnp.asarray(_alibi_slopes()), jnp.asarray(qi), jnp.asarray(kv), jnp.asarray(fl), used,
      z3, kv3, kv3, q_gain.reshape(1, HEAD_DIM), gate_logits,
      jnp.asarray(_gate_expand_matrix(), dtype=jnp.bfloat16), prev, sel)


def _win_kernel(slopes_ref, q_ref, kp_ref, vp_ref, kc_ref, vc_ref, qg_ref, gl_ref, ex_ref, prev_ref,
                o_ref, ssq_ref, *, tq):
    g = pl.program_id(1)
    i = pl.program_id(2)
    row = lax.broadcasted_iota(jnp.int32, (tq, tq), 0)
    col = lax.broadcasted_iota(jnp.int32, (tq, tq), 1)
    ok_cur = row >= col
    ok_prev = jnp.logical_and(row + tq - col < WINDOW, i > 0)
    gate = _branch_gate(gl_ref, ex_ref)
    prev = prev_ref[0]
    contract_last = (((1,), (1,)), ((), ()))
    ssq = jnp.zeros((tq, LANES), jnp.float32)
    for r, (qn, feat) in enumerate(_q_aug(q_ref, qg_ref, slopes_ref, g, i * tq, None)):
        qa = jnp.concatenate([qn, feat], axis=1)
        s_p = lax.dot_general(qa, kp_ref[0], contract_last, preferred_element_type=jnp.float32)
        s_c = lax.dot_general(qa, kc_ref[0], contract_last, preferred_element_type=jnp.float32)
        s_p = jnp.where(ok_prev, s_p, MASKED)
        s_c = jnp.where(ok_cur, s_c, MASKED)
        m = jnp.maximum(jnp.max(s_p, axis=-1, keepdims=True), jnp.max(s_c, axis=-1, keepdims=True))
        p_p = jnp.exp2(s_p - m).astype(jnp.bfloat16)
        p_c = jnp.exp2(s_c - m).astype(jnp.bfloat16)
        acc = (jnp.dot(p_p, vp_ref[0], preferred_element_type=jnp.float32)
               + jnp.dot(p_c, vc_ref[0], preferred_element_type=jnp.float32))
        cols = slice(r * HEAD_DIM, (r + 1) * HEAD_DIM)
        tot = prev[:, cols] + gate[:, cols] * (acc[:, 0:HEAD_DIM] / acc[:, HEAD_DIM:AUG_DIM])
        o_ref[0, :, cols] = tot.astype(o_ref.dtype)
        ssq = ssq + _lane_sum_sq(tot)
    ssq_ref[0] = ssq


def window_attention(z3, kv3, q_gain, gate_logits, prev):
    b, t, _ = z3.shape
    tq = min(ATT_TILE, t)
    assert WINDOW <= tq
    gw = NSA_REP * HEAD_DIM
    k_col, v_col = 2 * NSA_KV_HEADS, 3 * NSA_KV_HEADS
    q_tile = lambda col: (lambda bi, g, i, sl: (bi, i, col + g))
    kv_prev = lambda col: (lambda bi, g, i, sl: (bi, jnp.maximum(i - 1, 0), col + g))
    grid_spec = pltpu.PrefetchScalarGridSpec(
        num_scalar_prefetch=1,
        grid=(b, NSA_KV_HEADS, t // tq),
        in_specs=[pl.BlockSpec((1, tq, gw), q_tile(Q_OFF // gw)),
                  pl.BlockSpec((1, tq, AUG_DIM), kv_prev(k_col)),
                  pl.BlockSpec((1, tq, AUG_DIM), kv_prev(v_col)),
                  pl.BlockSpec((1, tq, AUG_DIM), q_tile(k_col)),
                  pl.BlockSpec((1, tq, AUG_DIM), q_tile(v_col)),
                  pl.BlockSpec((1, HEAD_DIM), lambda bi, g, i, sl: (0, 0)),
                  pl.BlockSpec((1, tq, LANES), lambda bi, g, i, sl: (bi, i, 0)),
                  pl.BlockSpec((None, None, LANES, gw), lambda bi, g, i, sl: (2, g, 0, 0)),
                  pl.BlockSpec((1, tq, gw), q_tile(0))],
        out_specs=[pl.BlockSpec((1, tq, gw), q_tile(0)), pl.BlockSpec((1, tq, LANES), q_tile(0))],
    )
    return pl.pallas_call(
        functools.partial(_win_kernel, tq=tq),
        grid_spec=grid_spec,
        out_shape=[jax.ShapeDtypeStruct((b, t, NSA_WIDTH), jnp.bfloat16),
                   jax.ShapeDtypeStruct((b, t, NSA_KV_HEADS * LANES), jnp.float32)],
        compiler_params=_cparams(("parallel", "parallel", "parallel")),
        name="window_attention",
    )(jnp.asarray(_alibi_slopes()), z3, kv3, kv3, kv3, kv3, q_gain.reshape(1, HEAD_DIM), gate_logits,
      jnp.asarray(_gate_expand_matrix(), dtype=jnp.bfloat16), prev)


def kernel(x, p, mix_norm, w_in, conv_w, q_gain, k_gain, cmp_pos, cmp_w1, cmp_w2, sgu_gain, w_sp,
           b_sp, mix_out_norm, w_out, mlp_norm, w_mlp_in, w_mlp_out, ple_norm, w_ple_proj,
           w_ple_gate):
    b, t, d = x.shape
    m = b * t
    bf = jnp.bfloat16
    depth = p.shape[0]
    p_rows = p.reshape(depth, m, PLE_DIM)
    w_in_t = jnp.swapaxes(w_in, 1, 2)
    ones = jnp.ones((D_FF,), jnp.float32)
    h2 = x.reshape(m, d)
    hb, ssq = stats_cast(h2)
    for i in range(depth):
        g_in = mix_norm[i]
        z_a = matmul(hb, cast_weight_t(w_in_t, i, g_in, 0, Z_A_COLS, transpose=False), ssq=ssq,
                     w_rows_are_outputs=True).reshape(b, t, Z_A_COLS)
        z_g = matmul(hb, cast_weight_t(w_in_t, i, g_in, Z_G_OFF, Z_G_COLS, transpose=False), ssq=ssq,
                     w_rows_are_outputs=True).reshape(b, t, Z_G_COLS)
        w_gate = cast_weight_t(w_in_t, i, g_in, Z_A_COLS, LANES, keep=N_GATES)
        gate_logits = matmul(hb, w_gate, ssq=ssq, tn=LANES).reshape(b, t, LANES)

        on = mix_out_norm[i]
        mixed_a = conv_mixer(z_a, conv_w[i], on[:CONV_WIDTH])
        kc, vc = compress_kv(z_a, cmp_pos[i], cmp_w1[i], cmp_w2[i], k_gain[i])
        o_b, sel, used = cmp_attention(z_a, kc, vc, q_gain[i], gate_logits)
        kv3 = kv_prep(z_a, k_gain[i])
        o_b = selected_attention(z_a, kv3, q_gain[i], gate_logits, o_b, sel, used)
        o_b, ssq_b = window_attention(z_a, kv3, q_gain[i], gate_logits, o_b)
        mixed_c = gmlp_mixer(z_g, sgu_gain[i], w_sp[i], b_sp[i], on[CONV_WIDTH + NSA_WIDTH:])
        g_out = jnp.concatenate([ones[:CONV_WIDTH], on[CONV_WIDTH:CONV_WIDTH + NSA_WIDTH],
                                 ones[:GMLP_WIDTH]])
        mixed = (mixed_a.reshape(m, CONV_WIDTH), o_b.reshape(m, NSA_WIDTH),
                 mixed_c.reshape(m, GMLP_WIDTH))
        h2, hb, ssq = matmul(mixed, cast_weight(w_out, i, g_out), mode="residual", res=h2,
                             ssq=(None, ssq_b.reshape(m, NSA_KV_HEADS * LANES), None),
                             stats=True, tn=512)

        hid = matmul(hb, cast_weight(w_mlp_in, i, mlp_norm[i]), mode="relu2", out_dtype=bf, ssq=ssq)
        h2 = matmul(hid, cast_weight(w_mlp_out, i, ones), mode="residual", res=h2)
        hb, ssq = stats_cast(h2)

        out = matmul(hb, cast_weight(w_ple_gate, i, ple_norm[i]), mode="ple", res=h2, p=p_rows,
                     wp=w_ple_proj, layer=i, ssq=ssq, stats=i + 1 < depth, tn=512)
        h2, hb, ssq = out if i + 1 < depth else (out, None, None)
    return h2.reshape(b, t, d)
```

```python
import functools

import jax
import jax.numpy as jnp
import numpy as np
from jax import lax
from jax.experimental import pallas as pl
from jax.experimental.pallas import tpu as pltpu

D_MODEL = 4096
HEAD_DIM = 128
CONV_WIDTH = D_MODEL // 4
CONV_K = 3
NSA_HEADS = (D_MODEL // 2) // HEAD_DIM
NSA_WIDTH = NSA_HEADS * HEAD_DIM
NSA_KV_HEADS = 4
NSA_REP = NSA_HEADS // NSA_KV_HEADS
NSA_BRANCHES = 3
KV_WIDTH = NSA_KV_HEADS * HEAD_DIM
CMP_LEN = 32
CMP_STRIDE = 16
SLC_BLOCK = 64
SLC_SHIFT = 6
N_SELECT = 16
WINDOW = 512
GMLP_WIDTH = D_MODEL // 4
GMLP_GROUPS = GMLP_WIDTH // HEAD_DIM
GMLP_CHUNK = 128
D_FF = 4 * D_MODEL
PLE_DIM = 256
N_GATES = NSA_HEADS * NSA_BRANCHES
EPS = 1e-6
NEG_INF = -1e30
FORCE_SCORE = 1e9
BELOW_NEG_INF = -3e38

Z_A_COLS = 3 * CONV_WIDTH + NSA_WIDTH + 2 * NSA_BRANCHES * KV_WIDTH
Z_G_COLS = 2 * GMLP_WIDTH
Z_G_OFF = Z_A_COLS + N_GATES
Q_OFF = 3 * CONV_WIDTH
KCMP_OFF = Q_OFF + NSA_WIDTH
VCMP_OFF = KCMP_OFF + KV_WIDTH
KV4_OFF = VCMP_OFF + KV_WIDTH

LANES = 128
VMEM_LIMIT_BYTES = 60 * 1024 * 1024


def _cparams(dims):
    return pltpu.CompilerParams(dimension_semantics=dims, vmem_limit_bytes=VMEM_LIMIT_BYTES)


def _rms(x, gain):
    return x * lax.rsqrt(jnp.mean(x * x, axis=-1, keepdims=True) + EPS) * gain


def _gelu(x):
    return jax.nn.gelu(x)


def _row_scale(ssq, width, n_groups=1):
    tot = ssq[:, 0:LANES]
    for gi in range(1, n_groups):
        tot = tot + ssq[:, gi * LANES:(gi + 1) * LANES]
    return lax.rsqrt(tot * (1.0 / width) + EPS)


def _lane_sum_sq(x):
    return jnp.broadcast_to(jnp.sum(x * x, axis=-1, keepdims=True), (x.shape[0], LANES))


def _stats_kernel(x_ref, hb_ref, ssq_ref):
    x = x_ref[...]
    hb_ref[...] = x.astype(hb_ref.dtype)
    ssq_ref[...] = _lane_sum_sq(x)


def stats_cast(x, tm=512):
    m, d = x.shape
    return pl.pallas_call(
        _stats_kernel,
        grid=(m // tm,),
        in_specs=[pl.BlockSpec((tm, d), lambda i: (i, 0))],
        out_specs=[pl.BlockSpec((tm, d), lambda i: (i, 0)),
                   pl.BlockSpec((tm, LANES), lambda i: (i, 0))],
        out_shape=[jax.ShapeDtypeStruct((m, d), jnp.bfloat16),
                   jax.ShapeDtypeStruct((m, LANES), jnp.float32)],
        compiler_params=_cparams(("parallel",)),
        name="stats_cast",
    )(x)


def _cast_kernel(w_ref, g_ref, o_ref):
    o_ref[...] = (w_ref[...] * g_ref[...]).astype(o_ref.dtype)


def _cast_t_kernel(a_ref, *refs, shift, keep, transpose):
    g_ref, o_ref = refs[-2:]
    x = a_ref[...]
    if shift:
        x = jnp.concatenate([x[shift:, :], refs[0][:shift, :]], axis=0)
    if keep < x.shape[0]:
        row = lax.broadcasted_iota(jnp.int32, x.shape, 0)
        x = jnp.where(row < keep, x, 0.0)
    x = x * g_ref[...]
    o_ref[...] = (x.T if transpose else x).astype(o_ref.dtype)


def cast_weight(w, layer, gain, tk=1024, tn=4096):
    _, kdim, n = w.shape
    tk, tn = min(tk, kdim), min(tn, n)
    assert kdim % tk == 0 and n % tn == 0
    return pl.pallas_call(
        _cast_kernel,
        grid=(kdim // tk, n // tn),
        in_specs=[pl.BlockSpec((None, tk, tn), lambda i, j: (layer, i, j)),
                  pl.BlockSpec((tk, 1), lambda i, j: (i, 0))],
        out_specs=pl.BlockSpec((tk, tn), lambda i, j: (i, j)),
        out_shape=jax.ShapeDtypeStruct((kdim, n), jnp.bfloat16),
        compiler_params=_cparams(("parallel", "parallel")),
        name="cast_weight",
    )(w, gain.reshape(kdim, 1))


def cast_weight_t(w_t, layer, gain, row0, nrows, keep=None, transpose=True, tr=512, tc=2048):
    _, n, kdim = w_t.shape
    tr, tc = min(tr, nrows), min(tc, kdim)
    shift = row0 % tr
    base = row0 // tr
    assert nrows % tr == 0 and kdim % tc == 0 and shift % 8 == 0
    in_specs = [pl.BlockSpec((None, tr, tc), lambda i, j: (layer, base + j, i))]
    if shift:
        head = 8 * pl.cdiv(shift, 8)
        while tr % head:
            head += 8
        in_specs.append(pl.BlockSpec((None, head, tc), lambda i, j: (layer, (base + j + 1) * (tr // head), i)))
    return pl.pallas_call(
        functools.partial(_cast_t_kernel, shift=shift, keep=keep or nrows, transpose=transpose),
        grid=(kdim // tc, nrows // tr),
        in_specs=in_specs + [pl.BlockSpec((1, tc), lambda i, j: (0, i))],
        out_specs=(pl.BlockSpec((tc, tr), lambda i, j: (i, j)) if transpose
                   else pl.BlockSpec((tr, tc), lambda i, j: (j, i))),
        out_shape=jax.ShapeDtypeStruct((kdim, nrows) if transpose else (nrows, kdim), jnp.bfloat16),
        compiler_params=_cparams(("parallel", "parallel")),
        name="cast_weight_t",
    )(*([w_t] * (2 if shift else 1)), gain.reshape(1, kdim))


def _mm_kernel(*refs, mode, nk, n_x, scale_groups, stats, w_rows_are_outputs):
    x_refs, refs = refs[:n_x], refs[n_x:]
    refs = list(refs)
    take = lambda cond: refs.pop(0) if cond else None
    w_ref = refs.pop(0)
    res_ref = take(mode in ("residual", "ple"))
    p_ref, wp_ref = take(mode == "ple"), take(mode == "ple")
    ssq_refs = [take(g > 0) for g in scale_groups]
    o_ref = refs.pop(0)
    hb_ref, ssq_out_ref = take(stats), take(stats)
    tn = o_ref.shape[1]
    j = pl.program_id(1)
    k = pl.program_id(2)

    def scaled(d, n):
        if ssq_refs[n] is None:
            return d
        s = _row_scale(ssq_refs[n][...], x_refs[n].shape[1] * (nk if len(x_refs) == 1 else 1),
                       scale_groups[n])
        return d * jnp.tile(s, (1, tn // LANES))

    def finish(new):
        o_ref[...] = new
        if stats:
            hb_ref[...] = new.astype(hb_ref.dtype)
            ssq = _lane_sum_sq(new)

            @pl.when(j == 0)
            def _():
                ssq_out_ref[...] = ssq

            @pl.when(j > 0)
            def _():
                ssq_out_ref[...] += ssq

    def epilogue(acc):
        if mode == "plain":
            o_ref[...] = acc.astype(o_ref.dtype)
        elif mode == "relu2":
            r = jnp.maximum(acc, 0.0)
            o_ref[...] = (r * r).astype(o_ref.dtype)
        elif mode == "residual":
            finish(res_ref[...] + acc)
        else:
            proj = jnp.dot(p_ref[...].astype(jnp.bfloat16), wp_ref[...].astype(jnp.bfloat16),
                           preferred_element_type=jnp.float32)
            finish(res_ref[...] + jax.nn.sigmoid(acc) * proj)

    def product():
        part, off = None, 0
        for n, x_ref in enumerate(x_refs):
            kx = x_ref.shape[1]
            if w_rows_are_outputs:
                d = lax.dot_general(x_ref[...], w_ref[:, off:off + kx], (((1,), (1,)), ((), ())),
                                    preferred_element_type=jnp.float32)
            else:
                d = jnp.dot(x_ref[...], w_ref[off:off + kx, :], preferred_element_type=jnp.float32)
            d = scaled(d, n)
            part = d if part is None else part + d
            off += kx
        return part

    if nk == 1:
        epilogue(product())
        return
    if mode == "residual":
        @pl.when(k == 0)
        def _():
            o_ref[...] = res_ref[...] + product()

        @pl.when(jnp.logical_and(k > 0, k < nk - 1))
        def _():
            o_ref[...] += product()

        @pl.when(k == nk - 1)
        def _():
            finish(o_ref[...] + product())
        return
    acc_ref, = refs

    @pl.when(k == 0)
    def _():
        acc_ref[...] = product()

    @pl.when(jnp.logical_and(k > 0, k < nk - 1))
    def _():
        acc_ref[...] += product()

    @pl.when(k == nk - 1)
    def _():
        epilogue(acc_ref[...] + product())


def matmul(x, w, mode="plain", out_dtype=jnp.float32, res=None, p=None, wp=None, layer=0,
           ssq=None, stats=False, w_rows_are_outputs=False, tm=1024, tn=1024, tk=4096):
    xs = x if isinstance(x, (tuple, list)) else (x,)
    ssqs = ssq if isinstance(ssq, (tuple, list)) else (ssq,) * len(xs)
    m = xs[0].shape[0]
    kdim, n = w.shape[::-1] if w_rows_are_outputs else w.shape
    assert sum(xi.shape[1] for xi in xs) == kdim and len(ssqs) == len(xs)
    tm, tn, tk = min(tm, m), min(tn, n), min(tk, kdim)
    nk = kdim // tk
    assert m % tm == 0 and n % tn == 0 and kdim % tk == 0 and (nk == 1 or len(xs) == 1)
    assert not stats or mode in ("residual", "ple")
    in_specs = [pl.BlockSpec((tm, xi.shape[1] if len(xs) > 1 else tk), lambda i, j, k: (i, k))
                for xi in xs]
    in_specs.append(pl.BlockSpec((tn, tk), lambda i, j, k: (j, k)) if w_rows_are_outputs
                    else pl.BlockSpec((tk, tn), lambda i, j, k: (k, j)))
    args = [*xs, w]
    if mode in ("residual", "ple"):
        in_specs.append(pl.BlockSpec((tm, tn), lambda i, j, k: (i, j)))
        args.append(res)
    if mode == "ple":
        pd = p.shape[2]
        in_specs += [pl.BlockSpec((None, tm, pd), lambda i, j, k: (layer, i, 0)),
                     pl.BlockSpec((None, pd, tn), lambda i, j, k: (layer, 0, j))]
        args += [p, wp]
    scale_groups = tuple(0 if s is None else s.shape[1] // LANES for s in ssqs)
    for s in ssqs:
        if s is not None:
            in_specs.append(pl.BlockSpec((tm, s.shape[1]), lambda i, j, k: (i, 0)))
            args.append(s)
    tile = pl.BlockSpec((tm, tn), lambda i, j, k: (i, j))
    out_specs, out_shape = tile, jax.ShapeDtypeStruct((m, n), out_dtype)
    if stats:
        out_specs = [tile, tile, pl.BlockSpec((tm, LANES), lambda i, j, k: (i, 0))]
        out_shape = [out_shape, jax.ShapeDtypeStruct((m, n), jnp.bfloat16),
                     jax.ShapeDtypeStruct((m, LANES), jnp.float32)]
    scratch = [pltpu.VMEM((tm, tn), jnp.float32)] if nk > 1 and mode != "residual" else []
    return pl.pallas_call(
        functools.partial(_mm_kernel, mode=mode, nk=nk, n_x=len(xs), scale_groups=scale_groups,
                          stats=stats, w_rows_are_outputs=w_rows_are_outputs),
        grid=(m // tm, n // tn, nk),
        in_specs=in_specs,
        out_specs=out_specs,
        out_shape=out_shape,
        scratch_shapes=scratch,
        compiler_params=_cparams(("parallel", "arbitrary" if stats else "parallel", "arbitrary")),
        name="matmul_" + mode,
    )(*args)


HALO_ROWS = 8


def _conv_kernel(xa_ref, gb_ref, gc_ref, xap_ref, gcp_ref, cw_ref, on_ref, o_ref):
    i = pl.program_id(1)
    inner = gc_ref[0] * xa_ref[0]
    prev = gcp_ref[0] * xap_ref[0]
    prev = jnp.where(i > 0, prev, 0.0)
    tt = inner.shape[0]
    row = lax.broadcasted_iota(jnp.int32, inner.shape, 0)
    m1 = jnp.where(row == 0, prev[HALO_ROWS - 1:HALO_ROWS], pltpu.roll(inner, 1, 0))
    m2 = pltpu.roll(inner, 2, 0)
    m2 = jnp.where(row == 0, prev[HALO_ROWS - 2:HALO_ROWS - 1], m2)
    m2 = jnp.where(row == 1, prev[HALO_ROWS - 1:HALO_ROWS], m2)
    cw = cw_ref[...]
    conv = cw[0:1] * m2 + cw[1:2] * m1 + cw[2:3] * inner
    o_ref[0] = _rms(gb_ref[0] * conv, on_ref[...]).astype(o_ref.dtype)


def conv_mixer(z3, conv_w, on_gain, tt=1024):
    b, t, _ = z3.shape
    c = CONV_WIDTH
    hb = tt // HALO_ROWS
    cur = lambda col: pl.BlockSpec((1, tt, c), lambda bi, i: (bi, i, col))
    halo = lambda col: pl.BlockSpec((1, HALO_ROWS, c),
                                    lambda bi, i: (bi, jnp.maximum(i * hb - 1, 0), col))
    return pl.pallas_call(
        _conv_kernel,
        grid=(b, t // tt),
        in_specs=[cur(0), cur(1), cur(2), halo(0), halo(2),
                  pl.BlockSpec((HALO_ROWS, c), lambda bi, i: (0, 0)),
                  pl.BlockSpec((1, c), lambda bi, i: (0, 0))],
        out_specs=pl.BlockSpec((1, tt, c), lambda bi, i: (bi, i, 0)),
        out_shape=jax.ShapeDtypeStruct((b, t, c), jnp.bfloat16),
        compiler_params=_cparams(("parallel", "parallel")),
        name="conv_mixer",
    )(z3, z3, z3, z3, z3,
      jnp.pad(conv_w, ((0, HALO_ROWS - CONV_K), (0, 0))), on_gain.reshape(1, c))


def _gmlp_kernel(gu_ref, gv_ref, sg_ref, wsp_ref, bexp_ref, on_ref, o_ref, *, n_chunks):
    u = _gelu(gu_ref[0])
    v = _rms(_gelu(gv_ref[0]), sg_ref[...]).astype(jnp.bfloat16)
    ck = GMLP_CHUNK
    r_i = lax.broadcasted_iota(jnp.int32, (ck, ck), 0)
    c_i = lax.broadcasted_iota(jnp.int32, (ck, ck), 1)
    bexp = bexp_ref[...]
    rows = []
    for c in range(n_chunks):
        cols = []
        for g in range(GMLP_GROUPS):
            w = jnp.where(r_i >= c_i, wsp_ref[g], 0.0).astype(jnp.bfloat16)
            vg = v[c * ck:(c + 1) * ck, g * HEAD_DIM:(g + 1) * HEAD_DIM]
            cols.append(jnp.dot(w, vg, preferred_element_type=jnp.float32))
        rows.append(jnp.concatenate(cols, axis=1) + bexp)
    spatial = jnp.concatenate(rows, axis=0) if n_chunks > 1 else rows[0]
    o_ref[0] = _rms(u * spatial, on_ref[...]).astype(o_ref.dtype)


def gmlp_mixer(z3, sgu_gain, w_sp, b_sp, on_gain, tt=512):
    b, t, _ = z3.shape
    c = GMLP_WIDTH
    bexp = jnp.repeat(b_sp.T, HEAD_DIM, axis=1)
    const2 = lambda shape: pl.BlockSpec(shape, lambda bi, i: (0,) * len(shape))
    return pl.pallas_call(
        functools.partial(_gmlp_kernel, n_chunks=tt // GMLP_CHUNK),
        grid=(b, t // tt),
        in_specs=[pl.BlockSpec((1, tt, c), lambda bi, i: (bi, i, 0)),
                  pl.BlockSpec((1, tt, c), lambda bi, i: (bi, i, 1)),
                  const2((1, c)), const2((GMLP_GROUPS, GMLP_CHUNK, GMLP_CHUNK)),
                  const2((GMLP_CHUNK, c)), const2((1, c))],
        out_specs=pl.BlockSpec((1, tt, c), lambda bi, i: (bi, i, 0)),
        out_shape=jax.ShapeDtypeStruct((b, t, c), jnp.bfloat16),
        compiler_params=_cparams(("parallel", "parallel")),
        name="gmlp_mixer",
    )(z3, z3, sgu_gain.reshape(1, c), w_sp, bexp, on_gain.reshape(1, c))


ATT_TILE = 512
KV_TILE_SHIFT = 9
AUG_DIM = 2 * HEAD_DIM
POS_LANE = 64
ONE_LANE = 70
FEAT_END = 73
MASK_BIG = 2.0 ** 100
MASKED = -MASK_BIG
M_INIT = NEG_INF
LOG2E = 1.4426950408889634


def _key_features(pos, with_block_onehot):
    lane = lax.broadcasted_iota(jnp.int32, pos.shape, 1)
    blk = lax.shift_right_logical(pos, SLC_SHIFT)
    hi = lax.shift_left(blk, SLC_SHIFT).astype(jnp.float32)
    lo = jnp.bitwise_and(pos, SLC_BLOCK - 1).astype(jnp.float32)
    posf = jnp.where(jnp.bitwise_and(lane, 1) == 0, hi, lo)
    tail = jnp.where(lane < ONE_LANE, posf, jnp.where(lane < FEAT_END, 1.0, 0.0))
    head = jnp.where(lane == blk, MASK_BIG, 0.0) if with_block_onehot else 0.0
    return jnp.where(lane < POS_LANE, head, tail)


def _kvprep_kernel(z_ref, kg_ref, o_ref, *, tt):
    i = pl.program_id(1)
    z = z_ref[0]
    pos = i * tt + lax.broadcasted_iota(jnp.int32, (tt, HEAD_DIM), 0)
    kfeat = (_key_features(pos, True).astype(o_ref.dtype), _key_features(pos, False).astype(o_ref.dtype))
    vfeat = jnp.ones((tt, HEAD_DIM), o_ref.dtype)
    for part in range(4):
        for g in range(NSA_KV_HEADS):
            src = part * KV_WIDTH + g * HEAD_DIM
            dst = (part * NSA_KV_HEADS + g) * AUG_DIM
            x = z[:, src:src + HEAD_DIM]
            if part % 2 == 0:
                x = _rms(x, kg_ref[1 + part // 2:2 + part // 2, :])
            o_ref[0, :, dst:dst + HEAD_DIM] = x.astype(o_ref.dtype)
            o_ref[0, :, dst + HEAD_DIM:dst + AUG_DIM] = kfeat[part // 2] if part % 2 == 0 else vfeat


def kv_prep(z3, k_gain, tt=512):
    b, t, _ = z3.shape
    w = 4 * KV_WIDTH
    wo = 4 * NSA_KV_HEADS * AUG_DIM
    kg = jnp.pad(k_gain, ((0, 8 - NSA_BRANCHES), (0, 0)))
    return pl.pallas_call(
        functools.partial(_kvprep_kernel, tt=tt),
        grid=(b, t // tt),
        in_specs=[pl.BlockSpec((1, tt, w), lambda bi, i: (bi, i, KV4_OFF // w)),
                  pl.BlockSpec((8, HEAD_DIM), lambda bi, i: (0, 0))],
        out_specs=pl.BlockSpec((1, tt, wo), lambda bi, i: (bi, i, 0)),
        out_shape=jax.ShapeDtypeStruct((b, t, wo), jnp.bfloat16),
        compiler_params=_cparams(("parallel", "parallel")),
        name="kv_prep",
    )(z3, kg)


def _compress_kernel(k_ref, v_ref, w1_ref, w2_ref, pe_ref, kg_ref, kc_ref, vc_ref, *, n_chunk):
    def one(src_ref, which):
        ya = jnp.zeros((n_chunk, HEAD_DIM), jnp.float32)
        yb = jnp.zeros((n_chunk, HEAD_DIM), jnp.float32)
        bias = jnp.zeros((8, HEAD_DIM), jnp.float32)
        for l in range(CMP_STRIDE):
            xl = src_ref[0, pl.ds(l, n_chunk, stride=CMP_STRIDE), :].astype(jnp.bfloat16)
            ya += jnp.dot(xl, w1_ref[which, l], preferred_element_type=jnp.float32)
            yb += jnp.dot(xl, w1_ref[which, CMP_STRIDE + l], preferred_element_type=jnp.float32)
        for l in range(CMP_LEN):
            pe_l = jnp.broadcast_to(pe_ref[which, l:l + 1, :], (8, HEAD_DIM)).astype(jnp.bfloat16)
            bias += jnp.dot(pe_l, w1_ref[which, l], preferred_element_type=jnp.float32)
        hid = _gelu(ya + pltpu.roll(yb, n_chunk - 1, 0) + bias[0:1])
        return jnp.dot(hid.astype(jnp.bfloat16), w2_ref[which], preferred_element_type=jnp.float32)

    kc_ref[0, 0, :, 0:HEAD_DIM] = _rms(one(k_ref, 0), kg_ref[0:1, :]).astype(kc_ref.dtype)
    n = lax.broadcasted_iota(jnp.int32, (n_chunk, HEAD_DIM), 0)
    kc_ref[0, 0, :, HEAD_DIM:AUG_DIM] = _key_features(n * CMP_STRIDE + CMP_LEN - 1, False).astype(kc_ref.dtype)
    vc_ref[0, 0] = one(v_ref, 1).astype(vc_ref.dtype)


def compress_kv(z3, cmp_pos, cmp_w1, cmp_w2, k_gain):
    b, t, _ = z3.shape
    n_chunk = t // CMP_STRIDE
    kg = jnp.pad(k_gain, ((0, 8 - NSA_BRANCHES), (0, 0)))
    const = lambda shape: pl.BlockSpec(shape, lambda bi, g: (0,) * len(shape))
    out = lambda width: jax.ShapeDtypeStruct((b, NSA_KV_HEADS, n_chunk, width), jnp.bfloat16)
    tile = lambda width: pl.BlockSpec((1, 1, n_chunk, width), lambda bi, g: (bi, g, 0, 0))
    return pl.pallas_call(
        functools.partial(_compress_kernel, n_chunk=n_chunk),
        grid=(b, NSA_KV_HEADS),
        in_specs=[pl.BlockSpec((1, t, HEAD_DIM), lambda bi, g: (bi, 0, KCMP_OFF // HEAD_DIM + g)),
                  pl.BlockSpec((1, t, HEAD_DIM), lambda bi, g: (bi, 0, VCMP_OFF // HEAD_DIM + g)),
                  const((2, CMP_LEN, HEAD_DIM, HEAD_DIM)), const((2, HEAD_DIM, HEAD_DIM)),
                  const((2, CMP_LEN, HEAD_DIM)), const((8, HEAD_DIM))],
        out_specs=[tile(AUG_DIM), tile(HEAD_DIM)],
        out_shape=[out(AUG_DIM), out(HEAD_DIM)],
        compiler_params=_cparams(("parallel", "parallel")),
        name="compress_kv",
    )(z3, z3, cmp_w1.astype(jnp.bfloat16), cmp_w2.astype(jnp.bfloat16), cmp_pos, kg)


def _alibi_slopes():
    h = np.arange(1, NSA_HEADS + 1, dtype=np.float32)
    return np.power(np.float32(2.0), -8.0 * h / NSA_HEADS).astype(np.float32)


def _overlap_matrix(n_chunk, n_slc):
    n_cmp = n_chunk - CMP_LEN // CMP_STRIDE + 1
    jc = np.arange(n_cmp)[:, None] * CMP_STRIDE
    bs = np.arange(n_slc)[None, :] * SLC_BLOCK
    ov = np.clip(np.minimum(jc + CMP_LEN, bs + SLC_BLOCK) - np.maximum(jc, bs), 0, None)
    out = np.zeros((n_chunk, LANES), np.float32)
    out[:n_cmp, :n_slc] = ov.astype(np.float32) / CMP_LEN
    return out


def _gate_expand_matrix():
    m = np.zeros((NSA_BRANCHES, NSA_KV_HEADS, LANES, NSA_REP * HEAD_DIM), np.float32)
    for br in range(NSA_BRANCHES):
        for g in range(NSA_KV_HEADS):
            for r in range(NSA_REP):
                m[br, g, (g * NSA_REP + r) * NSA_BRANCHES + br, r * HEAD_DIM:(r + 1) * HEAD_DIM] = 1.0
    return m


def _branch_gate(gl_ref, ex_ref):
    sig = jax.nn.sigmoid(gl_ref[0])
    hi = sig.astype(jnp.bfloat16)
    lo = (sig - hi.astype(jnp.float32)).astype(jnp.bfloat16)
    ex = ex_ref[...]
    return (jnp.dot(hi, ex, preferred_element_type=jnp.float32)
            + jnp.dot(lo, ex, preferred_element_type=jnp.float32))


def _cmp_kernel(slopes_ref, q_ref, kc_ref, vc_ref, qg_ref, ov_ref, gl_ref, ex_ref, o_ref, sel_ref,
                used_ref, *, tq, n_chunk, n_slc):
    g = pl.program_id(1)
    i = pl.program_id(2)
    gate = _branch_gate(gl_ref, ex_ref)
    n_cmp = n_chunk - CMP_LEN // CMP_STRIDE + 1
    kc = kc_ref[0, 0]
    vc = vc_ref[0, 0]
    t = i * tq + lax.broadcasted_iota(jnp.int32, (tq, n_chunk), 0)
    n = lax.broadcasted_iota(jnp.int32, (tq, n_chunk), 1)
    mask = jnp.logical_and(t >= n * CMP_STRIDE + CMP_LEN - 1, n < n_cmp)
    psum = jnp.zeros((tq, n_chunk), jnp.float32)
    for r, (qn, feat) in enumerate(_q_aug(q_ref, qg_ref, slopes_ref, g, i * tq, None)):
        s2 = lax.dot_general(jnp.concatenate([qn, feat], axis=1), kc, (((1,), (1,)), ((), ())),
                             preferred_element_type=jnp.float32)
        s2 = jnp.where(mask, s2, MASKED)
        mx = jnp.maximum(jnp.max(s2, axis=-1, keepdims=True), M_INIT)
        e = jnp.exp2(s2 - mx)
        den = jnp.sum(e, axis=-1, keepdims=True)
        p = e / jnp.where(den > 0.0, den, 1.0)
        psum += p
        o_ref[0, :, r * HEAD_DIM:(r + 1) * HEAD_DIM] = gate[:, r * HEAD_DIM:(r + 1) * HEAD_DIM] * jnp.dot(
            p.astype(jnp.bfloat16), vc, preferred_element_type=jnp.float32)

    ov = ov_ref[...]
    imp = jnp.zeros((tq, LANES), jnp.float32)
    rem = psum
    for _ in range(3):
        part = rem.astype(jnp.bfloat16)
        imp += jnp.dot(part, ov, preferred_element_type=jnp.float32)
        rem = rem - part.astype(jnp.float32)

    tb = i * tq + lax.broadcasted_iota(jnp.int32, (tq, LANES), 0)
    blk = lax.broadcasted_iota(jnp.int32, (tq, LANES), 1)
    qblk = lax.shift_right_logical(tb, SLC_SHIFT)
    forced = (blk == 0) | (blk == qblk) | (blk == qblk - 1)
    score = jnp.where(forced, FORCE_SCORE, jnp.where(blk <= qblk, imp, NEG_INF))
    score = jnp.where(blk < n_slc, score, BELOW_NEG_INF)
    st = score.T[0:n_slc]
    n_tiles = n_slc // 8
    tiles = [st[8 * j:8 * j + 8] for j in range(n_tiles)]
    sub8 = lax.broadcasted_iota(jnp.int32, (8, tq), 0)
    cnt = [jnp.zeros((8, tq), jnp.float32) for _ in range(n_tiles)]
    for m in range(n_slc):
        row = st[m:m + 1, :]
        own, off = divmod(m, 8)
        for j in range(n_tiles):
            if j < own:
                inc = jnp.where(row > tiles[j], 1.0, 0.0)
            elif j > own:
                inc = jnp.where(row >= tiles[j], 1.0, 0.0)
            else:
                later = jnp.where(sub8 > off, 1.0, 0.0)
                inc = (jnp.where(row >= tiles[j], later, 0.0)
                       + jnp.where(row > tiles[j], 1.0 - later, 0.0))
            cnt[j] = cnt[j] + inc
    keep = float(min(N_SELECT, n_slc))
    sel_t = jnp.concatenate([jnp.where(c < keep, 1.0, 0.0) for c in cnt]
                            + [jnp.zeros((LANES - n_slc, tq), jnp.float32)] * (n_slc < LANES), axis=0)
    sel = sel_t.T
    sel_ref[0, 0] = sel.astype(sel_ref.dtype)
    picked = jnp.broadcast_to(jnp.max(sel, axis=0, keepdims=True), (8, LANES)).astype(jnp.bfloat16)
    blk_i = lax.broadcasted_iota(jnp.int32, (LANES, LANES), 0)
    tile_i = lax.broadcasted_iota(jnp.int32, (LANES, LANES), 1)
    group = jnp.where(lax.shift_right_logical(blk_i, KV_TILE_SHIFT - SLC_SHIFT) == tile_i, 1.0, 0.0)
    used = jnp.dot(picked, group.astype(jnp.bfloat16), preferred_element_type=jnp.float32)
    used_ref[0, 0] = jnp.where(used > 0.5, 1, 0).astype(jnp.int32)


def cmp_attention(z3, kc, vc, q_gain, gate_logits):
    b, t, _ = z3.shape
    tq = min(ATT_TILE, t)
    n_chunk = t // CMP_STRIDE
    n_slc = t // SLC_BLOCK
    assert n_slc <= LANES
    gw = NSA_REP * HEAD_DIM
    grid_spec = pltpu.PrefetchScalarGridSpec(
        num_scalar_prefetch=1,
        grid=(b, NSA_KV_HEADS, t // tq),
        in_specs=[pl.BlockSpec((1, tq, gw), lambda bi, g, i, s: (bi, i, Q_OFF // gw + g)),
                  pl.BlockSpec((1, 1, n_chunk, AUG_DIM), lambda bi, g, i, s: (bi, g, 0, 0)),
                  pl.BlockSpec((1, 1, n_chunk, HEAD_DIM), lambda bi, g, i, s: (bi, g, 0, 0)),
                  pl.BlockSpec((1, HEAD_DIM), lambda bi, g, i, s: (0, 0)),
                  pl.BlockSpec((n_chunk, LANES), lambda bi, g, i, s: (0, 0)),
                  pl.BlockSpec((1, tq, LANES), lambda bi, g, i, s: (bi, i, 0)),
                  pl.BlockSpec((None, None, LANES, gw), lambda bi, g, i, s: (0, g, 0, 0))],
        out_specs=[pl.BlockSpec((1, tq, gw), lambda bi, g, i, s: (bi, i, g)),
                   pl.BlockSpec((1, 1, tq, LANES), lambda bi, g, i, s: (bi, g, i, 0)),
                   pl.BlockSpec((1, 1, 8, LANES), lambda bi, g, i, s: (bi, g, i, 0))],
    )
    return pl.pallas_call(
        functools.partial(_cmp_kernel, tq=tq, n_chunk=n_chunk, n_slc=n_slc),
        grid_spec=grid_spec,
        out_shape=[jax.ShapeDtypeStruct((b, t, NSA_WIDTH), jnp.float32),
                   jax.ShapeDtypeStruct((b, NSA_KV_HEADS, t, LANES), jnp.bfloat16),
                   jax.ShapeDtypeStruct((b, NSA_KV_HEADS, t // tq * 8, LANES), jnp.int32)],
        compiler_params=_cparams(("parallel", "parallel", "parallel")),
        name="cmp_attention",
    )(jnp.asarray(_alibi_slopes()), z3, kc, vc, q_gain.reshape(1, HEAD_DIM),
      jnp.asarray(_overlap_matrix(n_chunk, n_slc), dtype=jnp.bfloat16), gate_logits,
      jnp.asarray(_gate_expand_matrix(), dtype=jnp.bfloat16))


STEP_FIRST, STEP_LAST, STEP_MASKED = 1, 2, 4


def _flash_steps(t, tq, tk):
    qi, kv, fl = [], [], []
    for i in range(t // tq):
        lo, hi = i * tq, i * tq + tq - 1
        n_tiles = hi // tk + 1
        for j in range(n_tiles):
            flag = (STEP_FIRST if j == 0 else 0) | (STEP_LAST if j == n_tiles - 1 else 0)
            flag |= 0 if j * tk + tk - 1 <= lo else STEP_MASKED
            qi.append(i), kv.append(j), fl.append(flag)
    return [np.asarray(a, np.int32) for a in (qi, kv, fl)]


def _split3(x):
    a = x.astype(jnp.bfloat16).astype(jnp.float32)
    b = (x - a).astype(jnp.bfloat16).astype(jnp.float32)
    c = (x - a - b).astype(jnp.bfloat16).astype(jnp.float32)
    return a, b, c


def _q_aug(q_ref, qg_ref, slopes_ref, g, t_start, sel_m1):
    q = q_ref[0]
    tq = q.shape[0]
    lane = lax.broadcasted_iota(jnp.int32, (tq, HEAD_DIM), 1)
    if sel_m1 is None:
        sel_m1 = jnp.zeros((tq, HEAD_DIM), jnp.float32)
    t0 = t_start.astype(jnp.float32)
    out = []
    for r in range(NSA_REP):
        qn = _rms(q[:, r * HEAD_DIM:(r + 1) * HEAD_DIM], qg_ref[...]) * (HEAD_DIM ** -0.5 * LOG2E)
        slope2 = slopes_ref[g * NSA_REP + r] * LOG2E
        a, b, c = _split3(jnp.where(lane < ONE_LANE, slope2, -slope2 * t0))
        k3 = jnp.where(lane < ONE_LANE, lax.shift_right_logical(lane - POS_LANE, 1), lane - ONE_LANE)
        term = jnp.where(k3 == 0, a, jnp.where(k3 == 1, b, c))
        feat = jnp.where(lane < POS_LANE, sel_m1, jnp.where(lane < FEAT_END, term, 0.0))
        out.append((qn.astype(jnp.bfloat16), feat.astype(jnp.bfloat16)))
    return out


def _slc_kernel(slopes_ref, qi_ref, kv_ref, fl_ref, used_ref, q_ref, k_ref, v_ref, qg_ref, gl_ref,
                ex_ref, prev_ref, sel_ref, o_ref, q_sc, m_sc, acc_sc, *, tq, tk):
    g = pl.program_id(1)
    s = pl.program_id(2)
    i = qi_ref[s]
    kv_idx = kv_ref[s]
    flags = fl_ref[s]

    @pl.when(jnp.bitwise_and(flags, STEP_FIRST) != 0)
    def _():
        sel_m1 = sel_ref[0, 0].astype(jnp.float32) - 1.0
        for r, (qn, feat) in enumerate(_q_aug(q_ref, qg_ref, slopes_ref, g, i * tq, sel_m1)):
            q_sc[r, :, 0:HEAD_DIM] = qn
            q_sc[r, :, HEAD_DIM:AUG_DIM] = feat
        m_sc[...] = jnp.full(m_sc.shape, M_INIT, jnp.float32)
        acc_sc[...] = jnp.zeros(acc_sc.shape, jnp.float32)

    def step(masked):
        k = k_ref[0]
        v = v_ref[0]
        if masked:
            t = i * tq + lax.broadcasted_iota(jnp.int32, (tq, tk), 0)
            ok = t >= kv_idx * tk + lax.broadcasted_iota(jnp.int32, (tq, tk), 1)
        for r in range(NSA_REP):
            s2 = lax.dot_general(q_sc[r], k, (((1,), (1,)), ((), ())),
                                 preferred_element_type=jnp.float32)
            if masked:
                s2 = jnp.where(ok, s2, MASKED)
            m_prev = m_sc[r]
            m_new = jnp.maximum(m_prev, jnp.max(s2, axis=-1, keepdims=True))
            p = jnp.exp2(s2 - jnp.tile(m_new, (1, tk // HEAD_DIM))).astype(jnp.bfloat16)
            alpha = jnp.exp2(m_prev - m_new)
            acc_sc[r] = jnp.tile(alpha, (1, AUG_DIM // HEAD_DIM)) * acc_sc[r] + jnp.dot(
                p, v, preferred_element_type=jnp.float32)
            m_sc[r] = m_new

    is_masked = jnp.bitwise_and(flags, STEP_MASKED) != 0
    bg = pl.program_id(0) * NSA_KV_HEADS + g
    live = used_ref[(bg * pl.num_programs(2) + s)] != 0
    pl.when(jnp.logical_and(live, jnp.logical_not(is_masked)))(functools.partial(step, False))
    pl.when(jnp.logical_and(live, is_masked))(functools.partial(step, True))

    @pl.when(jnp.bitwise_and(flags, STEP_LAST) != 0)
    def _():
        gate = _branch_gate(gl_ref, ex_ref)
        prev = prev_ref[0]
        for r in range(NSA_REP):
            acc = acc_sc[r]
            cols = slice(r * HEAD_DIM, (r + 1) * HEAD_DIM)
            o_ref[0, :, cols] = prev[:, cols] + gate[:, cols] * (
                acc[:, 0:HEAD_DIM] / acc[:, HEAD_DIM:AUG_DIM])


def selected_attention(z3, kv3, q_gain, gate_logits, prev, sel, used):
    b, t, _ = z3.shape
    tq = tk = min(ATT_TILE, t)
    gw = NSA_REP * HEAD_DIM
    qi, kv, fl = _flash_steps(t, tq, tk)
    used = used[:, :, ::8, :][:, :, qi, kv].reshape(-1)

    q_tile = lambda col: (lambda bi, g, s, sl, qi, *_: (bi, qi[s], col + g))
    kv_tile = lambda col: (lambda bi, g, s, sl, qi, kv, *_: (bi, kv[s], col + g))
    grid_spec = pltpu.PrefetchScalarGridSpec(
        num_scalar_prefetch=5,
        grid=(b, NSA_KV_HEADS, len(qi)),
        in_specs=[pl.BlockSpec((1, tq, gw), q_tile(Q_OFF // gw)),
                  pl.BlockSpec((1, tk, AUG_DIM), kv_tile(0)),
                  pl.BlockSpec((1, tk, AUG_DIM), kv_tile(NSA_KV_HEADS)),
                  pl.BlockSpec((1, HEAD_DIM), lambda bi, g, s, *_: (0, 0)),
                  pl.BlockSpec((1, tq, LANES), lambda bi, g, s, sl, qi, *_: (bi, qi[s], 0)),
                  pl.BlockSpec((None, None, LANES, gw), lambda bi, g, s, *_: (1, g, 0, 0)),
                  pl.BlockSpec((1, tq, gw), q_tile(0)),
                  pl.BlockSpec((1, 1, tq, LANES), lambda bi, g, s, sl, qi, *_: (bi, g, qi[s], 0))],
        out_specs=pl.BlockSpec((1, tq, gw), q_tile(0)),
        scratch_shapes=[pltpu.VMEM((NSA_REP, tq, AUG_DIM), jnp.bfloat16),
                        pltpu.VMEM((NSA_REP, tq, HEAD_DIM), jnp.float32),
                        pltpu.VMEM((NSA_REP, tq, AUG_DIM), jnp.float32)],
    )
    return pl.pallas_call(
        functools.partial(_slc_kernel, tq=tq, tk=tk),
        grid_spec=grid_spec,
        out_shape=jax.ShapeDtypeStruct((b, t, NSA_WIDTH), jnp.float32),
        compiler_params=_cparams(("parallel", "parallel", "arbitrary")),
        name="selected_attention",
    )(jnp.asarray(_alibi_slopes()), jnp.asarray(qi), jnp.asarray(kv), jnp.asarray(fl), used,
      z3, kv3, kv3, q_gain.reshape(1, HEAD_DIM), gate_logits,
      jnp.asarray(_gate_expand_matrix(), dtype=jnp.bfloat16), prev, sel)


def _win_kernel(slopes_ref, q_ref, kp_ref, vp_ref, kc_ref, vc_ref, qg_ref, gl_ref, ex_ref, prev_ref,
                o_ref, ssq_ref, *, tq):
    g = pl.program_id(1)
    i = pl.program_id(2)
    row = lax.broadcasted_iota(jnp.int32, (tq, tq), 0)
    col = lax.broadcasted_iota(jnp.int32, (tq, tq), 1)
    ok_cur = row >= col
    ok_prev = jnp.logical_and(row + tq - col < WINDOW, i > 0)
    gate = _branch_gate(gl_ref, ex_ref)
    prev = prev_ref[0]
    contract_last = (((1,), (1,)), ((), ()))
    ssq = jnp.zeros((tq, LANES), jnp.float32)
    for r, (qn, feat) in enumerate(_q_aug(q_ref, qg_ref, slopes_ref, g, i * tq, None)):
        qa = jnp.concatenate([qn, feat], axis=1)
        s_p = lax.dot_general(qa, kp_ref[0], contract_last, preferred_element_type=jnp.float32)
        s_c = lax.dot_general(qa, kc_ref[0], contract_last, preferred_element_type=jnp.float32)
        s_p = jnp.where(ok_prev, s_p, MASKED)
        s_c = jnp.where(ok_cur, s_c, MASKED)
        m = jnp.maximum(jnp.max(s_p, axis=-1, keepdims=True), jnp.max(s_c, axis=-1, keepdims=True))
        p_p = jnp.exp2(s_p - m).astype(jnp.bfloat16)
        p_c = jnp.exp2(s_c - m).astype(jnp.bfloat16)
        acc = (jnp.dot(p_p, vp_ref[0], preferred_element_type=jnp.float32)
               + jnp.dot(p_c, vc_ref[0], preferred_element_type=jnp.float32))
        cols = slice(r * HEAD_DIM, (r + 1) * HEAD_DIM)
        tot = prev[:, cols] + gate[:, cols] * (acc[:, 0:HEAD_DIM] / acc[:, HEAD_DIM:AUG_DIM])
        o_ref[0, :, cols] = tot.astype(o_ref.dtype)
        ssq = ssq + _lane_sum_sq(tot)
    ssq_ref[0] = ssq


def window_attention(z3, kv3, q_gain, gate_logits, prev):
    b, t, _ = z3.shape
    tq = min(ATT_TILE, t)
    assert WINDOW <= tq
    gw = NSA_REP * HEAD_DIM
    k_col, v_col = 2 * NSA_KV_HEADS, 3 * NSA_KV_HEADS
    q_tile = lambda col: (lambda bi, g, i, sl: (bi, i, col + g))
    kv_prev = lambda col: (lambda bi, g, i, sl: (bi, jnp.maximum(i - 1, 0), col + g))
    grid_spec = pltpu.PrefetchScalarGridSpec(
        num_scalar_prefetch=1,
        grid=(b, NSA_KV_HEADS, t // tq),
        in_specs=[pl.BlockSpec((1, tq, gw), q_tile(Q_OFF // gw)),
                  pl.BlockSpec((1, tq, AUG_DIM), kv_prev(k_col)),
                  pl.BlockSpec((1, tq, AUG_DIM), kv_prev(v_col)),
                  pl.BlockSpec((1, tq, AUG_DIM), q_tile(k_col)),
                  pl.BlockSpec((1, tq, AUG_DIM), q_tile(v_col)),
                  pl.BlockSpec((1, HEAD_DIM), lambda bi, g, i, sl: (0, 0)),
                  pl.BlockSpec((1, tq, LANES), lambda bi, g, i, sl: (bi, i, 0)),
                  pl.BlockSpec((None, None, LANES, gw), lambda bi, g, i, sl: (2, g, 0, 0)),
                  pl.BlockSpec((1, tq, gw), q_tile(0))],
        out_specs=[pl.BlockSpec((1, tq, gw), q_tile(0)), pl.BlockSpec((1, tq, LANES), q_tile(0))],
    )
    return pl.pallas_call(
        functools.partial(_win_kernel, tq=tq),
        grid_spec=grid_spec,
        out_shape=[jax.ShapeDtypeStruct((b, t, NSA_WIDTH), jnp.bfloat16),
                   jax.ShapeDtypeStruct((b, t, NSA_KV_HEADS * LANES), jnp.float32)],
        compiler_params=_cparams(("parallel", "parallel", "parallel")),
        name="window_attention",
    )(jnp.asarray(_alibi_slopes()), z3, kv3, kv3, kv3, kv3, q_gain.reshape(1, HEAD_DIM), gate_logits,
      jnp.asarray(_gate_expand_matrix(), dtype=jnp.bfloat16), prev)


def kernel(x, p, mix_norm, w_in, conv_w, q_gain, k_gain, cmp_pos, cmp_w1, cmp_w2, sgu_gain, w_sp,
           b_sp, mix_out_norm, w_out, mlp_norm, w_mlp_in, w_mlp_out, ple_norm, w_ple_proj,
           w_ple_gate):
    b, t, d = x.shape
    m = b * t
    bf = jnp.bfloat16
    depth = p.shape[0]
    p_rows = p.reshape(depth, m, PLE_DIM)
    w_in_t = jnp.swapaxes(w_in, 1, 2)
    ones = jnp.ones((D_FF,), jnp.float32)
    h2 = x.reshape(m, d)
    hb, ssq = stats_cast(h2)
    for i in range(depth):
        g_in = mix_norm[i]
        z_a = matmul(hb, cast_weight_t(w_in_t, i, g_in, 0, Z_A_COLS, transpose=False), ssq=ssq,
                     w_rows_are_outputs=True).reshape(b, t, Z_A_COLS)
        z_g = matmul(hb, cast_weight_t(w_in_t, i, g_in, Z_G_OFF, Z_G_COLS, transpose=False), ssq=ssq,
                     w_rows_are_outputs=True).reshape(b, t, Z_G_COLS)
        w_gate = cast_weight_t(w_in_t, i, g_in, Z_A_COLS, LANES, keep=N_GATES)
        gate_logits = matmul(hb, w_gate, ssq=ssq, tn=LANES).reshape(b, t, LANES)

        on = mix_out_norm[i]
        mixed_a = conv_mixer(z_a, conv_w[i], on[:CONV_WIDTH])
        kc, vc = compress_kv(z_a, cmp_pos[i], cmp_w1[i], cmp_w2[i], k_gain[i])
        o_b, sel, used = cmp_attention(z_a, kc, vc, q_gain[i], gate_logits)
        kv3 = kv_prep(z_a, k_gain[i])
        o_b = selected_attention(z_a, kv3, q_gain[i], gate_logits, o_b, sel, used)
        o_b, ssq_b = window_attention(z_a, kv3, q_gain[i], gate_logits, o_b)
        mixed_c = gmlp_mixer(z_g, sgu_gain[i], w_sp[i], b_sp[i], on[CONV_WIDTH + NSA_WIDTH:])
        g_out = jnp.concatenate([ones[:CONV_WIDTH], on[CONV_WIDTH:CONV_WIDTH + NSA_WIDTH],
                                 ones[:GMLP_WIDTH]])
        mixed = (mixed_a.reshape(m, CONV_WIDTH), o_b.reshape(m, NSA_WIDTH),
                 mixed_c.reshape(m, GMLP_WIDTH))
        h2, hb, ssq = matmul(mixed, cast_weight(w_out, i, g_out), mode="residual", res=h2,
                             ssq=(None, ssq_b.reshape(m, NSA_KV_HEADS * LANES), None),
                             stats=True, tn=512)

        hid = matmul(hb, cast_weight(w_mlp_in, i, mlp_norm[i]), mode="relu2", out_dtype=bf, ssq=ssq)
        h2 = matmul(hid, cast_weight(w_mlp_out, i, ones), mode="residual", res=h2)
        hb, ssq = stats_cast(h2)

        out = matmul(hb, cast_weight(w_ple_gate, i, ple_norm[i]), mode="ple", res=h2, p=p_rows,
                     wp=w_ple_proj, layer=i, ssq=ssq, stats=i + 1 < depth, tn=512)
        h2, hb, ssq = out if i + 1 < depth else (out, None, None)
    return h2.reshape(b, t, d)
```

```python
import functools

import jax
import jax.numpy as jnp
import numpy as np
from jax import lax
from jax.experimental import pallas as pl
from jax.experimental.pallas import tpu as pltpu

D_MODEL = 4096
HEAD_DIM = 128
CONV_WIDTH = D_MODEL // 4
CONV_K = 3
NSA_HEADS = (D_MODEL // 2) // HEAD_DIM
NSA_WIDTH = NSA_HEADS * HEAD_DIM
NSA_KV_HEADS = 4
NSA_REP = NSA_HEADS // NSA_KV_HEADS
NSA_BRANCHES = 3
KV_WIDTH = NSA_KV_HEADS * HEAD_DIM
CMP_LEN = 32
CMP_STRIDE = 16
SLC_BLOCK = 64
SLC_SHIFT = 6
N_SELECT = 16
WINDOW = 512
GMLP_WIDTH = D_MODEL // 4
GMLP_GROUPS = GMLP_WIDTH // HEAD_DIM
GMLP_CHUNK = 128
D_FF = 4 * D_MODEL
PLE_DIM = 256
N_GATES = NSA_HEADS * NSA_BRANCHES
EPS = 1e-6
NEG_INF = -1e30
FORCE_SCORE = 1e9
BELOW_NEG_INF = -3e38

Z_A_COLS = 3 * CONV_WIDTH + NSA_WIDTH + 2 * NSA_BRANCHES * KV_WIDTH
Z_G_COLS = 2 * GMLP_WIDTH
Z_G_OFF = Z_A_COLS + N_GATES
Q_OFF = 3 * CONV_WIDTH
KCMP_OFF = Q_OFF + NSA_WIDTH
VCMP_OFF = KCMP_OFF + KV_WIDTH
KV4_OFF = VCMP_OFF + KV_WIDTH

LANES = 128
VMEM_LIMIT_BYTES = 60 * 1024 * 1024


def _cparams(dims):
    return pltpu.CompilerParams(dimension_semantics=dims, vmem_limit_bytes=VMEM_LIMIT_BYTES)


def _rms(x, gain):
    return x * lax.rsqrt(jnp.mean(x * x, axis=-1, keepdims=True) + EPS) * gain


def _gelu(x):
    return jax.nn.gelu(x)


def _row_scale(ssq, width, n_groups=1):
    tot = ssq[:, 0:LANES]
    for gi in range(1, n_groups):
        tot = tot + ssq[:, gi * LANES:(gi + 1) * LANES]
    return lax.rsqrt(tot * (1.0 / width) + EPS)


def _lane_sum_sq(x):
    return jnp.broadcast_to(jnp.sum(x * x, axis=-1, keepdims=True), (x.shape[0], LANES))


def _stats_kernel(x_ref, hb_ref, ssq_ref):
    x = x_ref[...]
    hb_ref[...] = x.astype(hb_ref.dtype)
    ssq_ref[...] = _lane_sum_sq(x)


def stats_cast(x, tm=512):
    m, d = x.shape
    return pl.pallas_call(
        _stats_kernel,
        grid=(m // tm,),
        in_specs=[pl.BlockSpec((tm, d), lambda i: (i, 0))],
        out_specs=[pl.BlockSpec((tm, d), lambda i: (i, 0)),
                   pl.BlockSpec((tm, LANES), lambda i: (i, 0))],
        out_shape=[jax.ShapeDtypeStruct((m, d), jnp.bfloat16),
                   jax.ShapeDtypeStruct((m, LANES), jnp.float32)],
        compiler_params=_cparams(("parallel",)),
        name="stats_cast",
    )(x)


def _cast_kernel(w_ref, g_ref, o_ref):
    o_ref[...] = (w_ref[...] * g_ref[...]).astype(o_ref.dtype)


def _cast_t_kernel(a_ref, *refs, shift, keep, transpose):
    g_ref, o_ref = refs[-2:]
    x = a_ref[...]
    if shift:
        x = jnp.concatenate([x[shift:, :], refs[0][:shift, :]], axis=0)
    if keep < x.shape[0]:
        row = lax.broadcasted_iota(jnp.int32, x.shape, 0)
        x = jnp.where(row < keep, x, 0.0)
    x = x * g_ref[...]
    o_ref[...] = (x.T if transpose else x).astype(o_ref.dtype)


def cast_weight(w, layer, gain, tk=1024, tn=4096):
    _, kdim, n = w.shape
    tk, tn = min(tk, kdim), min(tn, n)
    assert kdim % tk == 0 and n % tn == 0
    return pl.pallas_call(
        _cast_kernel,
        grid=(kdim // tk, n // tn),
        in_specs=[pl.BlockSpec((None, tk, tn), lambda i, j: (layer, i, j)),
                  pl.BlockSpec((tk, 1), lambda i, j: (i, 0))],
        out_specs=pl.BlockSpec((tk, tn), lambda i, j: (i, j)),
        out_shape=jax.ShapeDtypeStruct((kdim, n), jnp.bfloat16),
        compiler_params=_cparams(("parallel", "parallel")),
        name="cast_weight",
    )(w, gain.reshape(kdim, 1))


def cast_weight_t(w_t, layer, gain, row0, nrows, keep=None, transpose=True, tr=512, tc=2048):
    _, n, kdim = w_t.shape
    tr, tc = min(tr, nrows), min(tc, kdim)
    shift = row0 % tr
    base = row0 // tr
    assert nrows % tr == 0 and kdim % tc == 0 and shift % 8 == 0
    in_specs = [pl.BlockSpec((None, tr, tc), lambda i, j: (layer, base + j, i))]
    if shift:
        head = 8 * pl.cdiv(shift, 8)
        while tr % head:
            head += 8
        in_specs.append(pl.BlockSpec((None, head, tc), lambda i, j: (layer, (base + j + 1) * (tr // head), i)))
    return pl.pallas_call(
        functools.partial(_cast_t_kernel, shift=shift, keep=keep or nrows, transpose=transpose),
        grid=(kdim // tc, nrows // tr),
        in_specs=in_specs + [pl.BlockSpec((1, tc), lambda i, j: (0, i))],
        out_specs=(pl.BlockSpec((tc, tr), lambda i, j: (i, j)) if transpose
                   else pl.BlockSpec((tr, tc), lambda i, j: (j, i))),
        out_shape=jax.ShapeDtypeStruct((kdim, nrows) if transpose else (nrows, kdim), jnp.bfloat16),
        compiler_params=_cparams(("parallel", "parallel")),
        name="cast_weight_t",
    )(*([w_t] * (2 if shift else 1)), gain.reshape(1, kdim))


def _mm_kernel(*refs, mode, nk, n_x, scale_groups, stats, w_rows_are_outputs):
    x_refs, refs = refs[:n_x], refs[n_x:]
    refs = list(refs)
    take = lambda cond: refs.pop(0) if cond else None
    w_ref = refs.pop(0)
    res_ref = take(mode in ("residual", "ple"))
    p_ref, wp_ref = take(mode == "ple"), take(mode == "ple")
    ssq_refs = [take(g > 0) for g in scale_groups]
    o_ref = refs.pop(0)
    hb_ref, ssq_out_ref = take(stats), take(stats)
    tn = o_ref.shape[1]
    j = pl.program_id(1)
    k = pl.program_id(2)

    def scaled(d, n):
        if ssq_refs[n] is None:
            return d
        s = _row_scale(ssq_refs[n][...], x_refs[n].shape[1] * (nk if len(x_refs) == 1 else 1),
                       scale_groups[n])
        return d * jnp.tile(s, (1, tn // LANES))

    def finish(new):
        o_ref[...] = new
        if stats:
            hb_ref[...] = new.astype(hb_ref.dtype)
            ssq = _lane_sum_sq(new)

            @pl.when(j == 0)
            def _():
                ssq_out_ref[...] = ssq

            @pl.when(j > 0)
            def _():
                ssq_out_ref[...] += ssq

    def epilogue(acc):
        if mode == "plain":
            o_ref[...] = acc.astype(o_ref.dtype)
        elif mode == "relu2":
            r = jnp.maximum(acc, 0.0)
            o_ref[...] = (r * r).astype(o_ref.dtype)
        elif mode == "residual":
            finish(res_ref[...] + acc)
        else:
            proj = jnp.dot(p_ref[...].astype(jnp.bfloat16), wp_ref[...].astype(jnp.bfloat16),
                           preferred_element_type=jnp.float32)
            finish(res_ref[...] + jax.nn.sigmoid(acc) * proj)

    def product():
        part, off = None, 0
        for n, x_ref in enumerate(x_refs):
            kx = x_ref.shape[1]
            if w_rows_are_outputs:
                d = lax.dot_general(x_ref[...], w_ref[:, off:off + kx], (((1,), (1,)), ((), ())),
                                    preferred_element_type=jnp.float32)
            else:
                d = jnp.dot(x_ref[...], w_ref[off:off + kx, :], preferred_element_type=jnp.float32)
            d = scaled(d, n)
            part = d if part is None else part + d
            off += kx
        return part

    if nk == 1:
        epilogue(product())
        return
    if mode == "residual":
        @pl.when(k == 0)
        def _():
            o_ref[...] = res_ref[...] + product()

        @pl.when(jnp.logical_and(k > 0, k < nk - 1))
        def _():
            o_ref[...] += product()

        @pl.when(k == nk - 1)
        def _():
            finish(o_ref[...] + product())
        return
    acc_ref, = refs

    @pl.when(k == 0)
    def _():
        acc_ref[...] = product()

    @pl.when(jnp.logical_and(k > 0, k < nk - 1))
    def _():
        acc_ref[...] += product()

    @pl.when(k == nk - 1)
    def _():
        epilogue(acc_ref[...] + product())


def matmul(x, w, mode="plain", out_dtype=jnp.float32, res=None, p=None, wp=None, layer=0,
           ssq=None, stats=False, w_rows_are_outputs=False, tm=1024, tn=1024, tk=4096):
    xs = x if isinstance(x, (tuple, list)) else (x,)
    ssqs = ssq if isinstance(ssq, (tuple, list)) else (ssq,) * len(xs)
    m = xs[0].shape[0]
    kdim, n = w.shape[::-1] if w_rows_are_outputs else w.shape
    assert sum(xi.shape[1] for xi in xs) == kdim and len(ssqs) == len(xs)
    tm, tn, tk = min(tm, m), min(tn, n), min(tk, kdim)
    nk = kdim // tk
    assert m % tm == 0 and n % tn == 0 and kdim % tk == 0 and (nk == 1 or len(xs) == 1)
    assert not stats or mode in ("residual", "ple")
    in_specs = [pl.BlockSpec((tm, xi.shape[1] if len(xs) > 1 else tk), lambda i, j, k: (i, k))
                for xi in xs]
    in_specs.append(pl.BlockSpec((tn, tk), lambda i, j, k: (j, k)) if w_rows_are_outputs
                    else pl.BlockSpec((tk, tn), lambda i, j, k: (k, j)))
    args = [*xs, w]
    if mode in ("residual", "ple"):
        in_specs.append(pl.BlockSpec((tm, tn), lambda i, j, k: (i, j)))
        args.append(res)
    if mode == "ple":
        pd = p.shape[2]
        in_specs += [pl.BlockSpec((None, tm, pd), lambda i, j, k: (layer, i, 0)),
                     pl.BlockSpec((None, pd, tn), lambda i, j, k: (layer, 0, j))]
        args += [p, wp]
    scale_groups = tuple(0 if s is None else s.shape[1] // LANES for s in ssqs)
    for s in ssqs:
        if s is not None:
            in_specs.append(pl.BlockSpec((tm, s.shape[1]), lambda i, j, k: (i, 0)))
            args.append(s)
    tile = pl.BlockSpec((tm, tn), lambda i, j, k: (i, j))
    out_specs, out_shape = tile, jax.ShapeDtypeStruct((m, n), out_dtype)
    if stats:
        out_specs = [tile, tile, pl.BlockSpec((tm, LANES), lambda i, j, k: (i, 0))]
        out_shape = [out_shape, jax.ShapeDtypeStruct((m, n), jnp.bfloat16),
                     jax.ShapeDtypeStruct((m, LANES), jnp.float32)]
    scratch = [pltpu.VMEM((tm, tn), jnp.float32)] if nk > 1 and mode != "residual" else []
    return pl.pallas_call(
        functools.partial(_mm_kernel, mode=mode, nk=nk, n_x=len(xs), scale_groups=scale_groups,
                          stats=stats, w_rows_are_outputs=w_rows_are_outputs),
        grid=(m // tm, n // tn, nk),
        in_specs=in_specs,
        out_specs=out_specs,
        out_shape=out_shape,
        scratch_shapes=scratch,
        compiler_params=_cparams(("parallel", "arbitrary" if stats else "parallel", "arbitrary")),
        name="matmul_" + mode,
    )(*args)


HALO_ROWS = 8


def _conv_kernel(xa_ref, gb_ref, gc_ref, xap_ref, gcp_ref, cw_ref, on_ref, o_ref):
    i = pl.program_id(1)
    inner = gc_ref[0] * xa_ref[0]
    prev = gcp_ref[0] * xap_ref[0]
    prev = jnp.where(i > 0, prev, 0.0)
    tt = inner.shape[0]
    row = lax.broadcasted_iota(jnp.int32, inner.shape, 0)
    m1 = jnp.where(row == 0, prev[HALO_ROWS - 1:HALO_ROWS], pltpu.roll(inner, 1, 0))
    m2 = pltpu.roll(inner, 2, 0)
    m2 = jnp.where(row == 0, prev[HALO_ROWS - 2:HALO_ROWS - 1], m2)
    m2 = jnp.where(row == 1, prev[HALO_ROWS - 1:HALO_ROWS], m2)
    cw = cw_ref[...]
    conv = cw[0:1] * m2 + cw[1:2] * m1 + cw[2:3] * inner
    o_ref[0] = _rms(gb_ref[0] * conv, on_ref[...]).astype(o_ref.dtype)


def conv_mixer(z3, conv_w, on_gain, tt=1024):
    b, t, _ = z3.shape
    c = CONV_WIDTH
    hb = tt // HALO_ROWS
    cur = lambda col: pl.BlockSpec((1, tt, c), lambda bi, i: (bi, i, col))
    halo = lambda col: pl.BlockSpec((1, HALO_ROWS, c),
                                    lambda bi, i: (bi, jnp.maximum(i * hb - 1, 0), col))
    return pl.pallas_call(
        _conv_kernel,
        grid=(b, t // tt),
        in_specs=[cur(0), cur(1), cur(2), halo(0), halo(2),
                  pl.BlockSpec((HALO_ROWS, c), lambda bi, i: (0, 0)),
                  pl.BlockSpec((1, c), lambda bi, i: (0, 0))],
        out_specs=pl.BlockSpec((1, tt, c), lambda bi, i: (bi, i, 0)),
        out_shape=jax.ShapeDtypeStruct((b, t, c), jnp.bfloat16),
        compiler_params=_cparams(("parallel", "parallel")),
        name="conv_mixer",
    )(z3, z3, z3, z3, z3,
      jnp.pad(conv_w, ((0, HALO_ROWS - CONV_K), (0, 0))), on_gain.reshape(1, c))


def _gmlp_kernel(gu_ref, gv_ref, sg_ref, wsp_ref, bexp_ref, on_ref, o_ref, *, n_chunks):
    u = _gelu(gu_ref[0])
    v = _rms(_gelu(gv_ref[0]), sg_ref[...]).astype(jnp.bfloat16)
    ck = GMLP_CHUNK
    r_i = lax.broadcasted_iota(jnp.int32, (ck, ck), 0)
    c_i = lax.broadcasted_iota(jnp.int32, (ck, ck), 1)
    bexp = bexp_ref[...]
    rows = []
    for c in range(n_chunks):
        cols = []
        for g in range(GMLP_GROUPS):
            w = jnp.where(r_i >= c_i, wsp_ref[g], 0.0).astype(jnp.bfloat16)
            vg = v[c * ck:(c + 1) * ck, g * HEAD_DIM:(g + 1) * HEAD_DIM]
            cols.append(jnp.dot(w, vg, preferred_element_type=jnp.float32))
        rows.append(jnp.concatenate(cols, axis=1) + bexp)
    spatial = jnp.concatenate(rows, axis=0) if n_chunks > 1 else rows[0]
    o_ref[0] = _rms(u * spatial, on_ref[...]).astype(o_ref.dtype)


def gmlp_mixer(z3, sgu_gain, w_sp, b_sp, on_gain, tt=512):
    b, t, _ = z3.shape
    c = GMLP_WIDTH
    bexp = jnp.repeat(b_sp.T, HEAD_DIM, axis=1)
    const2 = lambda shape: pl.BlockSpec(shape, lambda bi, i: (0,) * len(shape))
    return pl.pallas_call(
        functools.partial(_gmlp_kernel, n_chunks=tt // GMLP_CHUNK),
        grid=(b, t // tt),
        in_specs=[pl.BlockSpec((1, tt, c), lambda bi, i: (bi, i, 0)),
                  pl.BlockSpec((1, tt, c), lambda bi, i: (bi, i, 1)),
                  const2((1, c)), const2((GMLP_GROUPS, GMLP_CHUNK, GMLP_CHUNK)),
                  const2((GMLP_CHUNK, c)), const2((1, c))],
        out_specs=pl.BlockSpec((1, tt, c), lambda bi, i: (bi, i, 0)),
        out_shape=jax.ShapeDtypeStruct((b, t, c), jnp.bfloat16),
        compiler_params=_cparams(("parallel", "parallel")),
        name="gmlp_mixer",
    )(z3, z3, sgu_gain.reshape(1, c), w_sp, bexp, on_gain.reshape(1, c))


ATT_TILE = 512
KV_TILE_SHIFT = 9
AUG_DIM = 2 * HEAD_DIM
POS_LANE = 64
ONE_LANE = 70
FEAT_END = 73
MASK_BIG = 2.0 ** 100
MASKED = -MASK_BIG
M_INIT = NEG_INF
LOG2E = 1.4426950408889634


def _key_features(pos, with_block_onehot):
    lane = lax.broadcasted_iota(jnp.int32, pos.shape, 1)
    blk = lax.shift_right_logical(pos, SLC_SHIFT)
    hi = lax.shift_left(blk, SLC_SHIFT).astype(jnp.float32)
    lo = jnp.bitwise_and(pos, SLC_BLOCK - 1).astype(jnp.float32)
    posf = jnp.where(jnp.bitwise_and(lane, 1) == 0, hi, lo)
    tail = jnp.where(lane < ONE_LANE, posf, jnp.where(lane < FEAT_END, 1.0, 0.0))
    head = jnp.where(lane == blk, MASK_BIG, 0.0) if with_block_onehot else 0.0
    return jnp.where(lane < POS_LANE, head, tail)


def _kvprep_kernel(z_ref, kg_ref, o_ref, *, tt):
    i = pl.program_id(1)
    z = z_ref[0]
    pos = i * tt + lax.broadcasted_iota(jnp.int32, (tt, HEAD_DIM), 0)
    kfeat = (_key_features(pos, True).astype(o_ref.dtype), _key_features(pos, False).astype(o_ref.dtype))
    vfeat = jnp.ones((tt, HEAD_DIM), o_ref.dtype)
    for part in range(4):
        for g in range(NSA_KV_HEADS):
            src = part * KV_WIDTH + g * HEAD_DIM
            dst = (part * NSA_KV_HEADS + g) * AUG_DIM
            x = z[:, src:src + HEAD_DIM]
            if part % 2 == 0:
                x = _rms(x, kg_ref[1 + part // 2:2 + part // 2, :])
            o_ref[0, :, dst:dst + HEAD_DIM] = x.astype(o_ref.dtype)
            o_ref[0, :, dst + HEAD_DIM:dst + AUG_DIM] = kfeat[part // 2] if part % 2 == 0 else vfeat


def kv_prep(z3, k_gain, tt=512):
    b, t, _ = z3.shape
    w = 4 * KV_WIDTH
    wo = 4 * NSA_KV_HEADS * AUG_DIM
    kg = jnp.pad(k_gain, ((0, 8 - NSA_BRANCHES), (0, 0)))
    return pl.pallas_call(
        functools.partial(_kvprep_kernel, tt=tt),
        grid=(b, t // tt),
        in_specs=[pl.BlockSpec((1, tt, w), lambda bi, i: (bi, i, KV4_OFF // w)),
                  pl.BlockSpec((8, HEAD_DIM), lambda bi, i: (0, 0))],
        out_specs=pl.BlockSpec((1, tt, wo), lambda bi, i: (bi, i, 0)),
        out_shape=jax.ShapeDtypeStruct((b, t, wo), jnp.bfloat16),
        compiler_params=_cparams(("parallel", "parallel")),
        name="kv_prep",
    )(z3, kg)


def _compress_kernel(k_ref, v_ref, w1_ref, w2_ref, pe_ref, kg_ref, kc_ref, vc_ref, *, n_chunk):
    def one(src_ref, which):
        ya = jnp.zeros((n_chunk, HEAD_DIM), jnp.float32)
        yb = jnp.zeros((n_chunk, HEAD_DIM), jnp.float32)
        bias = jnp.zeros((8, HEAD_DIM), jnp.float32)
        for l in range(CMP_STRIDE):
            xl = src_ref[0, pl.ds(l, n_chunk, stride=CMP_STRIDE), :].astype(jnp.bfloat16)
            ya += jnp.dot(xl, w1_ref[which, l], preferred_element_type=jnp.float32)
            yb += jnp.dot(xl, w1_ref[which, CMP_STRIDE + l], preferred_element_type=jnp.float32)
        for l in range(CMP_LEN):
            pe_l = jnp.broadcast_to(pe_ref[which, l:l + 1, :], (8, HEAD_DIM)).astype(jnp.bfloat16)
            bias += jnp.dot(pe_l, w1_ref[which, l], preferred_element_type=jnp.float32)
        hid = _gelu(ya + pltpu.roll(yb, n_chunk - 1, 0) + bias[0:1])
        return jnp.dot(hid.astype(jnp.bfloat16), w2_ref[which], preferred_element_type=jnp.float32)

    kc_ref[0, 0, :, 0:HEAD_DIM] = _rms(one(k_ref, 0), kg_ref[0:1, :]).astype(kc_ref.dtype)
    n = lax.broadcasted_iota(jnp.int32, (n_chunk, HEAD_DIM), 0)
    kc_ref[0, 0, :, HEAD_DIM:AUG_DIM] = _key_features(n * CMP_STRIDE + CMP_LEN - 1, False).astype(kc_ref.dtype)
    vc_ref[0, 0] = one(v_ref, 1).astype(vc_ref.dtype)


def compress_kv(z3, cmp_pos, cmp_w1, cmp_w2, k_gain):
    b, t, _ = z3.shape
    n_chunk = t // CMP_STRIDE
    kg = jnp.pad(k_gain, ((0, 8 - NSA_BRANCHES), (0, 0)))
    const = lambda shape: pl.BlockSpec(shape, lambda bi, g: (0,) * len(shape))
    out = lambda width: jax.ShapeDtypeStruct((b, NSA_KV_HEADS, n_chunk, width), jnp.bfloat16)
    tile = lambda width: pl.BlockSpec((1, 1, n_chunk, width), lambda bi, g: (bi, g, 0, 0))
    return pl.pallas_call(
        functools.partial(_compress_kernel, n_chunk=n_chunk),
        grid=(b, NSA_KV_HEADS),
        in_specs=[pl.BlockSpec((1, t, HEAD_DIM), lambda bi, g: (bi, 0, KCMP_OFF // HEAD_DIM + g)),
                  pl.BlockSpec((1, t, HEAD_DIM), lambda bi, g: (bi, 0, VCMP_OFF // HEAD_DIM + g)),
                  const((2, CMP_LEN, HEAD_DIM, HEAD_DIM)), const((2, HEAD_DIM, HEAD_DIM)),
                  const((2, CMP_LEN, HEAD_DIM)), const((8, HEAD_DIM))],
        out_specs=[tile(AUG_DIM), tile(HEAD_DIM)],
        out_shape=[out(AUG_DIM), out(HEAD_DIM)],
        compiler_params=_cparams(("parallel", "parallel")),
        name="compress_kv",
    )(z3, z3, cmp_w1.astype(jnp.bfloat16), cmp_w2.astype(jnp.bfloat16), cmp_pos, kg)


def _alibi_slopes():
    h = np.arange(1, NSA_HEADS + 1, dtype=np.float32)
    return np.power(np.float32(2.0), -8.0 * h / NSA_HEADS).astype(np.float32)


def _overlap_matrix(n_chunk, n_slc):
    n_cmp = n_chunk - CMP_LEN // CMP_STRIDE + 1
    jc = np.arange(n_cmp)[:, None] * CMP_STRIDE
    bs = np.arange(n_slc)[None, :] * SLC_BLOCK
    ov = np.clip(np.minimum(jc + CMP_LEN, bs + SLC_BLOCK) - np.maximum(jc, bs), 0, None)
    out = np.zeros((n_chunk, LANES), np.float32)
    out[:n_cmp, :n_slc] = ov.astype(np.float32) / CMP_LEN
    return out


def _gate_expand_matrix():
    m = np.zeros((NSA_BRANCHES, NSA_KV_HEADS, LANES, NSA_REP * HEAD_DIM), np.float32)
    for br in range(NSA_BRANCHES):
        for g in range(NSA_KV_HEADS):
            for r in range(NSA_REP):
                m[br, g, (g * NSA_REP + r) * NSA_BRANCHES + br, r * HEAD_DIM:(r + 1) * HEAD_DIM] = 1.0
    return m


def _branch_gate(gl_ref, ex_ref):
    sig = jax.nn.sigmoid(gl_ref[0])
    hi = sig.astype(jnp.bfloat16)
    lo = (sig - hi.astype(jnp.float32)).astype(jnp.bfloat16)
    ex = ex_ref[...]
    return (jnp.dot(hi, ex, preferred_element_type=jnp.float32)
            + jnp.dot(lo, ex, preferred_element_type=jnp.float32))


def _cmp_kernel(q_ref, kc_ref, vc_ref, ov_ref, gl_ref, ex_ref, o_ref, sel_ref,
                used_ref, *, tq, n_chunk, n_slc):
    i = pl.program_id(2)
    gate = _branch_gate(gl_ref, ex_ref)
    n_cmp = n_chunk - CMP_LEN // CMP_STRIDE + 1
    kc = kc_ref[0, 0]
    vc = vc_ref[0, 0]
    t = i * tq + lax.broadcasted_iota(jnp.int32, (tq, n_chunk), 0)
    n = lax.broadcasted_iota(jnp.int32, (tq, n_chunk), 1)
    mask = jnp.logical_and(t >= n * CMP_STRIDE + CMP_LEN - 1, n < n_cmp)
    psum = jnp.zeros((tq, n_chunk), jnp.float32)
    for r in range(NSA_REP):
        s2 = lax.dot_general(q_ref[0, :, r * AUG_DIM:(r + 1) * AUG_DIM], kc, (((1,), (1,)), ((), ())),
                             preferred_element_type=jnp.float32)
        s2 = jnp.where(mask, s2, MASKED)
        mx = jnp.maximum(jnp.max(s2, axis=-1, keepdims=True), M_INIT)
        e = jnp.exp2(s2 - mx)
        den = jnp.sum(e, axis=-1, keepdims=True)
        p = e / jnp.where(den > 0.0, den, 1.0)
        psum += p
        o_ref[0, :, r * HEAD_DIM:(r + 1) * HEAD_DIM] = gate[:, r * HEAD_DIM:(r + 1) * HEAD_DIM] * jnp.dot(
            p.astype(jnp.bfloat16), vc, preferred_element_type=jnp.float32)

    ov = ov_ref[...]
    imp = jnp.zeros((tq, LANES), jnp.float32)
    rem = psum
    for _ in range(3):
        part = rem.astype(jnp.bfloat16)
        imp += jnp.dot(part, ov, preferred_element_type=jnp.float32)
        rem = rem - part.astype(jnp.float32)

    tb = i * tq + lax.broadcasted_iota(jnp.int32, (tq, LANES), 0)
    blk = lax.broadcasted_iota(jnp.int32, (tq, LANES), 1)
    qblk = lax.shift_right_logical(tb, SLC_SHIFT)
    forced = (blk == 0) | (blk == qblk) | (blk == qblk - 1)
    score = jnp.where(forced, FORCE_SCORE, jnp.where(blk <= qblk, imp, NEG_INF))
    score = jnp.where(blk < n_slc, score, BELOW_NEG_INF)
    st = score.T[0:n_slc]
    n_tiles = n_slc // 8
    tiles = [st[8 * j:8 * j + 8] for j in range(n_tiles)]
    sub8 = lax.broadcasted_iota(jnp.int32, (8, tq), 0)
    cnt = [jnp.zeros((8, tq), jnp.float32) for _ in range(n_tiles)]
    for m in range(n_slc):
        row = st[m:m + 1, :]
        own, off = divmod(m, 8)
        for j in range(n_tiles):
            if j < own:
                inc = jnp.where(row > tiles[j], 1.0, 0.0)
            elif j > own:
                inc = jnp.where(row >= tiles[j], 1.0, 0.0)
            else:
                later = jnp.where(sub8 > off, 1.0, 0.0)
                inc = (jnp.where(row >= tiles[j], later, 0.0)
                       + jnp.where(row > tiles[j], 1.0 - later, 0.0))
            cnt[j] = cnt[j] + inc
    keep = float(min(N_SELECT, n_slc))
    sel_t = jnp.concatenate([jnp.where(c < keep, 1.0, 0.0) for c in cnt]
                            + [jnp.zeros((LANES - n_slc, tq), jnp.float32)] * (n_slc < LANES), axis=0)
    sel = sel_t.T
    sel_ref[0, 0] = sel.astype(sel_ref.dtype)
    picked = jnp.broadcast_to(jnp.max(sel, axis=0, keepdims=True), (8, LANES)).astype(jnp.bfloat16)
    blk_i = lax.broadcasted_iota(jnp.int32, (LANES, LANES), 0)
    tile_i = lax.broadcasted_iota(jnp.int32, (LANES, LANES), 1)
    group = jnp.where(lax.shift_right_logical(blk_i, KV_TILE_SHIFT - SLC_SHIFT) == tile_i, 1.0, 0.0)
    used = jnp.dot(picked, group.astype(jnp.bfloat16), preferred_element_type=jnp.float32)
    used_ref[0, 0] = jnp.where(used > 0.5, 1, 0).astype(jnp.int32)


def cmp_attention(q_aug, kc, vc, gate_logits):
    b, t, _ = q_aug.shape
    tq = min(ATT_TILE, t)
    n_chunk = t // CMP_STRIDE
    n_slc = t // SLC_BLOCK
    assert n_slc <= LANES
    gw = NSA_REP * HEAD_DIM
    return pl.pallas_call(
        functools.partial(_cmp_kernel, tq=tq, n_chunk=n_chunk, n_slc=n_slc),
        grid=(b, NSA_KV_HEADS, t // tq),
        in_specs=[pl.BlockSpec((1, tq, NSA_REP * AUG_DIM), lambda bi, g, i: (bi, i, g)),
                  pl.BlockSpec((1, 1, n_chunk, AUG_DIM), lambda bi, g, i: (bi, g, 0, 0)),
                  pl.BlockSpec((1, 1, n_chunk, HEAD_DIM), lambda bi, g, i: (bi, g, 0, 0)),
                  pl.BlockSpec((n_chunk, LANES), lambda bi, g, i: (0, 0)),
                  pl.BlockSpec((1, tq, LANES), lambda bi, g, i: (bi, i, 0)),
                  pl.BlockSpec((None, None, LANES, gw), lambda bi, g, i: (0, g, 0, 0))],
        out_specs=[pl.BlockSpec((1, tq, gw), lambda bi, g, i: (bi, i, g)),
                   pl.BlockSpec((1, 1, tq, LANES), lambda bi, g, i: (bi, g, i, 0)),
                   pl.BlockSpec((1, 1, 8, LANES), lambda bi, g, i: (bi, g, i, 0))],
        out_shape=[jax.ShapeDtypeStruct((b, t, NSA_WIDTH), jnp.float32),
                   jax.ShapeDtypeStruct((b, NSA_KV_HEADS, t, LANES), jnp.bfloat16),
                   jax.ShapeDtypeStruct((b, NSA_KV_HEADS, t // tq * 8, LANES), jnp.int32)],
        compiler_params=_cparams(("parallel", "parallel", "parallel")),
        name="cmp_attention",
    )(q_aug, kc, vc, jnp.asarray(_overlap_matrix(n_chunk, n_slc), dtype=jnp.bfloat16), gate_logits,
      jnp.asarray(_gate_expand_matrix(), dtype=jnp.bfloat16))


STEP_FIRST, STEP_LAST, STEP_MASKED = 1, 2, 4


def _flash_steps(t, tq, tk):
    qi, kv, fl = [], [], []
    for i in range(t // tq):
        lo, hi = i * tq, i * tq + tq - 1
        n_tiles = hi // tk + 1
        for j in range(n_tiles):
            flag = (STEP_FIRST if j == 0 else 0) | (STEP_LAST if j == n_tiles - 1 else 0)
            flag |= 0 if j * tk + tk - 1 <= lo else STEP_MASKED
            qi.append(i), kv.append(j), fl.append(flag)
    return [np.asarray(a, np.int32) for a in (qi, kv, fl)]


def _split3(x):
    a = x.astype(jnp.bfloat16).astype(jnp.float32)
    b = (x - a).astype(jnp.bfloat16).astype(jnp.float32)
    c = (x - a - b).astype(jnp.bfloat16).astype(jnp.float32)
    return a, b, c


def _qprep_kernel(slopes_ref, q_ref, qg_ref, o_ref, *, tq):
    g = pl.program_id(1)
    q = q_ref[0]
    lane = lax.broadcasted_iota(jnp.int32, (tq, HEAD_DIM), 1)
    t0 = (pl.program_id(2) * tq).astype(jnp.float32)
    for r in range(NSA_REP):
        qn = _rms(q[:, r * HEAD_DIM:(r + 1) * HEAD_DIM], qg_ref[...]) * (HEAD_DIM ** -0.5 * LOG2E)
        slope2 = slopes_ref[g * NSA_REP + r] * LOG2E
        a, b, c = _split3(jnp.where(lane < ONE_LANE, slope2, -slope2 * t0))
        k3 = jnp.where(lane < ONE_LANE, lax.shift_right_logical(lane - POS_LANE, 1), lane - ONE_LANE)
        term = jnp.where(k3 == 0, a, jnp.where(k3 == 1, b, c))
        feat = jnp.where(jnp.logical_and(lane >= POS_LANE, lane < FEAT_END), term, 0.0)
        o_ref[0, :, r * AUG_DIM:r * AUG_DIM + HEAD_DIM] = qn.astype(o_ref.dtype)
        o_ref[0, :, r * AUG_DIM + HEAD_DIM:(r + 1) * AUG_DIM] = feat.astype(o_ref.dtype)


def q_prep(z3, q_gain):
    b, t, _ = z3.shape
    tq = min(ATT_TILE, t)
    gw = NSA_REP * HEAD_DIM
    grid_spec = pltpu.PrefetchScalarGridSpec(
        num_scalar_prefetch=1,
        grid=(b, NSA_KV_HEADS, t // tq),
        in_specs=[pl.BlockSpec((1, tq, gw), lambda bi, g, i, sl: (bi, i, Q_OFF // gw + g)),
                  pl.BlockSpec((1, HEAD_DIM), lambda bi, g, i, sl: (0, 0))],
        out_specs=pl.BlockSpec((1, tq, NSA_REP * AUG_DIM), lambda bi, g, i, sl: (bi, i, g)),
    )
    return pl.pallas_call(
        functools.partial(_qprep_kernel, tq=tq),
        grid_spec=grid_spec,
        out_shape=jax.ShapeDtypeStruct((b, t, NSA_HEADS * AUG_DIM), jnp.bfloat16),
        compiler_params=_cparams(("parallel", "parallel", "parallel")),
        name="q_prep",
    )(jnp.asarray(_alibi_slopes()), z3, q_gain.reshape(1, HEAD_DIM))


def _slc_kernel(qi_ref, kv_ref, fl_ref, used_ref, q_ref, k_ref, v_ref, gl_ref,
                ex_ref, prev_ref, sel_ref, o_ref, q_sc, m_sc, acc_sc, *, tq, tk):
    g = pl.program_id(1)
    s = pl.program_id(2)
    i = qi_ref[s]
    kv_idx = kv_ref[s]
    flags = fl_ref[s]

    @pl.when(jnp.bitwise_and(flags, STEP_FIRST) != 0)
    def _():
        sel_m1 = (sel_ref[0, 0].astype(jnp.float32) - 1.0).astype(q_sc.dtype)
        lane = lax.broadcasted_iota(jnp.int32, (tq, HEAD_DIM), 1)
        for r in range(NSA_REP):
            q_sc[r, :, 0:HEAD_DIM] = q_ref[0, :, r * AUG_DIM:r * AUG_DIM + HEAD_DIM]
            q_sc[r, :, HEAD_DIM:AUG_DIM] = jnp.where(
                lane < POS_LANE, sel_m1, q_ref[0, :, r * AUG_DIM + HEAD_DIM:(r + 1) * AUG_DIM])
        m_sc[...] = jnp.full(m_sc.shape, M_INIT, jnp.float32)
        acc_sc[...] = jnp.zeros(acc_sc.shape, jnp.float32)

    def step(masked):
        k = k_ref[0]
        v = v_ref[0]
        if masked:
            t = i * tq + lax.broadcasted_iota(jnp.int32, (tq, tk), 0)
            ok = t >= kv_idx * tk + lax.broadcasted_iota(jnp.int32, (tq, tk), 1)
        for r in range(NSA_REP):
            s2 = lax.dot_general(q_sc[r], k, (((1,), (1,)), ((), ())),
                                 preferred_element_type=jnp.float32)
            if masked:
                s2 = jnp.where(ok, s2, MASKED)
            m_prev = m_sc[r]
            m_new = jnp.maximum(m_prev, jnp.max(s2, axis=-1, keepdims=True))
            p = jnp.exp2(s2 - jnp.tile(m_new, (1, tk // HEAD_DIM))).astype(jnp.bfloat16)
            alpha = jnp.exp2(m_prev - m_new)
            acc_sc[r] = jnp.tile(alpha, (1, AUG_DIM // HEAD_DIM)) * acc_sc[r] + jnp.dot(
                p, v, preferred_element_type=jnp.float32)
            m_sc[r] = m_new

    is_masked = jnp.bitwise_and(flags, STEP_MASKED) != 0
    bg = pl.program_id(0) * NSA_KV_HEADS + g
    live = used_ref[(bg * pl.num_programs(2) + s)] != 0
    pl.when(jnp.logical_and(live, jnp.logical_not(is_masked)))(functools.partial(step, False))
    pl.when(jnp.logical_and(live, is_masked))(functools.partial(step, True))

    @pl.when(jnp.bitwise_and(flags, STEP_LAST) != 0)
    def _():
        gate = _branch_gate(gl_ref, ex_ref)
        prev = prev_ref[0]
        for r in range(NSA_REP):
            acc = acc_sc[r]
            cols = slice(r * HEAD_DIM, (r + 1) * HEAD_DIM)
            o_ref[0, :, cols] = prev[:, cols] + gate[:, cols] * (
                acc[:, 0:HEAD_DIM] / acc[:, HEAD_DIM:AUG_DIM])


def selected_attention(q_aug, kv3, gate_logits, prev, sel, used):
    b, t, _ = q_aug.shape
    tq = tk = min(ATT_TILE, t)
    gw = NSA_REP * HEAD_DIM
    qi, kv, fl = _flash_steps(t, tq, tk)
    used = used[:, :, ::8, :][:, :, qi, kv].reshape(-1)

    q_tile = lambda bi, g, s, qi, *_: (bi, qi[s], g)
    kv_tile = lambda col: (lambda bi, g, s, qi, kv, *_: (bi, kv[s], col + g))
    grid_spec = pltpu.PrefetchScalarGridSpec(
        num_scalar_prefetch=4,
        grid=(b, NSA_KV_HEADS, len(qi)),
        in_specs=[pl.BlockSpec((1, tq, NSA_REP * AUG_DIM), q_tile),
                  pl.BlockSpec((1, tk, AUG_DIM), kv_tile(0)),
                  pl.BlockSpec((1, tk, AUG_DIM), kv_tile(NSA_KV_HEADS)),
                  pl.BlockSpec((1, tq, LANES), lambda bi, g, s, qi, *_: (bi, qi[s], 0)),
                  pl.BlockSpec((None, None, LANES, gw), lambda bi, g, s, *_: (1, g, 0, 0)),
                  pl.BlockSpec((1, tq, gw), q_tile),
                  pl.BlockSpec((1, 1, tq, LANES), lambda bi, g, s, qi, *_: (bi, g, qi[s], 0))],
        out_specs=pl.BlockSpec((1, tq, gw), q_tile),
        scratch_shapes=[pltpu.VMEM((NSA_REP, tq, AUG_DIM), jnp.bfloat16),
                        pltpu.VMEM((NSA_REP, tq, HEAD_DIM), jnp.float32),
                        pltpu.VMEM((NSA_REP, tq, AUG_DIM), jnp.float32)],
    )
    return pl.pallas_call(
        functools.partial(_slc_kernel, tq=tq, tk=tk),
        grid_spec=grid_spec,
        out_shape=jax.ShapeDtypeStruct((b, t, NSA_WIDTH), jnp.float32),
        compiler_params=_cparams(("parallel", "parallel", "arbitrary")),
        name="selected_attention",
    )(jnp.asarray(qi), jnp.asarray(kv), jnp.asarray(fl), used, q_aug, kv3, kv3, gate_logits,
      jnp.asarray(_gate_expand_matrix(), dtype=jnp.bfloat16), prev, sel)


def _win_kernel(q_ref, kp_ref, vp_ref, kc_ref, vc_ref, gl_ref, ex_ref, prev_ref,
                o_ref, ssq_ref, *, tq):
    i = pl.program_id(2)
    row = lax.broadcasted_iota(jnp.int32, (tq, tq), 0)
    col = lax.broadcasted_iota(jnp.int32, (tq, tq), 1)
    ok_cur = row >= col
    ok_prev = jnp.logical_and(row + tq - col < WINDOW, i > 0)
    gate = _branch_gate(gl_ref, ex_ref)
    prev = prev_ref[0]
    contract_last = (((1,), (1,)), ((), ()))
    ssq = jnp.zeros((tq, LANES), jnp.float32)
    for r in range(NSA_REP):
        qa = q_ref[0, :, r * AUG_DIM:(r + 1) * AUG_DIM]
        s_p = lax.dot_general(qa, kp_ref[0], contract_last, preferred_element_type=jnp.float32)
        s_c = lax.dot_general(qa, kc_ref[0], contract_last, preferred_element_type=jnp.float32)
        s_p = jnp.where(ok_prev, s_p, MASKED)
        s_c = jnp.where(ok_cur, s_c, MASKED)
        m = jnp.maximum(jnp.max(s_p, axis=-1, keepdims=True), jnp.max(s_c, axis=-1, keepdims=True))
        p_p = jnp.exp2(s_p - m).astype(jnp.bfloat16)
        p_c = jnp.exp2(s_c - m).astype(jnp.bfloat16)
        acc = (jnp.dot(p_p, vp_ref[0], preferred_element_type=jnp.float32)
               + jnp.dot(p_c, vc_ref[0], preferred_element_type=jnp.float32))
        cols = slice(r * HEAD_DIM, (r + 1) * HEAD_DIM)
        tot = prev[:, cols] + gate[:, cols] * (acc[:, 0:HEAD_DIM] / acc[:, HEAD_DIM:AUG_DIM])
        o_ref[0, :, cols] = tot.astype(o_ref.dtype)
        ssq = ssq + _lane_sum_sq(tot)
    ssq_ref[0] = ssq


def window_attention(q_aug, kv3, gate_logits, prev):
    b, t, _ = q_aug.shape
    tq = min(ATT_TILE, t)
    assert WINDOW <= tq
    gw = NSA_REP * HEAD_DIM
    k_col, v_col = 2 * NSA_KV_HEADS, 3 * NSA_KV_HEADS
    q_tile = lambda col: (lambda bi, g, i: (bi, i, col + g))
    kv_prev = lambda col: (lambda bi, g, i: (bi, jnp.maximum(i - 1, 0), col + g))
    return pl.pallas_call(
        functools.partial(_win_kernel, tq=tq),
        grid=(b, NSA_KV_HEADS, t // tq),
        in_specs=[pl.BlockSpec((1, tq, NSA_REP * AUG_DIM), q_tile(0)),
                  pl.BlockSpec((1, tq, AUG_DIM), kv_prev(k_col)),
                  pl.BlockSpec((1, tq, AUG_DIM), kv_prev(v_col)),
                  pl.BlockSpec((1, tq, AUG_DIM), q_tile(k_col)),
                  pl.BlockSpec((1, tq, AUG_DIM), q_tile(v_col)),
                  pl.BlockSpec((1, tq, LANES), lambda bi, g, i: (bi, i, 0)),
                  pl.BlockSpec((None, None, LANES, gw), lambda bi, g, i: (2, g, 0, 0)),
                  pl.BlockSpec((1, tq, gw), q_tile(0))],
        out_specs=[pl.BlockSpec((1, tq, gw), q_tile(0)), pl.BlockSpec((1, tq, LANES), q_tile(0))],
        out_shape=[jax.ShapeDtypeStruct((b, t, NSA_WIDTH), jnp.bfloat16),
                   jax.ShapeDtypeStruct((b, t, NSA_KV_HEADS * LANES), jnp.float32)],
        compiler_params=_cparams(("parallel", "parallel", "parallel")),
        name="window_attention",
    )(q_aug, kv3, kv3, kv3, kv3, gate_logits,
      jnp.asarray(_gate_expand_matrix(), dtype=jnp.bfloat16), prev)


def kernel(x, p, mix_norm, w_in, conv_w, q_gain, k_gain, cmp_pos, cmp_w1, cmp_w2, sgu_gain, w_sp,
           b_sp, mix_out_norm, w_out, mlp_norm, w_mlp_in, w_mlp_out, ple_norm, w_ple_proj,
           w_ple_gate):
    b, t, d = x.shape
    m = b * t
    bf = jnp.bfloat16
    depth = p.shape[0]
    p_rows = p.reshape(depth, m, PLE_DIM)
    w_in_t = jnp.swapaxes(w_in, 1, 2)
    ones = jnp.ones((D_FF,), jnp.float32)
    h2 = x.reshape(m, d)
    hb, ssq = stats_cast(h2)
    for i in range(depth):
        g_in = mix_norm[i]
        z_a = matmul(hb, cast_weight_t(w_in_t, i, g_in, 0, Z_A_COLS, transpose=False), ssq=ssq,
                     w_rows_are_outputs=True).reshape(b, t, Z_A_COLS)
        z_g = matmul(hb, cast_weight_t(w_in_t, i, g_in, Z_G_OFF, Z_G_COLS, transpose=False), ssq=ssq,
                     w_rows_are_outputs=True).reshape(b, t, Z_G_COLS)
        w_gate = cast_weight_t(w_in_t, i, g_in, Z_A_COLS, LANES, keep=N_GATES)
        gate_logits = matmul(hb, w_gate, ssq=ssq, tn=LANES).reshape(b, t, LANES)

        on = mix_out_norm[i]
        mixed_a = conv_mixer(z_a, conv_w[i], on[:CONV_WIDTH])
        kc, vc = compress_kv(z_a, cmp_pos[i], cmp_w1[i], cmp_w2[i], k_gain[i])
        q_aug = q_prep(z_a, q_gain[i])
        o_b, sel, used = cmp_attention(q_aug, kc, vc, gate_logits)
        kv3 = kv_prep(z_a, k_gain[i])
        o_b = selected_attention(q_aug, kv3, gate_logits, o_b, sel, used)
        o_b, ssq_b = window_attention(q_aug, kv3, gate_logits, o_b)
        mixed_c = gmlp_mixer(z_g, sgu_gain[i], w_sp[i], b_sp[i], on[CONV_WIDTH + NSA_WIDTH:])
        g_out = jnp.concatenate([ones[:CONV_WIDTH], on[CONV_WIDTH:CONV_WIDTH + NSA_WIDTH],
                                 ones[:GMLP_WIDTH]])
        mixed = (mixed_a.reshape(m, CONV_WIDTH), o_b.reshape(m, NSA_WIDTH),
                 mixed_c.reshape(m, GMLP_WIDTH))
        h2, hb, ssq = matmul(mixed, cast_weight(w_out, i, g_out), mode="residual", res=h2,
                             ssq=(None, ssq_b.reshape(m, NSA_KV_HEADS * LANES), None),
                             stats=True, tn=512)

        hid = matmul(hb, cast_weight(w_mlp_in, i, mlp_norm[i]), mode="relu2", out_dtype=bf, ssq=ssq)
        h2 = matmul(hid, cast_weight(w_mlp_out, i, ones), mode="residual", res=h2)
        hb, ssq = stats_cast(h2)

        out = matmul(hb, cast_weight(w_ple_gate, i, ple_norm[i]), mode="ple", res=h2, p=p_rows,
                     wp=w_ple_proj, layer=i, ssq=ssq, stats=i + 1 < depth, tn=512)
        h2, hb, ssq = out if i + 1 < depth else (out, None, None)
    return h2.reshape(b, t, d)
```

```python
import functools

import jax
import jax.numpy as jnp
import numpy as np
from jax import lax
from jax.experimental import pallas as pl
from jax.experimental.pallas import tpu as pltpu

D_MODEL = 4096
HEAD_DIM = 128
CONV_WIDTH = D_MODEL // 4
CONV_K = 3
NSA_HEADS = (D_MODEL // 2) // HEAD_DIM
NSA_WIDTH = NSA_HEADS * HEAD_DIM
NSA_KV_HEADS = 4
NSA_REP = NSA_HEADS // NSA_KV_HEADS
NSA_BRANCHES = 3
KV_WIDTH = NSA_KV_HEADS * HEAD_DIM
CMP_LEN = 32
CMP_STRIDE = 16
SLC_BLOCK = 64
SLC_SHIFT = 6
N_SELECT = 16
WINDOW = 512
GMLP_WIDTH = D_MODEL // 4
GMLP_GROUPS = GMLP_WIDTH // HEAD_DIM
GMLP_CHUNK = 128
D_FF = 4 * D_MODEL
PLE_DIM = 256
N_GATES = NSA_HEADS * NSA_BRANCHES
EPS = 1e-6
NEG_INF = -1e30
FORCE_SCORE = 1e9
BELOW_NEG_INF = -3e38

Z_A_COLS = 3 * CONV_WIDTH + NSA_WIDTH + 2 * NSA_BRANCHES * KV_WIDTH
Z_G_COLS = 2 * GMLP_WIDTH
Z_G_OFF = Z_A_COLS + N_GATES
Q_OFF = 3 * CONV_WIDTH
KCMP_OFF = Q_OFF + NSA_WIDTH
VCMP_OFF = KCMP_OFF + KV_WIDTH
KV4_OFF = VCMP_OFF + KV_WIDTH

LANES = 128
VMEM_LIMIT_BYTES = 60 * 1024 * 1024


def _cparams(dims):
    return pltpu.CompilerParams(dimension_semantics=dims, vmem_limit_bytes=VMEM_LIMIT_BYTES)


def _rms(x, gain):
    return x * lax.rsqrt(jnp.mean(x * x, axis=-1, keepdims=True) + EPS) * gain


def _gelu(x):
    return jax.nn.gelu(x)


def _row_scale(ssq, width, n_groups=1):
    tot = ssq[:, 0:LANES]
    for gi in range(1, n_groups):
        tot = tot + ssq[:, gi * LANES:(gi + 1) * LANES]
    return lax.rsqrt(tot * (1.0 / width) + EPS)


def _lane_sum_sq(x):
    return jnp.broadcast_to(jnp.sum(x * x, axis=-1, keepdims=True), (x.shape[0], LANES))


def _stats_kernel(x_ref, hb_ref, ssq_ref):
    x = x_ref[...]
    hb_ref[...] = x.astype(hb_ref.dtype)
    ssq_ref[...] = _lane_sum_sq(x)


def stats_cast(x, tm=512):
    m, d = x.shape
    return pl.pallas_call(
        _stats_kernel,
        grid=(m // tm,),
        in_specs=[pl.BlockSpec((tm, d), lambda i: (i, 0))],
        out_specs=[pl.BlockSpec((tm, d), lambda i: (i, 0)),
                   pl.BlockSpec((tm, LANES), lambda i: (i, 0))],
        out_shape=[jax.ShapeDtypeStruct((m, d), jnp.bfloat16),
                   jax.ShapeDtypeStruct((m, LANES), jnp.float32)],
        compiler_params=_cparams(("parallel",)),
        name="stats_cast",
    )(x)


def _cast_kernel(w_ref, g_ref, o_ref):
    o_ref[...] = (w_ref[...] * g_ref[...]).astype(o_ref.dtype)


def _cast_t_kernel(a_ref, *refs, shift, keep, transpose):
    g_ref, o_ref = refs[-2:]
    x = a_ref[...]
    if shift:
        x = jnp.concatenate([x[shift:, :], refs[0][:shift, :]], axis=0)
    if keep < x.shape[0]:
        row = lax.broadcasted_iota(jnp.int32, x.shape, 0)
        x = jnp.where(row < keep, x, 0.0)
    x = x * g_ref[...]
    o_ref[...] = (x.T if transpose else x).astype(o_ref.dtype)


def cast_weight(w, layer, gain, tk=1024, tn=4096):
    _, kdim, n = w.shape
    tk, tn = min(tk, kdim), min(tn, n)
    assert kdim % tk == 0 and n % tn == 0
    return pl.pallas_call(
        _cast_kernel,
        grid=(kdim // tk, n // tn),
        in_specs=[pl.BlockSpec((None, tk, tn), lambda i, j: (layer, i, j)),
                  pl.BlockSpec((tk, 1), lambda i, j: (i, 0))],
        out_specs=pl.BlockSpec((tk, tn), lambda i, j: (i, j)),
        out_shape=jax.ShapeDtypeStruct((kdim, n), jnp.bfloat16),
        compiler_params=_cparams(("parallel", "parallel")),
        name="cast_weight",
    )(w, gain.reshape(kdim, 1))


def cast_weight_t(w_t, layer, gain, row0, nrows, keep=None, transpose=True, tr=512, tc=2048):
    _, n, kdim = w_t.shape
    tr, tc = min(tr, nrows), min(tc, kdim)
    shift = row0 % tr
    base = row0 // tr
    assert nrows % tr == 0 and kdim % tc == 0 and shift % 8 == 0
    in_specs = [pl.BlockSpec((None, tr, tc), lambda i, j: (layer, base + j, i))]
    if shift:
        head = 8 * pl.cdiv(shift, 8)
        while tr % head:
            head += 8
        in_specs.append(pl.BlockSpec((None, head, tc), lambda i, j: (layer, (base + j + 1) * (tr // head), i)))
    return pl.pallas_call(
        functools.partial(_cast_t_kernel, shift=shift, keep=keep or nrows, transpose=transpose),
        grid=(kdim // tc, nrows // tr),
        in_specs=in_specs + [pl.BlockSpec((1, tc), lambda i, j: (0, i))],
        out_specs=(pl.BlockSpec((tc, tr), lambda i, j: (i, j)) if transpose
                   else pl.BlockSpec((tr, tc), lambda i, j: (j, i))),
        out_shape=jax.ShapeDtypeStruct((kdim, nrows) if transpose else (nrows, kdim), jnp.bfloat16),
        compiler_params=_cparams(("parallel", "parallel")),
        name="cast_weight_t",
    )(*([w_t] * (2 if shift else 1)), gain.reshape(1, kdim))


def _mm_kernel(*refs, mode, nk, n_x, scale_groups, stats, w_rows_are_outputs):
    x_refs, refs = refs[:n_x], refs[n_x:]
    refs = list(refs)
    take = lambda cond: refs.pop(0) if cond else None
    w_ref = refs.pop(0)
    res_ref = take(mode in ("residual", "ple"))
    p_ref, wp_ref = take(mode == "ple"), take(mode == "ple")
    ssq_refs = [take(g > 0) for g in scale_groups]
    o_ref = refs.pop(0)
    hb_ref, ssq_out_ref = take(stats), take(stats)
    tn = o_ref.shape[1]
    j = pl.program_id(1)
    k = pl.program_id(2)

    def scaled(d, n):
        if ssq_refs[n] is None:
            return d
        s = _row_scale(ssq_refs[n][...], x_refs[n].shape[1] * (nk if len(x_refs) == 1 else 1),
                       scale_groups[n])
        return d * jnp.tile(s, (1, tn // LANES))

    def finish(new):
        o_ref[...] = new
        if stats:
            hb_ref[...] = new.astype(hb_ref.dtype)
            ssq = _lane_sum_sq(new)

            @pl.when(j == 0)
            def _():
                ssq_out_ref[...] = ssq

            @pl.when(j > 0)
            def _():
                ssq_out_ref[...] += ssq

    def epilogue(acc):
        if mode == "plain":
            o_ref[...] = acc.astype(o_ref.dtype)
        elif mode == "relu2":
            r = jnp.maximum(acc, 0.0)
            o_ref[...] = (r * r).astype(o_ref.dtype)
        elif mode == "residual":
            finish(res_ref[...] + acc)
        else:
            proj = jnp.dot(p_ref[...].astype(jnp.bfloat16), wp_ref[...].astype(jnp.bfloat16),
                           preferred_element_type=jnp.float32)
            finish(res_ref[...] + jax.nn.sigmoid(acc) * proj)

    def product():
        part, off = None, 0
        for n, x_ref in enumerate(x_refs):
            kx = x_ref.shape[1]
            if w_rows_are_outputs:
                d = lax.dot_general(x_ref[...], w_ref[:, off:off + kx], (((1,), (1,)), ((), ())),
                                    preferred_element_type=jnp.float32)
            else:
                d = jnp.dot(x_ref[...], w_ref[off:off + kx, :], preferred_element_type=jnp.float32)
            d = scaled(d, n)
            part = d if part is None else part + d
            off += kx
        return part

    if nk == 1:
        epilogue(product())
        return
    if mode == "residual":
        @pl.when(k == 0)
        def _():
            o_ref[...] = res_ref[...] + product()

        @pl.when(jnp.logical_and(k > 0, k < nk - 1))
        def _():
            o_ref[...] += product()

        @pl.when(k == nk - 1)
        def _():
            finish(o_ref[...] + product())
        return
    acc_ref, = refs

    @pl.when(k == 0)
    def _():
        acc_ref[...] = product()

    @pl.when(jnp.logical_and(k > 0, k < nk - 1))
    def _():
        acc_ref[...] += product()

    @pl.when(k == nk - 1)
    def _():
        epilogue(acc_ref[...] + product())


def matmul(x, w, mode="plain", out_dtype=jnp.float32, res=None, p=None, wp=None, layer=0,
           ssq=None, stats=False, w_rows_are_outputs=False, tm=1024, tn=1024, tk=4096):
    xs = x if isinstance(x, (tuple, list)) else (x,)
    ssqs = ssq if isinstance(ssq, (tuple, list)) else (ssq,) * len(xs)
    m = xs[0].shape[0]
    kdim, n = w.shape[::-1] if w_rows_are_outputs else w.shape
    assert sum(xi.shape[1] for xi in xs) == kdim and len(ssqs) == len(xs)
    tm, tn, tk = min(tm, m), min(tn, n), min(tk, kdim)
    nk = kdim // tk
    assert m % tm == 0 and n % tn == 0 and kdim % tk == 0 and (nk == 1 or len(xs) == 1)
    assert not stats or mode in ("residual", "ple")
    in_specs = [pl.BlockSpec((tm, xi.shape[1] if len(xs) > 1 else tk), lambda i, j, k: (i, k))
                for xi in xs]
    in_specs.append(pl.BlockSpec((tn, tk), lambda i, j, k: (j, k)) if w_rows_are_outputs
                    else pl.BlockSpec((tk, tn), lambda i, j, k: (k, j)))
    args = [*xs, w]
    if mode in ("residual", "ple"):
        in_specs.append(pl.BlockSpec((tm, tn), lambda i, j, k: (i, j)))
        args.append(res)
    if mode == "ple":
        pd = p.shape[2]
        in_specs += [pl.BlockSpec((None, tm, pd), lambda i, j, k: (layer, i, 0)),
                     pl.BlockSpec((None, pd, tn), lambda i, j, k: (layer, 0, j))]
        args += [p, wp]
    scale_groups = tuple(0 if s is None else s.shape[1] // LANES for s in ssqs)
    for s in ssqs:
        if s is not None:
            in_specs.append(pl.BlockSpec((tm, s.shape[1]), lambda i, j, k: (i, 0)))
            args.append(s)
    tile = pl.BlockSpec((tm, tn), lambda i, j, k: (i, j))
    out_specs, out_shape = tile, jax.ShapeDtypeStruct((m, n), out_dtype)
    if stats:
        out_specs = [tile, tile, pl.BlockSpec((tm, LANES), lambda i, j, k: (i, 0))]
        out_shape = [out_shape, jax.ShapeDtypeStruct((m, n), jnp.bfloat16),
                     jax.ShapeDtypeStruct((m, LANES), jnp.float32)]
    scratch = [pltpu.VMEM((tm, tn), jnp.float32)] if nk > 1 and mode != "residual" else []
    return pl.pallas_call(
        functools.partial(_mm_kernel, mode=mode, nk=nk, n_x=len(xs), scale_groups=scale_groups,
                          stats=stats, w_rows_are_outputs=w_rows_are_outputs),
        grid=(m // tm, n // tn, nk),
        in_specs=in_specs,
        out_specs=out_specs,
        out_shape=out_shape,
        scratch_shapes=scratch,
        compiler_params=_cparams(("parallel", "arbitrary" if stats else "parallel", "arbitrary")),
        name="matmul_" + mode,
    )(*args)


HALO_ROWS = 8


def _conv_kernel(xa_ref, gb_ref, gc_ref, xap_ref, gcp_ref, cw_ref, on_ref, o_ref):
    i = pl.program_id(1)
    inner = gc_ref[0] * xa_ref[0]
    prev = gcp_ref[0] * xap_ref[0]
    prev = jnp.where(i > 0, prev, 0.0)
    tt = inner.shape[0]
    row = lax.broadcasted_iota(jnp.int32, inner.shape, 0)
    m1 = jnp.where(row == 0, prev[HALO_ROWS - 1:HALO_ROWS], pltpu.roll(inner, 1, 0))
    m2 = pltpu.roll(inner, 2, 0)
    m2 = jnp.where(row == 0, prev[HALO_ROWS - 2:HALO_ROWS - 1], m2)
    m2 = jnp.where(row == 1, prev[HALO_ROWS - 1:HALO_ROWS], m2)
    cw = cw_ref[...]
    conv = cw[0:1] * m2 + cw[1:2] * m1 + cw[2:3] * inner
    o_ref[0] = _rms(gb_ref[0] * conv, on_ref[...]).astype(o_ref.dtype)


def conv_mixer(z3, conv_w, on_gain, tt=1024):
    b, t, _ = z3.shape
    c = CONV_WIDTH
    hb = tt // HALO_ROWS
    cur = lambda col: pl.BlockSpec((1, tt, c), lambda bi, i: (bi, i, col))
    halo = lambda col: pl.BlockSpec((1, HALO_ROWS, c),
                                    lambda bi, i: (bi, jnp.maximum(i * hb - 1, 0), col))
    return pl.pallas_call(
        _conv_kernel,
        grid=(b, t // tt),
        in_specs=[cur(0), cur(1), cur(2), halo(0), halo(2),
                  pl.BlockSpec((HALO_ROWS, c), lambda bi, i: (0, 0)),
                  pl.BlockSpec((1, c), lambda bi, i: (0, 0))],
        out_specs=pl.BlockSpec((1, tt, c), lambda bi, i: (bi, i, 0)),
        out_shape=jax.ShapeDtypeStruct((b, t, c), jnp.bfloat16),
        compiler_params=_cparams(("parallel", "parallel")),
        name="conv_mixer",
    )(z3, z3, z3, z3, z3,
      jnp.pad(conv_w, ((0, HALO_ROWS - CONV_K), (0, 0))), on_gain.reshape(1, c))


def _gmlp_kernel(gu_ref, gv_ref, sg_ref, wsp_ref, bexp_ref, on_ref, o_ref, *, n_chunks):
    u = _gelu(gu_ref[0])
    v = _rms(_gelu(gv_ref[0]), sg_ref[...]).astype(jnp.bfloat16)
    ck = GMLP_CHUNK
    r_i = lax.broadcasted_iota(jnp.int32, (ck, ck), 0)
    c_i = lax.broadcasted_iota(jnp.int32, (ck, ck), 1)
    bexp = bexp_ref[...]
    rows = []
    for c in range(n_chunks):
        cols = []
        for g in range(GMLP_GROUPS):
            w = jnp.where(r_i >= c_i, wsp_ref[g], 0.0).astype(jnp.bfloat16)
            vg = v[c * ck:(c + 1) * ck, g * HEAD_DIM:(g + 1) * HEAD_DIM]
            cols.append(jnp.dot(w, vg, preferred_element_type=jnp.float32))
        rows.append(jnp.concatenate(cols, axis=1) + bexp)
    spatial = jnp.concatenate(rows, axis=0) if n_chunks > 1 else rows[0]
    o_ref[0] = _rms(u * spatial, on_ref[...]).astype(o_ref.dtype)


def gmlp_mixer(z3, sgu_gain, w_sp, b_sp, on_gain, tt=512):
    b, t, _ = z3.shape
    c = GMLP_WIDTH
    bexp = jnp.repeat(b_sp.T, HEAD_DIM, axis=1)
    const2 = lambda shape: pl.BlockSpec(shape, lambda bi, i: (0,) * len(shape))
    return pl.pallas_call(
        functools.partial(_gmlp_kernel, n_chunks=tt // GMLP_CHUNK),
        grid=(b, t // tt),
        in_specs=[pl.BlockSpec((1, tt, c), lambda bi, i: (bi, i, 0)),
                  pl.BlockSpec((1, tt, c), lambda bi, i: (bi, i, 1)),
                  const2((1, c)), const2((GMLP_GROUPS, GMLP_CHUNK, GMLP_CHUNK)),
                  const2((GMLP_CHUNK, c)), const2((1, c))],
        out_specs=pl.BlockSpec((1, tt, c), lambda bi, i: (bi, i, 0)),
        out_shape=jax.ShapeDtypeStruct((b, t, c), jnp.bfloat16),
        compiler_params=_cparams(("parallel", "parallel")),
        name="gmlp_mixer",
    )(z3, z3, sgu_gain.reshape(1, c), w_sp, bexp, on_gain.reshape(1, c))


ATT_TILE = 512
KV_TILE_SHIFT = 9
NARROW_KEYS = 128
AUG_DIM = 2 * HEAD_DIM
POS_LANE = 64
ONE_LANE = 70
FEAT_END = 73
MASK_BIG = 2.0 ** 100
MASKED = -MASK_BIG
M_INIT = NEG_INF
LOG2E = 1.4426950408889634


def _key_features(pos, with_block_onehot):
    lane = lax.broadcasted_iota(jnp.int32, pos.shape, 1)
    blk = lax.shift_right_logical(pos, SLC_SHIFT)
    hi = lax.shift_left(blk, SLC_SHIFT).astype(jnp.float32)
    lo = jnp.bitwise_and(pos, SLC_BLOCK - 1).astype(jnp.float32)
    posf = jnp.where(jnp.bitwise_and(lane, 1) == 0, hi, lo)
    tail = jnp.where(lane < ONE_LANE, posf, jnp.where(lane < FEAT_END, 1.0, 0.0))
    head = jnp.where(lane == blk, MASK_BIG, 0.0) if with_block_onehot else 0.0
    return jnp.where(lane < POS_LANE, head, tail)


def _kvprep_kernel(z_ref, kg_ref, o_ref, *, tt):
    i = pl.program_id(1)
    z = z_ref[0]
    pos = i * tt + lax.broadcasted_iota(jnp.int32, (tt, HEAD_DIM), 0)
    kfeat = (_key_features(pos, True).astype(o_ref.dtype), _key_features(pos, False).astype(o_ref.dtype))
    vfeat = jnp.ones((tt, HEAD_DIM), o_ref.dtype)
    for part in range(4):
        for g in range(NSA_KV_HEADS):
            src = part * KV_WIDTH + g * HEAD_DIM
            dst = (part * NSA_KV_HEADS + g) * AUG_DIM
            x = z[:, src:src + HEAD_DIM]
            if part % 2 == 0:
                x = _rms(x, kg_ref[1 + part // 2:2 + part // 2, :])
            o_ref[0, :, dst:dst + HEAD_DIM] = x.astype(o_ref.dtype)
            o_ref[0, :, dst + HEAD_DIM:dst + AUG_DIM] = kfeat[part // 2] if part % 2 == 0 else vfeat


def kv_prep(z3, k_gain, tt=512):
    b, t, _ = z3.shape
    w = 4 * KV_WIDTH
    wo = 4 * NSA_KV_HEADS * AUG_DIM
    kg = jnp.pad(k_gain, ((0, 8 - NSA_BRANCHES), (0, 0)))
    return pl.pallas_call(
        functools.partial(_kvprep_kernel, tt=tt),
        grid=(b, t // tt),
        in_specs=[pl.BlockSpec((1, tt, w), lambda bi, i: (bi, i, KV4_OFF // w)),
                  pl.BlockSpec((8, HEAD_DIM), lambda bi, i: (0, 0))],
        out_specs=pl.BlockSpec((1, tt, wo), lambda bi, i: (bi, i, 0)),
        out_shape=jax.ShapeDtypeStruct((b, t, wo), jnp.bfloat16),
        compiler_params=_cparams(("parallel", "parallel")),
        name="kv_prep",
    )(z3, kg)


def _compress_kernel(k_ref, v_ref, w1_ref, w2_ref, pe_ref, kg_ref, kc_ref, vc_ref, *, n_chunk):
    def one(src_ref, which):
        ya = jnp.zeros((n_chunk, HEAD_DIM), jnp.float32)
        yb = jnp.zeros((n_chunk, HEAD_DIM), jnp.float32)
        bias = jnp.zeros((8, HEAD_DIM), jnp.float32)
        for l in range(CMP_STRIDE):
            xl = src_ref[0, pl.ds(l, n_chunk, stride=CMP_STRIDE), :].astype(jnp.bfloat16)
            ya += jnp.dot(xl, w1_ref[which, l], preferred_element_type=jnp.float32)
            yb += jnp.dot(xl, w1_ref[which, CMP_STRIDE + l], preferred_element_type=jnp.float32)
        for l in range(CMP_LEN):
            pe_l = jnp.broadcast_to(pe_ref[which, l:l + 1, :], (8, HEAD_DIM)).astype(jnp.bfloat16)
            bias += jnp.dot(pe_l, w1_ref[which, l], preferred_element_type=jnp.float32)
        hid = _gelu(ya + pltpu.roll(yb, n_chunk - 1, 0) + bias[0:1])
        return jnp.dot(hid.astype(jnp.bfloat16), w2_ref[which], preferred_element_type=jnp.float32)

    kc_ref[0, 0, :, 0:HEAD_DIM] = _rms(one(k_ref, 0), kg_ref[0:1, :]).astype(kc_ref.dtype)
    n = lax.broadcasted_iota(jnp.int32, (n_chunk, HEAD_DIM), 0)
    kc_ref[0, 0, :, HEAD_DIM:AUG_DIM] = _key_features(n * CMP_STRIDE + CMP_LEN - 1, False).astype(kc_ref.dtype)
    vc_ref[0, 0] = one(v_ref, 1).astype(vc_ref.dtype)


def compress_kv(z3, cmp_pos, cmp_w1, cmp_w2, k_gain):
    b, t, _ = z3.shape
    n_chunk = t // CMP_STRIDE
    kg = jnp.pad(k_gain, ((0, 8 - NSA_BRANCHES), (0, 0)))
    const = lambda shape: pl.BlockSpec(shape, lambda bi, g: (0,) * len(shape))
    out = lambda width: jax.ShapeDtypeStruct((b, NSA_KV_HEADS, n_chunk, width), jnp.bfloat16)
    tile = lambda width: pl.BlockSpec((1, 1, n_chunk, width), lambda bi, g: (bi, g, 0, 0))
    return pl.pallas_call(
        functools.partial(_compress_kernel, n_chunk=n_chunk),
        grid=(b, NSA_KV_HEADS),
        in_specs=[pl.BlockSpec((1, t, HEAD_DIM), lambda bi, g: (bi, 0, KCMP_OFF // HEAD_DIM + g)),
                  pl.BlockSpec((1, t, HEAD_DIM), lambda bi, g: (bi, 0, VCMP_OFF // HEAD_DIM + g)),
                  const((2, CMP_LEN, HEAD_DIM, HEAD_DIM)), const((2, HEAD_DIM, HEAD_DIM)),
                  const((2, CMP_LEN, HEAD_DIM)), const((8, HEAD_DIM))],
        out_specs=[tile(AUG_DIM), tile(HEAD_DIM)],
        out_shape=[out(AUG_DIM), out(HEAD_DIM)],
        compiler_params=_cparams(("parallel", "parallel")),
        name="compress_kv",
    )(z3, z3, cmp_w1.astype(jnp.bfloat16), cmp_w2.astype(jnp.bfloat16), cmp_pos, kg)


def _alibi_slopes():
    h = np.arange(1, NSA_HEADS + 1, dtype=np.float32)
    return np.power(np.float32(2.0), -8.0 * h / NSA_HEADS).astype(np.float32)


def _overlap_matrix(n_chunk, n_slc):
    n_cmp = n_chunk - CMP_LEN // CMP_STRIDE + 1
    jc = np.arange(n_cmp)[:, None] * CMP_STRIDE
    bs = np.arange(n_slc)[None, :] * SLC_BLOCK
    ov = np.clip(np.minimum(jc + CMP_LEN, bs + SLC_BLOCK) - np.maximum(jc, bs), 0, None)
    out = np.zeros((n_chunk, LANES), np.float32)
    out[:n_cmp, :n_slc] = ov.astype(np.float32) / CMP_LEN
    return out


def _gate_expand_matrix():
    m = np.zeros((NSA_BRANCHES, NSA_KV_HEADS, LANES, NSA_REP * HEAD_DIM), np.float32)
    for br in range(NSA_BRANCHES):
        for g in range(NSA_KV_HEADS):
            for r in range(NSA_REP):
                m[br, g, (g * NSA_REP + r) * NSA_BRANCHES + br, r * HEAD_DIM:(r + 1) * HEAD_DIM] = 1.0
    return m


def _branch_gate(gl_ref, ex_ref):
    sig = jax.nn.sigmoid(gl_ref[0])
    hi = sig.astype(jnp.bfloat16)
    lo = (sig - hi.astype(jnp.float32)).astype(jnp.bfloat16)
    ex = ex_ref[...]
    return (jnp.dot(hi, ex, preferred_element_type=jnp.float32)
            + jnp.dot(lo, ex, preferred_element_type=jnp.float32))


def _cmp_kernel(slopes_ref, q_ref, kc_ref, vc_ref, qg_ref, ov_ref, gl_ref, ex_ref, o_ref, sel_ref,
                used_ref, *, tq, n_chunk, n_slc):
    g = pl.program_id(1)
    i = pl.program_id(2)
    gate = _branch_gate(gl_ref, ex_ref)
    n_cmp = n_chunk - CMP_LEN // CMP_STRIDE + 1
    kc = kc_ref[0, 0]
    vc = vc_ref[0, 0]
    t = i * tq + lax.broadcasted_iota(jnp.int32, (tq, n_chunk), 0)
    n = lax.broadcasted_iota(jnp.int32, (tq, n_chunk), 1)
    mask = jnp.logical_and(t >= n * CMP_STRIDE + CMP_LEN - 1, n < n_cmp)
    psum = jnp.zeros((tq, n_chunk), jnp.float32)
    for r, (qn, feat) in enumerate(_q_aug(q_ref, qg_ref, slopes_ref, g, i * tq, None)):
        s2 = lax.dot_general(jnp.concatenate([qn, feat], axis=1), kc, (((1,), (1,)), ((), ())),
                             preferred_element_type=jnp.float32)
        s2 = jnp.where(mask, s2, MASKED)
        mx = jnp.maximum(jnp.max(s2, axis=-1, keepdims=True), M_INIT)
        e = jnp.exp2(s2 - mx)
        den = jnp.sum(e, axis=-1, keepdims=True)
        p = e / jnp.where(den > 0.0, den, 1.0)
        psum += p
        o_ref[0, :, r * HEAD_DIM:(r + 1) * HEAD_DIM] = gate[:, r * HEAD_DIM:(r + 1) * HEAD_DIM] * jnp.dot(
            p.astype(jnp.bfloat16), vc, preferred_element_type=jnp.float32)

    ov = ov_ref[...]
    imp = jnp.zeros((tq, LANES), jnp.float32)
    rem = psum
    for _ in range(3):
        part = rem.astype(jnp.bfloat16)
        imp += jnp.dot(part, ov, preferred_element_type=jnp.float32)
        rem = rem - part.astype(jnp.float32)

    tb = i * tq + lax.broadcasted_iota(jnp.int32, (tq, LANES), 0)
    blk = lax.broadcasted_iota(jnp.int32, (tq, LANES), 1)
    qblk = lax.shift_right_logical(tb, SLC_SHIFT)
    forced = (blk == 0) | (blk == qblk) | (blk == qblk - 1)
    score = jnp.where(forced, FORCE_SCORE, jnp.where(blk <= qblk, imp, NEG_INF))
    score = jnp.where(blk < n_slc, score, BELOW_NEG_INF)
    st = score.T[0:n_slc]
    n_tiles = n_slc // 8
    tiles = [st[8 * j:8 * j + 8] for j in range(n_tiles)]
    sub8 = lax.broadcasted_iota(jnp.int32, (8, tq), 0)
    cnt = [jnp.zeros((8, tq), jnp.float32) for _ in range(n_tiles)]
    for m in range(n_slc):
        row = st[m:m + 1, :]
        own, off = divmod(m, 8)
        for j in range(n_tiles):
            if j < own:
                inc = jnp.where(row > tiles[j], 1.0, 0.0)
            elif j > own:
                inc = jnp.where(row >= tiles[j], 1.0, 0.0)
            else:
                later = jnp.where(sub8 > off, 1.0, 0.0)
                inc = (jnp.where(row >= tiles[j], later, 0.0)
                       + jnp.where(row > tiles[j], 1.0 - later, 0.0))
            cnt[j] = cnt[j] + inc
    keep = float(min(N_SELECT, n_slc))
    sel_t = jnp.concatenate([jnp.where(c < keep, 1.0, 0.0) for c in cnt]
                            + [jnp.zeros((LANES - n_slc, tq), jnp.float32)] * (n_slc < LANES), axis=0)
    sel = sel_t.T
    sel_ref[0, 0] = sel.astype(sel_ref.dtype)
    picked = jnp.broadcast_to(jnp.max(sel, axis=0, keepdims=True), (8, LANES)).astype(jnp.bfloat16)
    blk_i = lax.broadcasted_iota(jnp.int32, (LANES, LANES), 0)
    tile_i = lax.broadcasted_iota(jnp.int32, (LANES, LANES), 1)
    group = jnp.where(lax.shift_right_logical(blk_i, KV_TILE_SHIFT - SLC_SHIFT) == tile_i, 1.0, 0.0)
    used = jnp.dot(picked, group.astype(jnp.bfloat16), preferred_element_type=jnp.float32)
    far0 = jnp.where(jnp.logical_and(tile_i == 0, jnp.logical_and(
        blk_i >= NARROW_KEYS // SLC_BLOCK, blk_i < ATT_TILE // SLC_BLOCK)), 1.0, 0.0)
    far = jnp.dot(picked, far0.astype(jnp.bfloat16), preferred_element_type=jnp.float32)
    lane8 = lax.broadcasted_iota(jnp.int32, (8, LANES), 1)
    narrow = jnp.logical_and(lane8 == 0, far < 0.5)
    used_ref[0, 0] = jnp.where(used > 0.5, jnp.where(narrow, 2, 1), 0).astype(jnp.int32)


def cmp_attention(z3, kc, vc, q_gain, gate_logits):
    b, t, _ = z3.shape
    tq = min(ATT_TILE, t)
    n_chunk = t // CMP_STRIDE
    n_slc = t // SLC_BLOCK
    assert n_slc <= LANES
    gw = NSA_REP * HEAD_DIM
    grid_spec = pltpu.PrefetchScalarGridSpec(
        num_scalar_prefetch=1,
        grid=(b, NSA_KV_HEADS, t // tq),
        in_specs=[pl.BlockSpec((1, tq, gw), lambda bi, g, i, s: (bi, i, Q_OFF // gw + g)),
                  pl.BlockSpec((1, 1, n_chunk, AUG_DIM), lambda bi, g, i, s: (bi, g, 0, 0)),
                  pl.BlockSpec((1, 1, n_chunk, HEAD_DIM), lambda bi, g, i, s: (bi, g, 0, 0)),
                  pl.BlockSpec((1, HEAD_DIM), lambda bi, g, i, s: (0, 0)),
                  pl.BlockSpec((n_chunk, LANES), lambda bi, g, i, s: (0, 0)),
                  pl.BlockSpec((1, tq, LANES), lambda bi, g, i, s: (bi, i, 0)),
                  pl.BlockSpec((None, None, LANES, gw), lambda bi, g, i, s: (0, g, 0, 0))],
        out_specs=[pl.BlockSpec((1, tq, gw), lambda bi, g, i, s: (bi, i, g)),
                   pl.BlockSpec((1, 1, tq, LANES), lambda bi, g, i, s: (bi, g, i, 0)),
                   pl.BlockSpec((1, 1, 8, LANES), lambda bi, g, i, s: (bi, g, i, 0))],
    )
    return pl.pallas_call(
        functools.partial(_cmp_kernel, tq=tq, n_chunk=n_chunk, n_slc=n_slc),
        grid_spec=grid_spec,
        out_shape=[jax.ShapeDtypeStruct((b, t, NSA_WIDTH), jnp.float32),
                   jax.ShapeDtypeStruct((b, NSA_KV_HEADS, t, LANES), jnp.bfloat16),
                   jax.ShapeDtypeStruct((b, NSA_KV_HEADS, t // tq * 8, LANES), jnp.int32)],
        compiler_params=_cparams(("parallel", "parallel", "parallel")),
        name="cmp_attention",
    )(jnp.asarray(_alibi_slopes()), z3, kc, vc, q_gain.reshape(1, HEAD_DIM),
      jnp.asarray(_overlap_matrix(n_chunk, n_slc), dtype=jnp.bfloat16), gate_logits,
      jnp.asarray(_gate_expand_matrix(), dtype=jnp.bfloat16))


STEP_FIRST, STEP_LAST, STEP_MASKED = 1, 2, 4


def _flash_steps(t, tq, tk):
    qi, kv, fl = [], [], []
    for i in range(t // tq):
        lo, hi = i * tq, i * tq + tq - 1
        n_tiles = hi // tk + 1
        for j in range(n_tiles):
            flag = (STEP_FIRST if j == 0 else 0) | (STEP_LAST if j == n_tiles - 1 else 0)
            flag |= 0 if j * tk + tk - 1 <= lo else STEP_MASKED
            qi.append(i), kv.append(j), fl.append(flag)
    return [np.asarray(a, np.int32) for a in (qi, kv, fl)]


def _split3(x):
    a = x.astype(jnp.bfloat16).astype(jnp.float32)
    b = (x - a).astype(jnp.bfloat16).astype(jnp.float32)
    c = (x - a - b).astype(jnp.bfloat16).astype(jnp.float32)
    return a, b, c


def _q_aug(q_ref, qg_ref, slopes_ref, g, t_start, sel_m1):
    q = q_ref[0]
    tq = q.shape[0]
    lane = lax.broadcasted_iota(jnp.int32, (tq, HEAD_DIM), 1)
    if sel_m1 is None:
        sel_m1 = jnp.zeros((tq, HEAD_DIM), jnp.float32)
    t0 = t_start.astype(jnp.float32)
    out = []
    for r in range(NSA_REP):
        qn = _rms(q[:, r * HEAD_DIM:(r + 1) * HEAD_DIM], qg_ref[...]) * (HEAD_DIM ** -0.5 * LOG2E)
        slope2 = slopes_ref[g * NSA_REP + r] * LOG2E
        a, b, c = _split3(jnp.where(lane < ONE_LANE, slope2, -slope2 * t0))
        k3 = jnp.where(lane < ONE_LANE, lax.shift_right_logical(lane - POS_LANE, 1), lane - ONE_LANE)
        term = jnp.where(k3 == 0, a, jnp.where(k3 == 1, b, c))
        feat = jnp.where(lane < POS_LANE, sel_m1, jnp.where(lane < FEAT_END, term, 0.0))
        out.append((qn.astype(jnp.bfloat16), feat.astype(jnp.bfloat16)))
    return out


def _slc_kernel(slopes_ref, qi_ref, kv_ref, fl_ref, used_ref, q_ref, k_ref, v_ref, qg_ref, gl_ref,
                ex_ref, prev_ref, sel_ref, o_ref, q_sc, m_sc, acc_sc, *, tq, tk):
    g = pl.program_id(1)
    s = pl.program_id(2)
    i = qi_ref[s]
    kv_idx = kv_ref[s]
    flags = fl_ref[s]

    @pl.when(jnp.bitwise_and(flags, STEP_FIRST) != 0)
    def _():
        sel_m1 = sel_ref[0, 0].astype(jnp.float32) - 1.0
        for r, (qn, feat) in enumerate(_q_aug(q_ref, qg_ref, slopes_ref, g, i * tq, sel_m1)):
            q_sc[r, :, 0:HEAD_DIM] = qn
            q_sc[r, :, HEAD_DIM:AUG_DIM] = feat
        m_sc[...] = jnp.full(m_sc.shape, M_INIT, jnp.float32)
        acc_sc[...] = jnp.zeros(acc_sc.shape, jnp.float32)

    def step(masked, nk=tk):
        k = k_ref[0, 0:nk, :]
        v = v_ref[0, 0:nk, :]
        if masked:
            t = i * tq + lax.broadcasted_iota(jnp.int32, (tq, nk), 0)
            ok = t >= kv_idx * tk + lax.broadcasted_iota(jnp.int32, (tq, nk), 1)
        for r in range(NSA_REP):
            s2 = lax.dot_general(q_sc[r], k, (((1,), (1,)), ((), ())),
                                 preferred_element_type=jnp.float32)
            if masked:
                s2 = jnp.where(ok, s2, MASKED)
            m_prev = m_sc[r]
            m_new = jnp.maximum(m_prev, jnp.max(s2, axis=-1, keepdims=True))
            p = jnp.exp2(s2 - jnp.tile(m_new, (1, nk // HEAD_DIM))).astype(jnp.bfloat16)
            alpha = jnp.exp2(m_prev - m_new)
            acc_sc[r] = jnp.tile(alpha, (1, AUG_DIM // HEAD_DIM)) * acc_sc[r] + jnp.dot(
                p, v, preferred_element_type=jnp.float32)
            m_sc[r] = m_new

    is_masked = jnp.bitwise_and(flags, STEP_MASKED) != 0
    bg = pl.program_id(0) * NSA_KV_HEADS + g
    use = used_ref[(bg * pl.num_programs(2) + s)]
    narrow = jnp.logical_and(use == 2, jnp.logical_not(is_masked))
    whole = jnp.logical_and(use != 0, jnp.logical_not(narrow))
    pl.when(jnp.logical_and(whole, jnp.logical_not(is_masked)))(functools.partial(step, False))
    pl.when(jnp.logical_and(whole, is_masked))(functools.partial(step, True))
    pl.when(narrow)(functools.partial(step, False, min(NARROW_KEYS, tk)))

    @pl.when(jnp.bitwise_and(flags, STEP_LAST) != 0)
    def _():
        gate = _branch_gate(gl_ref, ex_ref)
        prev = prev_ref[0]
        for r in range(NSA_REP):
            acc = acc_sc[r]
            cols = slice(r * HEAD_DIM, (r + 1) * HEAD_DIM)
            o_ref[0, :, cols] = prev[:, cols] + gate[:, cols] * (
                acc[:, 0:HEAD_DIM] / acc[:, HEAD_DIM:AUG_DIM])


def selected_attention(z3, kv3, q_gain, gate_logits, prev, sel, used):
    b, t, _ = z3.shape
    tq = tk = min(ATT_TILE, t)
    gw = NSA_REP * HEAD_DIM
    qi, kv, fl = _flash_steps(t, tq, tk)
    used = used[:, :, ::8, :][:, :, qi, kv].reshape(-1)

    q_tile = lambda col: (lambda bi, g, s, sl, qi, *_: (bi, qi[s], col + g))
    kv_tile = lambda col: (lambda bi, g, s, sl, qi, kv, *_: (bi, kv[s], col + g))
    grid_spec = pltpu.PrefetchScalarGridSpec(
        num_scalar_prefetch=5,
        grid=(b, NSA_KV_HEADS, len(qi)),
        in_specs=[pl.BlockSpec((1, tq, gw), q_tile(Q_OFF // gw)),
                  pl.BlockSpec((1, tk, AUG_DIM), kv_tile(0)),
                  pl.BlockSpec((1, tk, AUG_DIM), kv_tile(NSA_KV_HEADS)),
                  pl.BlockSpec((1, HEAD_DIM), lambda bi, g, s, *_: (0, 0)),
                  pl.BlockSpec((1, tq, LANES), lambda bi, g, s, sl, qi, *_: (bi, qi[s], 0)),
                  pl.BlockSpec((None, None, LANES, gw), lambda bi, g, s, *_: (1, g, 0, 0)),
                  pl.BlockSpec((1, tq, gw), q_tile(0)),
                  pl.BlockSpec((1, 1, tq, LANES), lambda bi, g, s, sl, qi, *_: (bi, g, qi[s], 0))],
        out_specs=pl.BlockSpec((1, tq, gw), q_tile(0)),
        scratch_shapes=[pltpu.VMEM((NSA_REP, tq, AUG_DIM), jnp.bfloat16),
                        pltpu.VMEM((NSA_REP, tq, HEAD_DIM), jnp.float32),
                        pltpu.VMEM((NSA_REP, tq, AUG_DIM), jnp.float32)],
    )
    return pl.pallas_call(
        functools.partial(_slc_kernel, tq=tq, tk=tk),
        grid_spec=grid_spec,
        out_shape=jax.ShapeDtypeStruct((b, t, NSA_WIDTH), jnp.float32),
        compiler_params=_cparams(("parallel", "parallel", "arbitrary")),
        name="selected_attention",
    )(jnp.asarray(_alibi_slopes()), jnp.asarray(qi), jnp.asarray(kv), jnp.asarray(fl), used,
      z3, kv3, kv3, q_gain.reshape(1, HEAD_DIM), gate_logits,
      jnp.asarray(_gate_expand_matrix(), dtype=jnp.bfloat16), prev, sel)


def _win_kernel(slopes_ref, q_ref, kp_ref, vp_ref, kc_ref, vc_ref, qg_ref, gl_ref, ex_ref, prev_ref,
                o_ref, ssq_ref, *, tq):
    g = pl.program_id(1)
    i = pl.program_id(2)
    row = lax.broadcasted_iota(jnp.int32, (tq, tq), 0)
    col = lax.broadcasted_iota(jnp.int32, (tq, tq), 1)
    ok_cur = row >= col
    ok_prev = jnp.logical_and(row + tq - col < WINDOW, i > 0)
    gate = _branch_gate(gl_ref, ex_ref)
    prev = prev_ref[0]
    contract_last = (((1,), (1,)), ((), ()))
    ssq = jnp.zeros((tq, LANES), jnp.float32)
    for r, (qn, feat) in enumerate(_q_aug(q_ref, qg_ref, slopes_ref, g, i * tq, None)):
        qa = jnp.concatenate([qn, feat], axis=1)
        s_p = lax.dot_general(qa, kp_ref[0], contract_last, preferred_element_type=jnp.float32)
        s_c = lax.dot_general(qa, kc_ref[0], contract_last, preferred_element_type=jnp.float32)
        s_p = jnp.where(ok_prev, s_p, MASKED)
        s_c = jnp.where(ok_cur, s_c, MASKED)
        m = jnp.maximum(jnp.max(s_p, axis=-1, keepdims=True), jnp.max(s_c, axis=-1, keepdims=True))
        p_p = jnp.exp2(s_p - m).astype(jnp.bfloat16)
        p_c = jnp.exp2(s_c - m).astype(jnp.bfloat16)
        acc = (jnp.dot(p_p, vp_ref[0], preferred_element_type=jnp.float32)
               + jnp.dot(p_c, vc_ref[0], preferred_element_type=jnp.float32))
        cols = slice(r * HEAD_DIM, (r + 1) * HEAD_DIM)
        tot = prev[:, cols] + gate[:, cols] * (acc[:, 0:HEAD_DIM] / acc[:, HEAD_DIM:AUG_DIM])
        o_ref[0, :, cols] = tot.astype(o_ref.dtype)
        ssq = ssq + _lane_sum_sq(tot)
    ssq_ref[0] = ssq


def window_attention(z3, kv3, q_gain, gate_logits, prev):
    b, t, _ = z3.shape
    tq = min(ATT_TILE, t)
    assert WINDOW <= tq
    gw = NSA_REP * HEAD_DIM
    k_col, v_col = 2 * NSA_KV_HEADS, 3 * NSA_KV_HEADS
    q_tile = lambda col: (lambda bi, g, i, sl: (bi, i, col + g))
    kv_prev = lambda col: (lambda bi, g, i, sl: (bi, jnp.maximum(i - 1, 0), col + g))
    grid_spec = pltpu.PrefetchScalarGridSpec(
        num_scalar_prefetch=1,
        grid=(b, NSA_KV_HEADS, t // tq),
        in_specs=[pl.BlockSpec((1, tq, gw), q_tile(Q_OFF // gw)),
                  pl.BlockSpec((1, tq, AUG_DIM), kv_prev(k_col)),
                  pl.BlockSpec((1, tq, AUG_DIM), kv_prev(v_col)),
                  pl.BlockSpec((1, tq, AUG_DIM), q_tile(k_col)),
                  pl.BlockSpec((1, tq, AUG_DIM), q_tile(v_col)),
                  pl.BlockSpec((1, HEAD_DIM), lambda bi, g, i, sl: (0, 0)),
                  pl.BlockSpec((1, tq, LANES), lambda bi, g, i, sl: (bi, i, 0)),
                  pl.BlockSpec((None, None, LANES, gw), lambda bi, g, i, sl: (2, g, 0, 0)),
                  pl.BlockSpec((1, tq, gw), q_tile(0))],
        out_specs=[pl.BlockSpec((1, tq, gw), q_tile(0)), pl.BlockSpec((1, tq, LANES), q_tile(0))],
    )
    return pl.pallas_call(
        functools.partial(_win_kernel, tq=tq),
        grid_spec=grid_spec,
        out_shape=[jax.ShapeDtypeStruct((b, t, NSA_WIDTH), jnp.bfloat16),
                   jax.ShapeDtypeStruct((b, t, NSA_KV_HEADS * LANES), jnp.float32)],
        compiler_params=_cparams(("parallel", "parallel", "parallel")),
        name="window_attention",
    )(jnp.asarray(_alibi_slopes()), z3, kv3, kv3, kv3, kv3, q_gain.reshape(1, HEAD_DIM), gate_logits,
      jnp.asarray(_gate_expand_matrix(), dtype=jnp.bfloat16), prev)


def kernel(x, p, mix_norm, w_in, conv_w, q_gain, k_gain, cmp_pos, cmp_w1, cmp_w2, sgu_gain, w_sp,
           b_sp, mix_out_norm, w_out, mlp_norm, w_mlp_in, w_mlp_out, ple_norm, w_ple_proj,
           w_ple_gate):
    b, t, d = x.shape
    m = b * t
    bf = jnp.bfloat16
    depth = p.shape[0]
    p_rows = p.reshape(depth, m, PLE_DIM)
    w_in_t = jnp.swapaxes(w_in, 1, 2)
    ones = jnp.ones((D_FF,), jnp.float32)
    h2 = x.reshape(m, d)
    hb, ssq = stats_cast(h2)
    for i in range(depth):
        g_in = mix_norm[i]
        z_a = matmul(hb, cast_weight_t(w_in_t, i, g_in, 0, Z_A_COLS, transpose=False), ssq=ssq,
                     w_rows_are_outputs=True).reshape(b, t, Z_A_COLS)
        z_g = matmul(hb, cast_weight_t(w_in_t, i, g_in, Z_G_OFF, Z_G_COLS, transpose=False), ssq=ssq,
                     w_rows_are_outputs=True).reshape(b, t, Z_G_COLS)
        w_gate = cast_weight_t(w_in_t, i, g_in, Z_A_COLS, LANES, keep=N_GATES)
        gate_logits = matmul(hb, w_gate, ssq=ssq, tn=LANES).reshape(b, t, LANES)

        on = mix_out_norm[i]
        mixed_a = conv_mixer(z_a, conv_w[i], on[:CONV_WIDTH])
        kc, vc = compress_kv(z_a, cmp_pos[i], cmp_w1[i], cmp_w2[i], k_gain[i])
        o_b, sel, used = cmp_attention(z_a, kc, vc, q_gain[i], gate_logits)
        kv3 = kv_prep(z_a, k_gain[i])
        o_b = selected_attention(z_a, kv3, q_gain[i], gate_logits, o_b, sel, used)
        o_b, ssq_b = window_attention(z_a, kv3, q_gain[i], gate_logits, o_b)
        mixed_c = gmlp_mixer(z_g, sgu_gain[i], w_sp[i], b_sp[i], on[CONV_WIDTH + NSA_WIDTH:])
        g_out = jnp.concatenate([ones[:CONV_WIDTH], on[CONV_WIDTH:CONV_WIDTH + NSA_WIDTH],
                                 ones[:GMLP_WIDTH]])
        mixed = (mixed_a.reshape(m, CONV_WIDTH), o_b.reshape(m, NSA_WIDTH),
                 mixed_c.reshape(m, GMLP_WIDTH))
        h2, hb, ssq = matmul(mixed, cast_weight(w_out, i, g_out), mode="residual", res=h2,
                             ssq=(None, ssq_b.reshape(m, NSA_KV_HEADS * LANES), None),
                             stats=True, tn=512)

        hid = matmul(hb, cast_weight(w_mlp_in, i, mlp_norm[i]), mode="relu2", out_dtype=bf, ssq=ssq)
        h2 = matmul(hid, cast_weight(w_mlp_out, i, ones), mode="residual", res=h2)
        hb, ssq = stats_cast(h2)

        out = matmul(hb, cast_weight(w_ple_gate, i, ple_norm[i]), mode="ple", res=h2, p=p_rows,
                     wp=w_ple_proj, layer=i, ssq=ssq, stats=i + 1 < depth, tn=512)
        h2, hb, ssq = out if i + 1 < depth else (out, None, None)
    return h2.reshape(b, t, d)
```
